```python
import jax, jax.numpy as jnp
from jax import lax
import numpy as np

D_MODEL = 1024
BATCH = 4
SEQ = 8192
DEPTH = 2

N_MIXERS = 2
MEM_LEN = 256
MEM_HEADS = 4
MEM_HEAD_DIM = 64
MEM_WIDTH = MEM_HEADS * MEM_HEAD_DIM
TM_WIDTH = D_MODEL - MEM_WIDTH
TM_HEAD_DIM = 128
TM_HEADS = TM_WIDTH // TM_HEAD_DIM
CONV_K = 4
DELTA_CHUNK = 64
GMLP_CHUNK = 128
MOE_GROUPS = 4
EXPERTS_PER_GROUP = 4
N_EXPERTS = MOE_GROUPS * EXPERTS_PER_GROUP
TOP_K = 2
D_EXPERT = 256
EPS = 1e-6
N_DELTA = (DEPTH + 1) // 2
N_GMLP = DEPTH // 2
DN_IN_COLS = 3 * TM_WIDTH + TM_WIDTH + 2 * TM_HEADS + MEM_WIDTH
GM_IN_COLS = 2 * TM_WIDTH + MEM_WIDTH

kernel_name = "hybrid_deltanet_gmlp_memxattn_hmoe"


def rmsnorm(x, g):
    xf = x.astype(jnp.float32)
    y = xf * lax.rsqrt(jnp.mean(xf * xf, axis=-1, keepdims=True) + EPS)
    return (y * g.astype(jnp.float32)).astype(x.dtype)


def layernorm(x, g, b):
    xf = x.astype(jnp.float32)
    mu = jnp.mean(xf, axis=-1, keepdims=True)
    var = jnp.mean(jnp.square(xf - mu), axis=-1, keepdims=True)
    y = (xf - mu) * lax.rsqrt(var + EPS)
    return (y * g.astype(jnp.float32) + b.astype(jnp.float32)).astype(x.dtype)


def l2norm(x):
    return x * lax.rsqrt(jnp.sum(x * x, axis=-1, keepdims=True) + EPS)


def causal_depthwise_conv(x, w):
    k_w, c = w.shape
    xp = jnp.pad(x, ((0, 0), (k_w - 1, 0), (0, 0)))
    return lax.conv_general_dilated(xp, w[:, None, :].astype(x.dtype), window_strides=(1,), padding='VALID',
                                    dimension_numbers=('NWC', 'WIO', 'NWC'), feature_group_count=c)


def chunk_gated_delta_rule(q, k, v, g, beta):
    b_, s_, h_, d_ = q.shape
    c_ = DELTA_CHUNK
    n_ = s_ // c_
    f32 = jnp.float32
    q = l2norm(q.astype(f32)) * (d_ ** -0.5)
    k = l2norm(k.astype(f32))
    v = v.astype(f32)

    def to_chunks(t):
        return t.reshape(b_, n_, c_, h_, -1).transpose(0, 3, 1, 2, 4)

    q, k, v = to_chunks(q), to_chunks(k), to_chunks(v)
    g = g.astype(f32).reshape(b_, n_, c_, h_).transpose(0, 3, 1, 2)
    beta = beta.astype(f32).reshape(b_, n_, c_, h_).transpose(0, 3, 1, 2)
    g = jnp.cumsum(g, axis=-1)
    tril = jnp.tril(jnp.ones((c_, c_), dtype=bool))
    strict = jnp.tril(jnp.ones((c_, c_), dtype=bool), -1)
    decay = jnp.exp(jnp.where(tril, g[..., :, None] - g[..., None, :], -jnp.inf))
    kb = k * beta[..., None]
    vb = v * beta[..., None]
    a_mat = jnp.where(strict, jnp.einsum('bhncd,bhnsd->bhncs', kb, k) * decay, 0.0)
    eye = jnp.broadcast_to(jnp.eye(c_, dtype=f32), a_mat.shape)
    t_inv = lax.linalg.triangular_solve(a_mat + eye, eye, left_side=True, lower=True)
    u = jnp.einsum('bhncs,bhnsd->bhncd', t_inv, vb)
    w = jnp.einsum('bhncs,bhnsd->bhncd', t_inv, kb * jnp.exp(g)[..., None])
    attn_intra = jnp.where(tril, jnp.einsum('bhncd,bhnsd->bhncs', q, k) * decay, 0.0)
    q_g = q * jnp.exp(g)[..., None]
    g_last = g[..., -1]
    k_g = k * jnp.exp(g_last[..., None] - g)[..., None]

    xs = tuple(jnp.moveaxis(t, 2, 0) for t in (q_g, k_g, u, w, attn_intra, g_last))

    def step(state, inp):
        qg, kg, uc, wc, ac, gl = inp
        v_new = uc - jnp.einsum('bhcd,bhde->bhce', wc, state)
        out = jnp.einsum('bhcd,bhde->bhce', qg, state) + jnp.einsum('bhcs,bhse->bhce', ac, v_new)
        state = state * jnp.exp(gl)[..., None, None] + jnp.einsum('bhcd,bhce->bhde', kg, v_new)
        return state, out

    state0 = jnp.zeros((b_, h_, d_, v.shape[-1]), f32)
    _, o = lax.scan(step, state0, xs)
    return o.transpose(1, 0, 3, 2, 4).reshape(b_, s_, h_, -1)


def gated_deltanet_branch(h, w_in, conv_w, a_log, dt_bias, o_norm_g):
    b_, s_, _ = h.shape
    proj = h @ w_in
    o1 = 3 * TM_WIDTH
    o2 = o1 + TM_WIDTH
    o3 = o2 + TM_HEADS
    o4 = o3 + TM_HEADS
    qkv, z, a, bb, q_mem = proj[..., :o1], proj[..., o1:o2], proj[..., o2:o3], proj[..., o3:o4], proj[..., o4:]
    qkv = jax.nn.silu(causal_depthwise_conv(qkv, conv_w))
    q = qkv[..., :TM_WIDTH].reshape(b_, s_, TM_HEADS, TM_HEAD_DIM)
    k = qkv[..., TM_WIDTH:2 * TM_WIDTH].reshape(b_, s_, TM_HEADS, TM_HEAD_DIM)
    v = qkv[..., 2 * TM_WIDTH:].reshape(b_, s_, TM_HEADS, TM_HEAD_DIM)
    beta = jax.nn.sigmoid(bb.astype(jnp.float32))
    g = -jnp.exp(a_log.astype(jnp.float32)) * jax.nn.softplus(a.astype(jnp.float32) + dt_bias.astype(jnp.float32))
    o = chunk_gated_delta_rule(q, k, v, g, beta)
    o = rmsnorm(o, o_norm_g) * jax.nn.silu(z.reshape(b_, s_, TM_HEADS, TM_HEAD_DIM).astype(jnp.float32))
    return o.reshape(b_, s_, TM_WIDTH).astype(h.dtype), q_mem


def gmlp_spatial_branch(h, w_in, ln_g, ln_b, w_spatial, b_spatial):
    b_, s_, _ = h.shape
    n_ = s_ // GMLP_CHUNK
    proj = h @ w_in
    uv = jax.nn.gelu(proj[..., :2 * TM_WIDTH], approximate=False)
    q_mem = proj[..., 2 * TM_WIDTH:]
    u, v = uv[..., :TM_WIDTH], uv[..., TM_WIDTH:]
    v = layernorm(v, ln_g, ln_b).reshape(b_, n_, GMLP_CHUNK, TM_HEADS, TM_HEAD_DIM)
    tril = jnp.tril(jnp.ones((GMLP_CHUNK, GMLP_CHUNK), dtype=bool))
    w_causal = jnp.where(tril[None], w_spatial, 0.0).astype(v.dtype)
    s = jnp.einsum('gts,bnsgc->bntgc', w_causal, v) + b_spatial.T[:, :, None].astype(v.dtype)
    return u * s.reshape(b_, s_, TM_WIDTH), q_mem


def memory_cross_attention(q_cols, mem_n, w_mem_kv):
    b_, s_, _ = q_cols.shape
    kv = mem_n @ w_mem_kv
    k = kv[..., :MEM_WIDTH].reshape(b_, -1, MEM_HEADS, MEM_HEAD_DIM)
    v = kv[..., MEM_WIDTH:].reshape(b_, -1, MEM_HEADS, MEM_HEAD_DIM)
    q = q_cols.reshape(b_, s_, MEM_HEADS, MEM_HEAD_DIM)
    scores = jnp.einsum('bshd,bmhd->bhsm', q, k).astype(jnp.float32) * (MEM_HEAD_DIM ** -0.5)
    p = jax.nn.softmax(scores, axis=-1).astype(v.dtype)
    return jnp.einsum('bhsm,bmhd->bshd', p, v).reshape(b_, s_, MEM_WIDTH)


def hierarchical_moe(h, w_group, b_group, w_router, b_router, w_gate, w_up, w_down):
    b_, s_, d_ = h.shape
    t = h.reshape(-1, d_)
    n_tok = t.shape[0]
    g_logits = (t @ w_group + b_group).astype(jnp.float32)
    p_group = jax.nn.softmax(g_logits, axis=-1)
    g_idx = jnp.argmax(g_logits, axis=-1)
    e_logits = (t @ w_router + b_router).astype(jnp.float32).reshape(n_tok, MOE_GROUPS, EXPERTS_PER_GROUP)
    e_sel = jnp.take_along_axis(e_logits, g_idx[:, None, None], axis=1)[:, 0]
    top_l, top_i = lax.top_k(e_sel, TOP_K)
    weights = jax.nn.softmax(top_l, axis=-1) * jnp.take_along_axis(p_group, g_idx[:, None], axis=1)
    e_id = g_idx[:, None] * EXPERTS_PER_GROUP + top_i
    gate = jnp.sum(jax.nn.one_hot(e_id, N_EXPERTS, dtype=jnp.float32) * weights[..., None], axis=1).astype(t.dtype)
    out = jnp.zeros_like(t)
    for e in range(N_EXPERTS):
        he = jax.nn.silu(t @ w_gate[e]) * (t @ w_up[e])
        out = out + gate[:, e:e + 1] * (he @ w_down[e])
    return out.reshape(b_, s_, d_)


def setup_inputs(seed: int = 0) -> dict:
    key = jax.random.key(seed)
    ks = jax.random.split(key, 26)
    f32 = jnp.float32

    def nrm(k, shape, scale):
        return jax.random.normal(k, shape, f32) * scale

    dt = jnp.exp(jax.random.uniform(ks[9], (N_DELTA, TM_HEADS), f32) * (np.log(0.1) - np.log(0.001)) + np.log(0.001))
    return {
        "x": nrm(ks[0], (BATCH, SEQ, D_MODEL), 1.0),
        "mem": nrm(ks[1], (BATCH, MEM_LEN, D_MODEL), 1.0),
        "mem_norm_g": 1.0 + nrm(ks[2], (D_MODEL,), 0.02),
        "mix_norm_g": 1.0 + nrm(ks[3], (DEPTH, D_MODEL), 0.02),
        "w_out": nrm(ks[4], (DEPTH, TM_WIDTH + MEM_WIDTH, D_MODEL), (TM_WIDTH + MEM_WIDTH) ** -0.5),
        "w_mem_kv": nrm(ks[5], (DEPTH, D_MODEL, 2 * MEM_WIDTH), D_MODEL ** -0.5),
        "dn_w_in": nrm(ks[6], (N_DELTA, D_MODEL, DN_IN_COLS), D_MODEL ** -0.5),
        "dn_conv_w": nrm(ks[7], (N_DELTA, CONV_K, 3 * TM_WIDTH), CONV_K ** -0.5),
        "dn_a_log": jnp.log(jax.random.uniform(ks[8], (N_DELTA, TM_HEADS), f32, 1.0, 16.0)),
        "dn_dt_bias": dt + jnp.log(-jnp.expm1(-dt)),
        "dn_o_norm_g": 1.0 + nrm(ks[10], (N_DELTA, TM_HEAD_DIM), 0.02),
        "gm_w_in": nrm(ks[11], (N_GMLP, D_MODEL, GM_IN_COLS), D_MODEL ** -0.5),
        "gm_ln_g": 1.0 + nrm(ks[12], (N_GMLP, TM_WIDTH), 0.02),
        "gm_ln_b": nrm(ks[13], (N_GMLP, TM_WIDTH), 0.02),
        "gm_w_spatial": nrm(ks[14], (N_GMLP, TM_HEADS, GMLP_CHUNK, GMLP_CHUNK), 0.5 * GMLP_CHUNK ** -0.5),
        "gm_b_spatial": 1.0 + nrm(ks[15], (N_GMLP, TM_HEADS, GMLP_CHUNK), 0.01),
        "ffn_norm_g": 1.0 + nrm(ks[16], (DEPTH, D_MODEL), 0.02),
        "moe_w_group": nrm(ks[17], (DEPTH, D_MODEL, MOE_GROUPS), D_MODEL ** -0.5),
        "moe_b_group": nrm(ks[18], (DEPTH, MOE_GROUPS), 0.01),
        "moe_w_router": nrm(ks[19], (DEPTH, D_MODEL, N_EXPERTS), D_MODEL ** -0.5),
        "moe_b_router": nrm(ks[20], (DEPTH, N_EXPERTS), 0.01),
        "moe_w_gate": nrm(ks[21], (DEPTH, N_EXPERTS, D_MODEL, D_EXPERT), D_MODEL ** -0.5),
        "moe_w_up": nrm(ks[22], (DEPTH, N_EXPERTS, D_MODEL, D_EXPERT), D_MODEL ** -0.5),
        "moe_w_down": nrm(ks[23], (DEPTH, N_EXPERTS, D_EXPERT, D_MODEL), D_EXPERT ** -0.5),
        "final_norm_g": 1.0 + nrm(ks[24], (D_MODEL,), 0.02),
    }


def reference(x, mem, mem_norm_g, mix_norm_g, w_out, w_mem_kv, dn_w_in, dn_conv_w, dn_a_log, dn_dt_bias,
              dn_o_norm_g, gm_w_in, gm_ln_g, gm_ln_b, gm_w_spatial, gm_b_spatial, ffn_norm_g, moe_w_group,
              moe_b_group, moe_w_router, moe_b_router, moe_w_gate, moe_w_up, moe_w_down, final_norm_g):
    mem_n = rmsnorm(mem, mem_norm_g)
    for i in range(DEPTH):
        j = i // N_MIXERS
        h = rmsnorm(x, mix_norm_g[i])
        if i % N_MIXERS == 0:
            mix, q_mem = gated_deltanet_branch(h, dn_w_in[j], dn_conv_w[j], dn_a_log[j], dn_dt_bias[j], dn_o_norm_g[j])
        else:
            mix, q_mem = gmlp_spatial_branch(h, gm_w_in[j], gm_ln_g[j], gm_ln_b[j], gm_w_spatial[j], gm_b_spatial[j])
        cross = memory_cross_attention(q_mem, mem_n, w_mem_kv[i])
        x = x + jnp.concatenate([mix, cross], axis=-1) @ w_out[i]
        h = rmsnorm(x, ffn_norm_g[i])
        x = x + hierarchical_moe(h, moe_w_group[i], moe_b_group[i], moe_w_router[i], moe_b_router[i],
                                 moe_w_gate[i], moe_w_up[i], moe_w_down[i])
    return rmsnorm(x, final_norm_g)
```

```python
import functools

import jax
import jax.numpy as jnp
from jax import lax
from jax.experimental import pallas as pl
from jax.experimental.pallas import tpu as pltpu

F32 = jnp.float32
BF16 = jnp.bfloat16
EPS = 1e-6

MEM_HEADS = 4
MEM_HEAD_DIM = 64
MEM_WIDTH = MEM_HEADS * MEM_HEAD_DIM
HEAD_DIM = 128
CONV_K = 4
CHUNK = 128
MOE_GROUPS = 4
EXPERTS_PER_GROUP = 4
N_EXPERTS = MOE_GROUPS * EXPERTS_PER_GROUP
LANES = 128
VMEM_LIMIT = 56 * 1024 * 1024

NT_DIMS = (((1,), (1,)), ((), ()))


def _dot(a, b):
    return jnp.dot(a, b, preferred_element_type=F32)


def _dot_nt(a, b):
    return lax.dot_general(a, b, NT_DIMS, preferred_element_type=F32)


def _rms(x, g):
    return x * lax.rsqrt(jnp.mean(x * x, axis=-1, keepdims=True) + EPS) * g


def _sigmoid(x):
    return 1.0 / (1.0 + jnp.exp(-x))


def _softplus(x):
    return jnp.maximum(x, 0.0) + jnp.log1p(jnp.exp(-jnp.abs(x)))


def _split3(x):
    hi = x.astype(BF16)
    r = x - hi.astype(F32)
    mid = r.astype(BF16)
    lo = (r - mid.astype(F32)).astype(BF16)
    return hi, mid, lo


def _cparams(sem):
    return pltpu.CompilerParams(dimension_semantics=sem, vmem_limit_bytes=VMEM_LIMIT)


def _memkv_kernel(mem_ref, g_ref, w_ref, kv_ref):
    mn = _rms(mem_ref[0], g_ref[...]).astype(BF16)
    kv_ref[0] = _dot(mn, w_ref[...]).astype(BF16)


def _memkv(mem, g, w_all):
    b, m, d = mem.shape
    n = w_all.shape[1]
    return pl.pallas_call(
        _memkv_kernel,
        grid=(b,),
        in_specs=[pl.BlockSpec((1, m, d), lambda i: (i, 0, 0)),
                  pl.BlockSpec((1, d), lambda i: (0, 0)),
                  pl.BlockSpec((d, n), lambda i: (0, 0))],
        out_specs=pl.BlockSpec((1, m, n), lambda i: (i, 0, 0)),
        out_shape=jax.ShapeDtypeStruct((b, m, n), BF16),
        compiler_params=_cparams(("parallel",)),
        name="memkv",
    )(mem, g, w_all)


def _cross_attn(qm, kmem, vmem):
    lane = lax.broadcasted_iota(jnp.int32, (1, MEM_WIDTH), 1)
    out = jnp.zeros(qm.shape, F32)
    for h in range(MEM_HEADS):
        msk = (lane >= h * MEM_HEAD_DIM) & (lane < (h + 1) * MEM_HEAD_DIM)
        qh = jnp.where(msk, qm, 0.0).astype(BF16)
        s = _dot_nt(qh, kmem) * (MEM_HEAD_DIM ** -0.5)
        p = jnp.exp(s - jnp.max(s, axis=-1, keepdims=True))
        inv = 1.0 / jnp.sum(p, axis=-1, keepdims=True)
        vh = jnp.where(msk, vmem, jnp.zeros_like(vmem))
        out = out + _dot(p.astype(BF16), vh) * inv
    return out


def _router_gate(logits):
    lane = lax.broadcasted_iota(jnp.int32, logits.shape, 1)
    neg = jnp.float32(-jnp.inf)
    big = jnp.int32(1 << 20)
    is_g = (lane >= N_EXPERTS) & (lane < N_EXPERTS + MOE_GROUPS)
    gl = jnp.where(is_g, logits, neg)
    gmax = jnp.max(gl, axis=-1, keepdims=True)
    g_idx = jnp.min(jnp.where(gl == gmax, lane, big), axis=-1, keepdims=True) - N_EXPERTS
    p_grp = 1.0 / jnp.sum(jnp.exp(gl - gmax), axis=-1, keepdims=True)
    lo = g_idx * EXPERTS_PER_GROUP
    in_grp = (lane >= lo) & (lane < lo + EXPERTS_PER_GROUP)
    el = jnp.where(in_grp, logits, neg)
    m1 = jnp.max(el, axis=-1, keepdims=True)
    i1 = jnp.min(jnp.where(el == m1, lane, big), axis=-1, keepdims=True)
    el2 = jnp.where(lane == i1, neg, el)
    m2 = jnp.max(el2, axis=-1, keepdims=True)
    i2 = jnp.min(jnp.where(el2 == m2, lane, big), axis=-1, keepdims=True)
    e2 = jnp.exp(m2 - m1)
    w1 = p_grp / (1.0 + e2)
    w2 = p_grp * e2 / (1.0 + e2)
    return jnp.where(lane == i1, w1, jnp.where(lane == i2, w2, 0.0))


def _dn_front_kernel(x_ref, ng_ref, wqkv_ref, wz_ref, wqm_ref, wabt_ref, convw_ref, alog_ref, dtb_ref, kv_ref,
                     q_ref, k_ref, v_ref, zg_ref, gb_ref, cross_ref, cbuf_ref, *, tm, n_heads):
    j = pl.program_id(1)
    tw = n_heads * HEAD_DIM
    h = _rms(x_ref[0], ng_ref[...]).astype(BF16)

    pre = _dot(h, wqkv_ref[...])

    @pl.when(j == 0)
    def _():
        cbuf_ref[0:8, :] = jnp.zeros((8, 3 * tw), F32)

    cbuf_ref[8:8 + tm, :] = pre
    w = convw_ref[...]
    acc = w[CONV_K - 1:CONV_K, :] * pre
    for kk in range(CONV_K - 1):
        acc = acc + w[kk:kk + 1, :] * cbuf_ref[pl.ds(8 - (CONV_K - 1) + kk, tm), :]
    cbuf_ref[0:8, :] = cbuf_ref[tm:tm + 8, :]
    qkv = acc * _sigmoid(acc)

    for hd in range(n_heads):
        sl = slice(hd * HEAD_DIM, (hd + 1) * HEAD_DIM)
        qh = qkv[:, hd * HEAD_DIM:(hd + 1) * HEAD_DIM]
        kh = qkv[:, tw + hd * HEAD_DIM:tw + (hd + 1) * HEAD_DIM]
        qn = qh * (lax.rsqrt(jnp.sum(qh * qh, axis=-1, keepdims=True) + EPS) * (HEAD_DIM ** -0.5))
        kn = kh * lax.rsqrt(jnp.sum(kh * kh, axis=-1, keepdims=True) + EPS)
        q_ref[0, :, sl] = qn.astype(BF16)
        k_ref[0, :, sl] = kn.astype(BF16)
    v_ref[0] = qkv[:, 2 * tw:].astype(BF16)

    z = _dot(h, wz_ref[...])
    zg_ref[0] = (z * _sigmoid(z)).astype(BF16)

    abt = _dot_nt(wabt_ref[...], h)
    g_t = -jnp.exp(alog_ref[...]) * _softplus(abt + dtb_ref[...])
    b_t = _sigmoid(abt)
    for hd in range(n_heads):
        gb_ref[0, hd, 0:1, :] = g_t[hd:hd + 1, :]
        gb_ref[0, hd, 1:2, :] = b_t[n_heads + hd:n_heads + hd + 1, :]

    qm = _dot(h, wqm_ref[...])
    kv = kv_ref[0]
    cross_ref[0] = _cross_attn(qm, kv[:, :MEM_WIDTH], kv[:, MEM_WIDTH:]).astype(BF16)


def _dn_front(x, ng, wqkv, wz, wqm, wabt, convw, alog, dtb, kv, *, tm):
    b, s, d = x.shape
    tw = wz.shape[1]
    n_heads = tw // HEAD_DIM
    mlen = kv.shape[1]
    const = lambda shape: pl.BlockSpec(shape, lambda i, j: (0,) * len(shape))
    tok = lambda w: pl.BlockSpec((1, tm, w), lambda i, j: (i, j, 0))
    return pl.pallas_call(
        functools.partial(_dn_front_kernel, tm=tm, n_heads=n_heads),
        grid=(b, s // tm),
        in_specs=[tok(d), const((1, d)), const((d, 3 * tw)), const((d, tw)), const((d, MEM_WIDTH)),
                  const((16, d)), const((CONV_K, 3 * tw)), const((16, 1)), const((16, 1)),
                  pl.BlockSpec((1, mlen, 2 * MEM_WIDTH), lambda i, j: (i, 0, 0))],
        out_specs=[tok(tw), tok(tw), tok(tw), tok(tw),
                   pl.BlockSpec((1, n_heads, 2, tm), lambda i, j: (i, 0, 0, j)),
                   tok(MEM_WIDTH)],
        out_shape=[jax.ShapeDtypeStruct((b, s, tw), BF16)] * 4
                  + [jax.ShapeDtypeStruct((b, n_heads, 2, s), F32),
                     jax.ShapeDtypeStruct((b, s, MEM_WIDTH), BF16)],
        scratch_shapes=[pltpu.VMEM((tm + 8, 3 * tw), F32)],
        compiler_params=_cparams(("parallel", "arbitrary")),
        name="dn_front",
    )(x, ng, wqkv, wz, wqm, wabt, convw, alog, dtb, kv)


def _tri_inverse(a, ii, jj):
    c = a.shape[0]
    eye = (ii == jj).astype(F32)
    bi, bj = ii >> 4, jj >> 4
    n0 = jnp.where(bi == bj, -a, 0.0)
    x = eye + n0
    n0b = n0.astype(BF16)
    n1 = _dot(n0b, n0b)
    n1b = n1.astype(BF16)
    r = _dot(n1b, jnp.concatenate([n1b, x.astype(BF16)], axis=1))
    n2, x = r[:, :c], x + r[:, c:]
    n2b = n2.astype(BF16)
    r = _dot(n2b, jnp.concatenate([n2b, x.astype(BF16)], axis=1))
    n3, x = r[:, :c], x + r[:, c:]
    x = x + _dot(n3.astype(BF16), x.astype(BF16))
    shift = 4
    while (1 << shift) < c:
        inner = (ii >> shift) == (jj >> shift)
        outer = (ii >> (shift + 1)) == (jj >> (shift + 1))
        l = jnp.where(outer & jnp.logical_not(inner), a, 0.0).astype(BF16)
        xb = x.astype(BF16)
        x = x - _dot(xb, _dot(l, xb).astype(BF16))
        shift += 1
    return x


def _dn_intra_kernel(q_ref, k_ref, v_ref, gb_ref, u_ref, wp_ref, ae_ref, kts_ref, egl_ref, *, n_chunks):
    c = CHUNK
    ii = lax.broadcasted_iota(jnp.int32, (c, c), 0)
    jj = lax.broadcasted_iota(jnp.int32, (c, c), 1)
    tril = ii >= jj
    strict = ii > jj
    eye = ii == jj
    upper_incl = (ii <= jj).astype(BF16)
    for n in range(n_chunks):
        rows = slice(n * c, (n + 1) * c)
        q = q_ref[0, rows, :]
        k = k_ref[0, rows, :]
        v = v_ref[0, rows, :]
        g_row = gb_ref[0, 0, 0:1, rows]
        b_row = gb_ref[0, 0, 1:2, rows]
        g8 = jnp.broadcast_to(g_row, (8, c))
        hi, mid, lo = _split3(g8)
        gc8 = _dot(hi, upper_incl) + _dot(mid, upper_incl) + _dot(lo, upper_incl)
        gc_row = gc8[0:1, :]
        gl = jnp.sum(g_row, axis=-1, keepdims=True)
        g_r = jnp.broadcast_to(gc_row, (c, c))
        g_c = g_r.T
        b_r = jnp.broadcast_to(b_row, (c, c))
        b_c = b_r.T
        decay = jnp.where(tril, jnp.exp(g_c - g_r), 0.0)
        kq = _dot_nt(jnp.concatenate([k, q], axis=0), k)
        a = jnp.where(strict, kq[:c] * decay, 0.0) * b_c
        attn = jnp.where(tril, kq[c:] * decay, 0.0)
        t = _tri_inverse(a, ii, jj)
        tb = t * b_r
        e_r = jnp.exp(g_r)
        u_ref[0, 0, rows, :] = _dot(tb.astype(BF16), v).astype(BF16)
        wp_ref[0, 0, rows, :] = (tb * e_r).astype(BF16)
        ae_ref[0, 0, rows, 0:c] = attn.astype(BF16)
        ae_ref[0, 0, rows, c:2 * c] = jnp.where(eye, e_r, 0.0).astype(BF16)
        kscale = jnp.exp(gl - gc_row)
        kts_ref[0, 0, :, rows] = (k.astype(F32).T * kscale).astype(BF16)
        egl_ref[0, 0, n] = jnp.broadcast_to(jnp.exp(gl), (1, LANES))


def _dn_intra(q, k, v, gb, *, tb):
    b, s, tw = q.shape
    n_heads = tw // HEAD_DIM
    n_chunks = tb // CHUNK
    qkv_spec = pl.BlockSpec((1, tb, HEAD_DIM), lambda i, h, j: (i, j, h))
    hs = lambda w: pl.BlockSpec((1, 1, tb, w), lambda i, h, j: (i, h, j, 0))
    return pl.pallas_call(
        functools.partial(_dn_intra_kernel, n_chunks=n_chunks),
        grid=(b, n_heads, s // tb),
        in_specs=[qkv_spec, qkv_spec, qkv_spec,
                  pl.BlockSpec((1, 1, 2, tb), lambda i, h, j: (i, h, 0, j))],
        out_specs=[hs(HEAD_DIM), hs(CHUNK), hs(2 * CHUNK),
                   pl.BlockSpec((1, 1, HEAD_DIM, tb), lambda i, h, j: (i, h, 0, j)),
                   pl.BlockSpec((1, 1, n_chunks, 1, LANES), lambda i, h, j: (i, h, j, 0, 0))],
        out_shape=[jax.ShapeDtypeStruct((b, n_heads, s, HEAD_DIM), BF16),
                   jax.ShapeDtypeStruct((b, n_heads, s, CHUNK), BF16),
                   jax.ShapeDtypeStruct((b, n_heads, s, 2 * CHUNK), BF16),
                   jax.ShapeDtypeStruct((b, n_heads, HEAD_DIM, s), BF16),
                   jax.ShapeDtypeStruct((b, n_heads, s // CHUNK, 1, LANES), F32)],
        compiler_params=_cparams(("parallel", "parallel", "parallel")),
        name="dn_intra",
    )(q, k, v, gb)


def _dn_scan_kernel(q_ref, k_ref, u_ref, wp_ref, ae_ref, kts_ref, egl_ref, o_ref, s_ref, *, n_chunks, n_heads):
    c = CHUNK

    @pl.when(pl.program_id(1) == 0)
    def _():
        s_ref[...] = jnp.zeros(s_ref.shape, F32)

    states = [s_ref[hd] for hd in range(n_heads)]
    for n in range(n_chunks):
        rows = slice(n * c, (n + 1) * c)
        for hd in range(n_heads):
            cols = slice(hd * HEAD_DIM, (hd + 1) * HEAD_DIM)
            st = states[hd]
            kq = jnp.concatenate([k_ref[0, rows, cols], q_ref[0, rows, cols]], axis=0)
            kqs = _dot(kq, st.astype(BF16))
            x = u_ref[0, hd, rows, :].astype(F32) - _dot(wp_ref[0, hd, rows, :], kqs[:c].astype(BF16))
            xb = x.astype(BF16)
            r = jnp.concatenate([xb, kqs[c:].astype(BF16)], axis=0)
            o_ref[0, rows, cols] = _dot(ae_ref[0, hd, rows, :], r).astype(BF16)
            states[hd] = st * egl_ref[0, hd, n] + _dot(kts_ref[0, hd, :, rows], xb)
    for hd in range(n_heads):
        s_ref[hd] = states[hd]


def _dn_scan(q, k, u, wp, ae, kts, egl, *, tb):
    b, s, tw = q.shape
    n_heads = tw // HEAD_DIM
    n_chunks = tb // CHUNK
    tok = pl.BlockSpec((1, tb, tw), lambda i, j: (i, j, 0))
    hs = lambda w: pl.BlockSpec((1, n_heads, tb, w), lambda i, j: (i, 0, j, 0))
    return pl.pallas_call(
        functools.partial(_dn_scan_kernel, n_chunks=n_chunks, n_heads=n_heads),
        grid=(b, s // tb),
        in_specs=[tok, tok, hs(HEAD_DIM), hs(CHUNK), hs(2 * CHUNK),
                  pl.BlockSpec((1, n_heads, HEAD_DIM, tb), lambda i, j: (i, 0, 0, j)),
                  pl.BlockSpec((1, n_heads, n_chunks, 1, LANES), lambda i, j: (i, 0, j, 0, 0))],
        out_specs=tok,
        out_shape=jax.ShapeDtypeStruct((b, s, tw), BF16),
        scratch_shapes=[pltpu.VMEM((n_heads, HEAD_DIM, HEAD_DIM), F32)],
        compiler_params=_cparams(("parallel", "arbitrary")),
        name="dn_scan",
    )(q, k, u, wp, ae, kts, egl)


def _mixer_tail(x, mix_b, cross_b, wo_mix_ref, wo_mem_ref, fg_ref, wr_ref, br_ref, x1_ref, h2_ref, gate_ref):
    y = _dot(mix_b, wo_mix_ref[...]) + _dot(cross_b, wo_mem_ref[...])
    x1 = x + y
    x1_ref[...] = x1
    h2 = _rms(x1, fg_ref[...]).astype(BF16)
    h2_ref[...] = h2
    logits = _dot(h2, wr_ref[...]) + br_ref[...]
    gate_ref[...] = _router_gate(logits)


def _dn_out_kernel(x_ref, o_ref, zg_ref, cross_ref, og_ref, wo_mix_ref, wo_mem_ref, fg_ref, wr_ref, br_ref,
                   x1_ref, h2_ref, gate_ref, *, n_heads):
    og = og_ref[...]
    parts = []
    for hd in range(n_heads):
        cols = slice(hd * HEAD_DIM, (hd + 1) * HEAD_DIM)
        oh = o_ref[:, cols].astype(F32)
        on = oh * lax.rsqrt(jnp.mean(oh * oh, axis=-1, keepdims=True) + EPS) * og
        parts.append((on * zg_ref[:, cols].astype(F32)).astype(BF16))
    mix = jnp.concatenate(parts, axis=1)
    _mixer_tail(x_ref[...], mix, cross_ref[...], wo_mix_ref, wo_mem_ref, fg_ref, wr_ref, br_ref,
                x1_ref, h2_ref, gate_ref)


def _dn_out(x2d, o2d, zg2d, cross2d, og, wo_mix, wo_mem, fg, wr, br, *, tm):
    t, d = x2d.shape
    tw = o2d.shape[1]
    n_heads = tw // HEAD_DIM
    const = lambda shape: pl.BlockSpec(shape, lambda i: (0,) * len(shape))
    tok = lambda w: pl.BlockSpec((tm, w), lambda i: (i, 0))
    return pl.pallas_call(
        functools.partial(_dn_out_kernel, n_heads=n_heads),
        grid=(t // tm,),
        in_specs=[tok(d), tok(tw), tok(tw), tok(MEM_WIDTH), const((1, HEAD_DIM)), const((tw, d)),
                  const((MEM_WIDTH, d)), const((1, d)), const((d, LANES)), const((1, LANES))],
        out_specs=[tok(d), tok(d), tok(LANES)],
        out_shape=[jax.ShapeDtypeStruct((t, d), F32), jax.ShapeDtypeStruct((t, d), BF16),
                   jax.ShapeDtypeStruct((t, LANES), F32)],
        compiler_params=_cparams(("parallel",)),
        name="dn_out",
    )(x2d, o2d, zg2d, cross2d, og, wo_mix, wo_mem, fg, wr, br)


def _moe_kernel(h_ref, gate_ref, x_ref, wgu_ref, wd_ref, fin_ref, out_ref, *, d_expert, final_norm):
    e = pl.program_id(1)

    @pl.when(e == 0)
    def _():
        out_ref[...] = x_ref[...]

    gu = _dot(h_ref[...], wgu_ref[0])
    gpart, upart = gu[:, :d_expert], gu[:, d_expert:]
    he = (gpart * _sigmoid(gpart) * upart).astype(BF16)
    y = _dot(he, wd_ref[0])
    gate = gate_ref[...]
    lane = lax.broadcasted_iota(jnp.int32, gate.shape, 1)
    ge = jnp.sum(jnp.where(lane == e, gate, 0.0), axis=-1, keepdims=True)
    out_ref[...] += ge * y

    if final_norm:
        @pl.when(e == pl.num_programs(1) - 1)
        def _():
            out_ref[...] = _rms(out_ref[...], fin_ref[...])


def _moe(h2, gate, x1, wgu, wd, fin_g, *, tm, final_norm):
    t, d = x1.shape
    n_e, _, de2 = wgu.shape
    d_expert = de2 // 2
    tok = lambda w: pl.BlockSpec((tm, w), lambda i, e: (i, 0))
    return pl.pallas_call(
        functools.partial(_moe_kernel, d_expert=d_expert, final_norm=final_norm),
        grid=(t // tm, n_e),
        in_specs=[tok(d), tok(LANES), tok(d),
                  pl.BlockSpec((1, d, de2), lambda i, e: (e, 0, 0)),
                  pl.BlockSpec((1, d_expert, d), lambda i, e: (e, 0, 0)),
                  pl.BlockSpec((1, d), lambda i, e: (0, 0))],
        out_specs=tok(d),
        out_shape=jax.ShapeDtypeStruct((t, d), F32),
        compiler_params=_cparams(("parallel", "arbitrary")),
        name="moe",
    )(h2, gate, x1, wgu, wd, fin_g)


def _gm_layer_kernel(x_ref, ng_ref, win_ref, wqm_ref, lng_ref, lnb_ref, wsp_ref, bsp_ref, kv_ref,
                     wo_mix_ref, wo_mem_ref, fg_ref, wr_ref, br_ref, x1_ref, h2_ref, gate_ref, *, tm, n_groups):
    c = CHUNK
    tw = n_groups * HEAD_DIM
    x = x_ref[...]
    h = _rms(x, ng_ref[...]).astype(BF16)
    proj = _dot(h, win_ref[...])
    uv = 0.5 * proj * (1.0 + lax.erf(proj * (2.0 ** -0.5)))
    u, v = uv[:, :tw], uv[:, tw:]
    mu = jnp.mean(v, axis=-1, keepdims=True)
    vc = v - mu
    var = jnp.mean(vc * vc, axis=-1, keepdims=True)
    vn = (vc * lax.rsqrt(var + EPS) * lng_ref[...] + lnb_ref[...]).astype(BF16)
    ii = lax.broadcasted_iota(jnp.int32, (c, c), 0)
    jj = lax.broadcasted_iota(jnp.int32, (c, c), 1)
    tril = ii >= jj
    row_parts = []
    for n in range(tm // c):
        col_parts = []
        for g in range(n_groups):
            wc = jnp.where(tril, wsp_ref[g], jnp.zeros((c, c), BF16))
            sg = _dot(wc, vn[n * c:(n + 1) * c, g * HEAD_DIM:(g + 1) * HEAD_DIM]) + bsp_ref[g]
            col_parts.append(sg)
        row_parts.append(jnp.concatenate(col_parts, axis=1))
    sgate = jnp.concatenate(row_parts, axis=0)
    mix = (u * sgate).astype(BF16)
    qm = _dot(h, wqm_ref[...])
    kv = kv_ref[0]
    cross = _cross_attn(qm, kv[:, :MEM_WIDTH], kv[:, MEM_WIDTH:]).astype(BF16)
    _mixer_tail(x, mix, cross, wo_mix_ref, wo_mem_ref, fg_ref, wr_ref, br_ref, x1_ref, h2_ref, gate_ref)


def _gm_layer(x2d, ng, win, wqm, lng, lnb, wsp, bsp, kv, kv_layer, wo_mix, wo_mem, fg, wr, br, *, tm, seq):
    t, d = x2d.shape
    tw = lng.shape[1]
    n_groups = tw // HEAD_DIM
    mlen = kv.shape[1]
    tiles_per_seq = seq // tm
    const = lambda shape: pl.BlockSpec(shape, lambda i: (0,) * len(shape))
    tok = lambda w: pl.BlockSpec((tm, w), lambda i: (i, 0))
    return pl.pallas_call(
        functools.partial(_gm_layer_kernel, tm=tm, n_groups=n_groups),
        grid=(t // tm,),
        in_specs=[tok(d), const((1, d)), const((d, 2 * tw)), const((d, MEM_WIDTH)), const((1, tw)), const((1, tw)),
                  const((n_groups, CHUNK, CHUNK)), const((n_groups, CHUNK, HEAD_DIM)),
                  pl.BlockSpec((1, mlen, 2 * MEM_WIDTH), lambda i: (i // tiles_per_seq, 0, kv_layer)),
                  const((tw, d)), const((MEM_WIDTH, d)), const((1, d)), const((d, LANES)), const((1, LANES))],
        out_specs=[tok(d), tok(d), tok(LANES)],
        out_shape=[jax.ShapeDtypeStruct((t, d), F32), jax.ShapeDtypeStruct((t, d), BF16),
                   jax.ShapeDtypeStruct((t, LANES), F32)],
        compiler_params=_cparams(("parallel",)),
        name="gm_layer",
    )(x2d, ng, win, wqm, lng, lnb, wsp, bsp, kv, wo_mix, wo_mem, fg, wr, br)


def _router_params(w_group, b_group, w_router, b_router):
    d = w_group.shape[0]
    pad = LANES - N_EXPERTS - MOE_GROUPS
    wr = jnp.concatenate([w_router, w_group, jnp.zeros((d, pad), F32)], axis=1).astype(BF16)
    br = jnp.concatenate([b_router, b_group, jnp.zeros((pad,), F32)])[None, :]
    return wr, br


def _col16(vec):
    return jnp.concatenate([vec, jnp.zeros((16 - vec.shape[0],), F32)])[:, None]


def kernel(x, mem, mem_norm_g, mix_norm_g, w_out, w_mem_kv, dn_w_in, dn_conv_w, dn_a_log, dn_dt_bias, dn_o_norm_g,
           gm_w_in, gm_ln_g, gm_ln_b, gm_w_spatial, gm_b_spatial, ffn_norm_g, moe_w_group, moe_b_group,
           moe_w_router, moe_b_router, moe_w_gate, moe_w_up, moe_w_down, final_norm_g):
    b, s, d = x.shape
    tw = d - MEM_WIDTH
    n_heads = tw // HEAD_DIM
    t = b * s

    kv = _memkv(mem, mem_norm_g[None, :], jnp.concatenate([w_mem_kv[0], w_mem_kv[1]], axis=1).astype(BF16))

    w_in = dn_w_in[0]
    o1, o2, o3, o4 = 3 * tw, 4 * tw, 4 * tw + n_heads, 4 * tw + 2 * n_heads
    wabt = jnp.concatenate([w_in[:, o2:o4].T, jnp.zeros((16 - 2 * n_heads, d), F32)], axis=0).astype(BF16)
    q, k, v, zg, gb, cross = _dn_front(
        x, mix_norm_g[0][None, :], w_in[:, :o1].astype(BF16), w_in[:, o1:o2].astype(BF16),
        w_in[:, o4:].astype(BF16), wabt, dn_conv_w[0], _col16(dn_a_log[0]), _col16(dn_dt_bias[0]), kv, tm=512)
    u, wp, ae, kts, egl = _dn_intra(q, k, v, gb, tb=512)
    o = _dn_scan(q, k, u, wp, ae, kts, egl, tb=512)
    wr0, br0 = _router_params(moe_w_group[0], moe_b_group[0], moe_w_router[0], moe_b_router[0])
    wo0 = w_out[0].astype(BF16)
    x1, h2, gate = _dn_out(x.reshape(t, d), o.reshape(t, tw), zg.reshape(t, tw), cross.reshape(t, MEM_WIDTH),
                           dn_o_norm_g[0][None, :], wo0[:tw], wo0[tw:], ffn_norm_g[0][None, :], wr0, br0, tm=512)
    wgu0 = jnp.concatenate([moe_w_gate[0], moe_w_up[0]], axis=2).astype(BF16)
    x2 = _moe(h2, gate, x1, wgu0, moe_w_down[0].astype(BF16), final_norm_g[None, :], tm=1024, final_norm=False)

    win1 = gm_w_in[0]
    wr1, br1 = _router_params(moe_w_group[1], moe_b_group[1], moe_w_router[1], moe_b_router[1])
    wo1 = w_out[1].astype(BF16)
    bsp = jnp.broadcast_to(gm_b_spatial[0][:, :, None], (n_heads, CHUNK, HEAD_DIM))
    x3, h3, gate1 = _gm_layer(
        x2, mix_norm_g[1][None, :], win1[:, :2 * tw].astype(BF16), win1[:, 2 * tw:].astype(BF16),
        gm_ln_g[0][None, :], gm_ln_b[0][None, :], gm_w_spatial[0].astype(BF16), bsp, kv, 1,
        wo1[:tw], wo1[tw:], ffn_norm_g[1][None, :], wr1, br1, tm=512, seq=s)
    wgu1 = jnp.concatenate([moe_w_gate[1], moe_w_up[1]], axis=2).astype(BF16)
    out = _moe(h3, gate1, x3, wgu1, moe_w_down[1].astype(BF16), final_norm_g[None, :], tm=1024, final_norm=True)
    return out.reshape(b, s, d)
```

```python
import functools

import jax
import jax.numpy as jnp
from jax import lax
from jax.experimental import pallas as pl
from jax.experimental.pallas import tpu as pltpu

F32 = jnp.float32
BF16 = jnp.bfloat16
EPS = 1e-6

MEM_HEADS = 4
MEM_HEAD_DIM = 64
MEM_WIDTH = MEM_HEADS * MEM_HEAD_DIM
HEAD_DIM = 128
CONV_K = 4
CHUNK = 128
MOE_GROUPS = 4
EXPERTS_PER_GROUP = 4
N_EXPERTS = MOE_GROUPS * EXPERTS_PER_GROUP
PAIRS_PER_GROUP = EXPERTS_PER_GROUP * (EXPERTS_PER_GROUP - 1) // 2
N_CLASSES = MOE_GROUPS * PAIRS_PER_GROUP
MOE_TILE = 256
LANES = 128
SUBLANES = 8
RANK_BITS = 16
VMEM_LIMIT = 56 * 1024 * 1024

NT_DIMS = (((1,), (1,)), ((), ()))


def _dot(a, b):
    return jnp.dot(a, b, preferred_element_type=F32)


def _dot_nt(a, b):
    return lax.dot_general(a, b, NT_DIMS, preferred_element_type=F32)


def _rms(x, g):
    return x * lax.rsqrt(jnp.mean(x * x, axis=-1, keepdims=True) + EPS) * g


def _sigmoid(x):
    return 1.0 / (1.0 + jnp.exp(-x))


def _softplus(x):
    return jnp.maximum(x, 0.0) + jnp.log1p(jnp.exp(-jnp.abs(x)))


def _split3(x):
    hi = x.astype(BF16)
    r = x - hi.astype(F32)
    mid = r.astype(BF16)
    lo = (r - mid.astype(F32)).astype(BF16)
    return hi, mid, lo


def _cparams(sem):
    return pltpu.CompilerParams(dimension_semantics=sem, vmem_limit_bytes=VMEM_LIMIT)


def _memkv_kernel(mem_ref, g_ref, w_ref, kv_ref):
    mn = _rms(mem_ref[0], g_ref[...]).astype(BF16)
    kv_ref[0] = _dot(mn, w_ref[...]).astype(BF16)


def _memkv(mem, g, w_all):
    b, m, d = mem.shape
    n = w_all.shape[1]
    return pl.pallas_call(
        _memkv_kernel,
        grid=(b,),
        in_specs=[pl.BlockSpec((1, m, d), lambda i: (i, 0, 0)),
                  pl.BlockSpec((1, d), lambda i: (0, 0)),
                  pl.BlockSpec((d, n), lambda i: (0, 0))],
        out_specs=pl.BlockSpec((1, m, n), lambda i: (i, 0, 0)),
        out_shape=jax.ShapeDtypeStruct((b, m, n), BF16),
        compiler_params=_cparams(("parallel",)),
        name="memkv",
    )(mem, g, w_all)


def _cross_attn(qm, kmem, vmem):
    lane = lax.broadcasted_iota(jnp.int32, (1, MEM_WIDTH), 1)
    out = jnp.zeros(qm.shape, F32)
    for h in range(MEM_HEADS):
        msk = (lane >= h * MEM_HEAD_DIM) & (lane < (h + 1) * MEM_HEAD_DIM)
        qh = jnp.where(msk, qm, 0.0).astype(BF16)
        s = _dot_nt(qh, kmem) * (MEM_HEAD_DIM ** -0.5)
        p = jnp.exp(s - jnp.max(s, axis=-1, keepdims=True))
        inv = 1.0 / jnp.sum(p, axis=-1, keepdims=True)
        vh = jnp.where(msk, vmem, jnp.zeros_like(vmem))
        out = out + _dot(p.astype(BF16), vh) * inv
    return out


def _router(logits):
    lane = lax.broadcasted_iota(jnp.int32, logits.shape, 1)
    neg = jnp.float32(-jnp.inf)
    big = jnp.int32(1 << 20)
    is_g = (lane >= N_EXPERTS) & (lane < N_EXPERTS + MOE_GROUPS)
    gl = jnp.where(is_g, logits, neg)
    gmax = jnp.max(gl, axis=-1, keepdims=True)
    g_idx = jnp.min(jnp.where(gl == gmax, lane, big), axis=-1, keepdims=True) - N_EXPERTS
    p_grp = 1.0 / jnp.sum(jnp.exp(gl - gmax), axis=-1, keepdims=True)
    lo = g_idx * EXPERTS_PER_GROUP
    in_grp = (lane >= lo) & (lane < lo + EXPERTS_PER_GROUP)
    el = jnp.where(in_grp, logits, neg)
    m1 = jnp.max(el, axis=-1, keepdims=True)
    i1 = jnp.min(jnp.where(el == m1, lane, big), axis=-1, keepdims=True)
    el2 = jnp.where(lane == i1, neg, el)
    m2 = jnp.max(el2, axis=-1, keepdims=True)
    i2 = jnp.min(jnp.where(el2 == m2, lane, big), axis=-1, keepdims=True)
    e2 = jnp.exp(m2 - m1)
    w1 = p_grp / (1.0 + e2)
    w2 = p_grp * e2 / (1.0 + e2)
    first_lower = i1 < i2
    e_lo = jnp.where(first_lower, i1, i2) - lo
    e_hi = jnp.where(first_lower, i2, i1) - lo
    pair_off = jnp.where(e_lo == 0, 0, jnp.where(e_lo == 1, 3, 5))
    cls = g_idx * PAIRS_PER_GROUP + pair_off + e_hi - e_lo - 1
    return cls, jnp.where(first_lower, w1, w2), jnp.where(first_lower, w2, w1)


def _class_experts():
    lo_ids, hi_ids = [], []
    for g in range(MOE_GROUPS):
        for a in range(EXPERTS_PER_GROUP):
            for b in range(a + 1, EXPERTS_PER_GROUP):
                lo_ids.append(g * EXPERTS_PER_GROUP + a)
                hi_ids.append(g * EXPERTS_PER_GROUP + b)
    return lo_ids, hi_ids


def _dn_front_kernel(x_ref, ng_ref, wqkv_ref, wz_ref, wqm_ref, wabt_ref, convw_ref, alog_ref, dtb_ref, kv_ref,
                     q_ref, k_ref, v_ref, zg_ref, gb_ref, cross_ref, cbuf_ref, *, tm, n_heads):
    j = pl.program_id(1)
    tw = n_heads * HEAD_DIM
    h = _rms(x_ref[0], ng_ref[...]).astype(BF16)

    pre = _dot(h, wqkv_ref[...])

    @pl.when(j == 0)
    def _():
        cbuf_ref[0:8, :] = jnp.zeros((8, 3 * tw), F32)

    cbuf_ref[8:8 + tm, :] = pre
    w = convw_ref[...]
    acc = w[CONV_K - 1:CONV_K, :] * pre
    for kk in range(CONV_K - 1):
        acc = acc + w[kk:kk + 1, :] * cbuf_ref[pl.ds(8 - (CONV_K - 1) + kk, tm), :]
    cbuf_ref[0:8, :] = cbuf_ref[tm:tm + 8, :]
    qkv = acc * _sigmoid(acc)

    for hd in range(n_heads):
        sl = slice(hd * HEAD_DIM, (hd + 1) * HEAD_DIM)
        qh = qkv[:, hd * HEAD_DIM:(hd + 1) * HEAD_DIM]
        kh = qkv[:, tw + hd * HEAD_DIM:tw + (hd + 1) * HEAD_DIM]
        qn = qh * (lax.rsqrt(jnp.sum(qh * qh, axis=-1, keepdims=True) + EPS) * (HEAD_DIM ** -0.5))
        kn = kh * lax.rsqrt(jnp.sum(kh * kh, axis=-1, keepdims=True) + EPS)
        q_ref[0, :, sl] = qn.astype(BF16)
        k_ref[0, :, sl] = kn.astype(BF16)
    v_ref[0] = qkv[:, 2 * tw:].astype(BF16)

    z = _dot(h, wz_ref[...])
    zg_ref[0] = (z * _sigmoid(z)).astype(BF16)

    abt = _dot_nt(wabt_ref[...], h)
    g_t = -jnp.exp(alog_ref[...]) * _softplus(abt + dtb_ref[...])
    b_t = _sigmoid(abt)
    for hd in range(n_heads):
        gb_ref[0, hd, 0:1, :] = g_t[hd:hd + 1, :]
        gb_ref[0, hd, 1:2, :] = b_t[n_heads + hd:n_heads + hd + 1, :]

    qm = _dot(h, wqm_ref[...])
    kv = kv_ref[0]
    cross_ref[0] = _cross_attn(qm, kv[:, :MEM_WIDTH], kv[:, MEM_WIDTH:]).astype(BF16)


def _dn_front(x, ng, wqkv, wz, wqm, wabt, convw, alog, dtb, kv, *, tm):
    b, s, d = x.shape
    tw = wz.shape[1]
    n_heads = tw // HEAD_DIM
    mlen = kv.shape[1]
    const = lambda shape: pl.BlockSpec(shape, lambda i, j: (0,) * len(shape))
    tok = lambda w: pl.BlockSpec((1, tm, w), lambda i, j: (i, j, 0))
    return pl.pallas_call(
        functools.partial(_dn_front_kernel, tm=tm, n_heads=n_heads),
        grid=(b, s // tm),
        in_specs=[tok(d), const((1, d)), const((d, 3 * tw)), const((d, tw)), const((d, MEM_WIDTH)),
                  const((16, d)), const((CONV_K, 3 * tw)), const((16, 1)), const((16, 1)),
                  pl.BlockSpec((1, mlen, 2 * MEM_WIDTH), lambda i, j: (i, 0, 0))],
        out_specs=[tok(tw), tok(tw), tok(tw), tok(tw),
                   pl.BlockSpec((1, n_heads, 2, tm), lambda i, j: (i, 0, 0, j)),
                   tok(MEM_WIDTH)],
        out_shape=[jax.ShapeDtypeStruct((b, s, tw), BF16)] * 4
                  + [jax.ShapeDtypeStruct((b, n_heads, 2, s), F32),
                     jax.ShapeDtypeStruct((b, s, MEM_WIDTH), BF16)],
        scratch_shapes=[pltpu.VMEM((tm + 8, 3 * tw), F32)],
        compiler_params=_cparams(("parallel", "arbitrary")),
        name="dn_front",
    )(x, ng, wqkv, wz, wqm, wabt, convw, alog, dtb, kv)


def _tri_inverse(a_list, ii, jj):
    c = CHUNK
    eye = (ii == jj).astype(F32)
    blk = (ii >> 4) == (jj >> 4)
    n0 = [jnp.where(blk, -a, 0.0) for a in a_list]
    x = [eye + n for n in n0]
    nb = [n.astype(BF16) for n in n0]
    nb = [_dot(n, n).astype(BF16) for n in nb]
    for _ in range(2):
        r = [_dot(n, jnp.concatenate([n, xi.astype(BF16)], axis=1)) for n, xi in zip(nb, x)]
        nb = [ri[:, :c].astype(BF16) for ri in r]
        x = [xi + ri[:, c:] for xi, ri in zip(x, r)]
    x = [xi + _dot(n, xi.astype(BF16)) for n, xi in zip(nb, x)]
    shift = 4
    while (1 << shift) < c:
        inner = (ii >> shift) == (jj >> shift)
        outer = (ii >> (shift + 1)) == (jj >> (shift + 1))
        sel = outer & jnp.logical_not(inner)
        xb = [xi.astype(BF16) for xi in x]
        lx = [_dot(jnp.where(sel, a, 0.0).astype(BF16), xi).astype(BF16) for a, xi in zip(a_list, xb)]
        x = [xi - _dot(xbi, li) for xi, xbi, li in zip(x, xb, lx)]
        shift += 1
    return x


def _dn_intra_kernel(q_ref, k_ref, v_ref, gb_ref, u_ref, wp_ref, ae_ref, kts_ref, egl_ref, *, n_chunks):
    c = CHUNK
    ii = lax.broadcasted_iota(jnp.int32, (c, c), 0)
    jj = lax.broadcasted_iota(jnp.int32, (c, c), 1)
    tril = ii >= jj
    strict = ii > jj
    eye = ii == jj
    upper_incl = (ii <= jj).astype(BF16)
    chunks = range(n_chunks)
    rows = [slice(n * c, (n + 1) * c) for n in chunks]
    k = [k_ref[0, r, :] for r in rows]
    g_row = [gb_ref[0, 0, 0:1, r] for r in rows]
    b_row = [gb_ref[0, 0, 1:2, r] for r in rows]
    kq = [_dot_nt(jnp.concatenate([k[n], q_ref[0, rows[n], :]], axis=0), k[n]) for n in chunks]
    gc_row = []
    for n in chunks:
        hi, mid, lo = _split3(jnp.broadcast_to(g_row[n], (8, c)))
        gc8 = _dot(hi, upper_incl) + _dot(mid, upper_incl) + _dot(lo, upper_incl)
        gc_row.append(gc8[0:1, :])
    gl = [jnp.sum(g, axis=-1, keepdims=True) for g in g_row]
    a_list, b_rs, e_rs = [], [], []
    for n in chunks:
        g_r = jnp.broadcast_to(gc_row[n], (c, c))
        b_r = jnp.broadcast_to(b_row[n], (c, c))
        decay = jnp.where(tril, jnp.exp(g_r.T - g_r), 0.0)
        a_list.append(jnp.where(strict, kq[n][:c] * decay, 0.0) * b_r.T)
        ae_ref[0, 0, rows[n], 0:c] = jnp.where(tril, kq[n][c:] * decay, 0.0).astype(BF16)
        e_r = jnp.exp(g_r)
        ae_ref[0, 0, rows[n], c:2 * c] = jnp.where(eye, e_r, 0.0).astype(BF16)
        kts_ref[0, 0, :, rows[n]] = (k[n].astype(F32).T * jnp.exp(gl[n] - gc_row[n])).astype(BF16)
        egl_ref[0, 0, n] = jnp.broadcast_to(jnp.exp(gl[n]), (1, LANES))
        b_rs.append(b_r)
        e_rs.append(e_r)
    t = _tri_inverse(a_list, ii, jj)
    tb = [t[n] * b_rs[n] for n in chunks]
    u = [_dot(tb[n].astype(BF16), v_ref[0, rows[n], :]) for n in chunks]
    for n in chunks:
        u_ref[0, 0, rows[n], :] = u[n].astype(BF16)
        wp_ref[0, 0, rows[n], :] = (tb[n] * e_rs[n]).astype(BF16)


def _dn_intra(q, k, v, gb, *, tb):
    b, s, tw = q.shape
    n_heads = tw // HEAD_DIM
    n_chunks = tb // CHUNK
    qkv_spec = pl.BlockSpec((1, tb, HEAD_DIM), lambda i, h, j: (i, j, h))
    hs = lambda w: pl.BlockSpec((1, 1, tb, w), lambda i, h, j: (i, h, j, 0))
    return pl.pallas_call(
        functools.partial(_dn_intra_kernel, n_chunks=n_chunks),
        grid=(b, n_heads, s // tb),
        in_specs=[qkv_spec, qkv_spec, qkv_spec,
                  pl.BlockSpec((1, 1, 2, tb), lambda i, h, j: (i, h, 0, j))],
        out_specs=[hs(HEAD_DIM), hs(CHUNK), hs(2 * CHUNK),
                   pl.BlockSpec((1, 1, HEAD_DIM, tb), lambda i, h, j: (i, h, 0, j)),
                   pl.BlockSpec((1, 1, n_chunks, 1, LANES), lambda i, h, j: (i, h, j, 0, 0))],
        out_shape=[jax.ShapeDtypeStruct((b, n_heads, s, HEAD_DIM), BF16),
                   jax.ShapeDtypeStruct((b, n_heads, s, CHUNK), BF16),
                   jax.ShapeDtypeStruct((b, n_heads, s, 2 * CHUNK), BF16),
                   jax.ShapeDtypeStruct((b, n_heads, HEAD_DIM, s), BF16),
                   jax.ShapeDtypeStruct((b, n_heads, s // CHUNK, 1, LANES), F32)],
        compiler_params=_cparams(("parallel", "parallel", "parallel")),
        name="dn_intra",
    )(q, k, v, gb)


def _dn_scan_kernel(q_ref, k_ref, u_ref, wp_ref, ae_ref, kts_ref, egl_ref, o_ref, s_ref, *, n_chunks, n_heads):
    c = CHUNK

    @pl.when(pl.program_id(1) == 0)
    def _():
        s_ref[...] = jnp.zeros(s_ref.shape, F32)

    heads = range(n_heads)
    cols = [slice(hd * HEAD_DIM, (hd + 1) * HEAD_DIM) for hd in heads]
    states = [s_ref[hd] for hd in heads]
    for n in range(n_chunks):
        rows = slice(n * c, (n + 1) * c)
        kqs = [_dot(jnp.concatenate([k_ref[0, rows, cols[hd]], q_ref[0, rows, cols[hd]]], axis=0),
                    states[hd].astype(BF16)) for hd in heads]
        wks = [_dot(wp_ref[0, hd, rows, :], kqs[hd][:c].astype(BF16)) for hd in heads]
        xb = [(u_ref[0, hd, rows, :].astype(F32) - wks[hd]).astype(BF16) for hd in heads]
        upd = [_dot(kts_ref[0, hd, :, rows], xb[hd]) for hd in heads]
        states = [states[hd] * egl_ref[0, hd, n] + upd[hd] for hd in heads]
        out = [_dot(ae_ref[0, hd, rows, :], jnp.concatenate([xb[hd], kqs[hd][c:].astype(BF16)], axis=0))
               for hd in heads]
        for hd in heads:
            o_ref[0, rows, cols[hd]] = out[hd].astype(BF16)
    for hd in heads:
        s_ref[hd] = states[hd]


def _dn_scan(q, k, u, wp, ae, kts, egl, *, tb):
    b, s, tw = q.shape
    n_heads = tw // HEAD_DIM
    n_chunks = tb // CHUNK
    tok = pl.BlockSpec((1, tb, tw), lambda i, j: (i, j, 0))
    hs = lambda w: pl.BlockSpec((1, n_heads, tb, w), lambda i, j: (i, 0, j, 0))
    return pl.pallas_call(
        functools.partial(_dn_scan_kernel, n_chunks=n_chunks, n_heads=n_heads),
        grid=(b, s // tb),
        in_specs=[tok, tok, hs(HEAD_DIM), hs(CHUNK), hs(2 * CHUNK),
                  pl.BlockSpec((1, n_heads, HEAD_DIM, tb), lambda i, j: (i, 0, 0, j)),
                  pl.BlockSpec((1, n_heads, n_chunks, 1, LANES), lambda i, j: (i, 0, j, 0, 0))],
        out_specs=tok,
        out_shape=jax.ShapeDtypeStruct((b, s, tw), BF16),
        scratch_shapes=[pltpu.VMEM((n_heads, HEAD_DIM, HEAD_DIM), F32)],
        compiler_params=_cparams(("parallel", "arbitrary")),
        name="dn_scan",
    )(q, k, u, wp, ae, kts, egl)


def _mixer_tail(x, mix_b, cross_b, wo_mix_ref, wo_mem_ref, fg_ref, wr_ref, br_ref,
                x1_ref, rows_ref, code_ref, cnt_ref, carry_ref):
    tm, d = x.shape
    half = d // 2
    y = _dot(mix_b, wo_mix_ref[...]) + _dot(cross_b, wo_mem_ref[...])
    x1 = x + y
    x1_ref[...] = x1
    h2 = _rms(x1, fg_ref[...]).astype(BF16)
    logits = _dot(h2, wr_ref[...]) + br_ref[...]
    cls, w_lo, w_hi = _router(logits)

    bits = lax.bitcast_convert_type(h2.astype(F32), jnp.uint32)
    word = (bits[:, :half] >> 16) | bits[:, half:]
    for c in range(half // LANES):
        rows_ref[pl.ds(c, tm, stride=SUBLANES), :] = word[:, c * LANES:(c + 1) * LANES]
    n_feat_rows = half // LANES
    rows_ref[pl.ds(n_feat_rows, tm, stride=SUBLANES), :] = lax.bitcast_convert_type(
        jnp.broadcast_to(w_lo, (tm, LANES)), jnp.uint32)
    rows_ref[pl.ds(n_feat_rows + 1, tm, stride=SUBLANES), :] = lax.bitcast_convert_type(
        jnp.broadcast_to(w_hi, (tm, LANES)), jnp.uint32)
    for c in range(n_feat_rows + 2, SUBLANES):
        rows_ref[pl.ds(c, tm, stride=SUBLANES), :] = jnp.zeros((tm, LANES), jnp.uint32)

    @pl.when(pl.program_id(0) == 0)
    def _():
        carry_ref[...] = jnp.zeros(carry_ref.shape, F32)

    lane = lax.broadcasted_iota(jnp.int32, (tm, LANES), 1)
    onehot = lane == cls
    ii = lax.broadcasted_iota(jnp.int32, (tm, tm), 0)
    jj = lax.broadcasted_iota(jnp.int32, (tm, tm), 1)
    earlier = (ii > jj).astype(BF16)
    before = _dot(earlier, onehot.astype(BF16)) + carry_ref[...]
    rank = jnp.sum(jnp.where(onehot, before, 0.0), axis=-1, keepdims=True).astype(jnp.int32)
    code_ref[...] = (cls << RANK_BITS) | rank
    carry_ref[...] = carry_ref[...] + jnp.sum(onehot.astype(F32), axis=0, keepdims=True)
    cnt_ref[...] = carry_ref[...]


def _tail_out_specs(tm, d):
    return [pl.BlockSpec((tm, d), lambda i: (i, 0)),
            pl.BlockSpec((tm * SUBLANES, LANES), lambda i: (i, 0)),
            pl.BlockSpec((tm, 1), lambda i: (i, 0)),
            pl.BlockSpec((1, LANES), lambda i: (0, 0))]


def _tail_out_shapes(t, d):
    return [jax.ShapeDtypeStruct((t, d), F32), jax.ShapeDtypeStruct((t * SUBLANES, LANES), jnp.uint32),
            jax.ShapeDtypeStruct((t, 1), jnp.int32), jax.ShapeDtypeStruct((1, LANES), F32)]


def _dn_out_kernel(x_ref, o_ref, zg_ref, cross_ref, og_ref, wo_mix_ref, wo_mem_ref, fg_ref, wr_ref, br_ref,
                   x1_ref, rows_ref, code_ref, cnt_ref, carry_ref, *, n_heads):
    og = og_ref[...]
    parts = []
    for hd in range(n_heads):
        cols = slice(hd * HEAD_DIM, (hd + 1) * HEAD_DIM)
        oh = o_ref[:, cols].astype(F32)
        on = oh * lax.rsqrt(jnp.mean(oh * oh, axis=-1, keepdims=True) + EPS) * og
        parts.append((on * zg_ref[:, cols].astype(F32)).astype(BF16))
    mix = jnp.concatenate(parts, axis=1)
    _mixer_tail(x_ref[...], mix, cross_ref[...], wo_mix_ref, wo_mem_ref, fg_ref, wr_ref, br_ref,
                x1_ref, rows_ref, code_ref, cnt_ref, carry_ref)


def _dn_out(x2d, o2d, zg2d, cross2d, og, wo_mix, wo_mem, fg, wr, br, *, tm):
    t, d = x2d.shape
    tw = o2d.shape[1]
    n_heads = tw // HEAD_DIM
    const = lambda shape: pl.BlockSpec(shape, lambda i: (0,) * len(shape))
    tok = lambda w: pl.BlockSpec((tm, w), lambda i: (i, 0))
    return pl.pallas_call(
        functools.partial(_dn_out_kernel, n_heads=n_heads),
        grid=(t // tm,),
        in_specs=[tok(d), tok(tw), tok(tw), tok(MEM_WIDTH), const((1, HEAD_DIM)), const((tw, d)),
                  const((MEM_WIDTH, d)), const((1, d)), const((d, LANES)), const((1, LANES))],
        out_specs=_tail_out_specs(tm, d),
        out_shape=_tail_out_shapes(t, d),
        scratch_shapes=[pltpu.VMEM((1, LANES), F32)],
        compiler_params=_cparams(("arbitrary",)),
        name="dn_out",
    )(x2d, o2d, zg2d, cross2d, og, wo_mix, wo_mem, fg, wr, br)


def _row_tile(ref, r):
    return ref.at[pl.ds(pl.multiple_of(r * SUBLANES, SUBLANES), SUBLANES), :]


def _dispatch_kernel(pos_ref, rows_hbm, init_hbm, xs_hbm, sem, *, td):
    del init_hbm
    base = pl.program_id(0) * td

    def issue(r, carry):
        pltpu.make_async_copy(_row_tile(rows_hbm, base + r), _row_tile(xs_hbm, pos_ref[r]), sem).start()
        return carry

    lax.fori_loop(0, td, issue, 0, unroll=8)
    pltpu.make_async_copy(rows_hbm.at[pl.ds(0, td * SUBLANES), :], xs_hbm.at[pl.ds(0, td * SUBLANES), :], sem).wait()


def _dispatch(pos, rows, n_slots, *, td):
    t = pos.shape[0]
    init = jnp.zeros((n_slots * SUBLANES, LANES), jnp.uint32)
    return pl.pallas_call(
        functools.partial(_dispatch_kernel, td=td),
        grid=(t // td,),
        in_specs=[pl.BlockSpec((td,), lambda i: (i,), memory_space=pltpu.SMEM),
                  pl.BlockSpec(memory_space=pl.ANY), pl.BlockSpec(memory_space=pl.ANY)],
        out_specs=pl.BlockSpec(memory_space=pl.ANY),
        out_shape=jax.ShapeDtypeStruct(init.shape, jnp.uint32),
        scratch_shapes=[pltpu.SemaphoreType.DMA(())],
        input_output_aliases={2: 0},
        compiler_params=_cparams(("arbitrary",)),
        name="moe_dispatch",
    )(pos, rows, init)


def _experts_kernel(ea_ref, eb_ref, used_ref, xs_ref, wgu_a_ref, wgu_b_ref, wd_a_ref, wd_b_ref, ys_ref,
                    *, tile, d, d_expert):
    i = pl.program_id(0)
    n_feat_rows = d // 2 // LANES

    @pl.when(used_ref[i] == 1)
    def _():
        words = [xs_ref[pl.ds(c, tile, stride=SUBLANES), :] for c in range(n_feat_rows)]
        low = [lax.bitcast_convert_type(w << 16, F32) for w in words]
        high = [lax.bitcast_convert_type(w & jnp.uint32(0xFFFF0000), F32) for w in words]
        h = jnp.concatenate(low + high, axis=1).astype(BF16)
        w_lo = lax.bitcast_convert_type(xs_ref[pl.ds(n_feat_rows, tile, stride=SUBLANES), :], F32)
        w_hi = lax.bitcast_convert_type(xs_ref[pl.ds(n_feat_rows + 1, tile, stride=SUBLANES), :], F32)
        reps = d_expert // LANES
        parts = []
        for wgu_ref, wt in ((wgu_a_ref, w_lo), (wgu_b_ref, w_hi)):
            gu = _dot(h, wgu_ref[0])
            gpart, upart = gu[:, :d_expert], gu[:, d_expert:]
            parts.append((gpart * _sigmoid(gpart) * upart * jnp.concatenate([wt] * reps, axis=1)).astype(BF16))
        y = _dot(parts[0], wd_a_ref[0]) + _dot(parts[1], wd_b_ref[0])
        for c in range(d // LANES):
            ys_ref[pl.ds(c, tile, stride=SUBLANES), :] = y[:, c * LANES:(c + 1) * LANES]

    @pl.when(used_ref[i] == 0)
    def _():
        ys_ref[...] = jnp.zeros(ys_ref.shape, F32)


def _experts(tile_lo, tile_hi, tile_used, xs, wgu, wd, *, d):
    n_tiles = tile_lo.shape[0]
    de2 = wgu.shape[2]
    blk = MOE_TILE * SUBLANES
    grid_spec = pltpu.PrefetchScalarGridSpec(
        num_scalar_prefetch=3,
        grid=(n_tiles,),
        in_specs=[pl.BlockSpec((blk, LANES), lambda i, ea, eb, us: (i, 0)),
                  pl.BlockSpec((1, d, de2), lambda i, ea, eb, us: (ea[i], 0, 0)),
                  pl.BlockSpec((1, d, de2), lambda i, ea, eb, us: (eb[i], 0, 0)),
                  pl.BlockSpec((1, de2 // 2, d), lambda i, ea, eb, us: (ea[i], 0, 0)),
                  pl.BlockSpec((1, de2 // 2, d), lambda i, ea, eb, us: (eb[i], 0, 0))],
        out_specs=pl.BlockSpec((blk, LANES), lambda i, ea, eb, us: (i, 0)))
    return pl.pallas_call(
        functools.partial(_experts_kernel, tile=MOE_TILE, d=d, d_expert=de2 // 2),
        grid_spec=grid_spec,
        out_shape=jax.ShapeDtypeStruct((n_tiles * blk, LANES), F32),
        compiler_params=_cparams(("arbitrary",)),
        name="moe_experts",
    )(tile_lo, tile_hi, tile_used, xs, wgu, wgu, wd, wd)


def _combine_kernel(pos_ref, ys_hbm, x_ref, fin_ref, out_ref, buf_ref, sem, *, tc, d, final_norm):
    def issue(r, carry):
        pltpu.make_async_copy(_row_tile(ys_hbm, pos_ref[r]), _row_tile(buf_ref, r), sem).start()
        return carry

    lax.fori_loop(0, tc, issue, 0, unroll=8)
    pltpu.make_async_copy(ys_hbm.at[pl.ds(0, tc * SUBLANES), :], buf_ref, sem).wait()
    y = jnp.concatenate([buf_ref[pl.ds(c, tc, stride=SUBLANES), :] for c in range(d // LANES)], axis=1)
    out = x_ref[...] + y
    if final_norm:
        out = _rms(out, fin_ref[...])
    out_ref[...] = out


def _combine(pos, ys, x1, fin_g, *, tc, final_norm):
    t, d = x1.shape
    return pl.pallas_call(
        functools.partial(_combine_kernel, tc=tc, d=d, final_norm=final_norm),
        grid=(t // tc,),
        in_specs=[pl.BlockSpec((tc,), lambda i: (i,), memory_space=pltpu.SMEM),
                  pl.BlockSpec(memory_space=pl.ANY),
                  pl.BlockSpec((tc, d), lambda i: (i, 0)),
                  pl.BlockSpec((1, d), lambda i: (0, 0))],
        out_specs=pl.BlockSpec((tc, d), lambda i: (i, 0)),
        out_shape=jax.ShapeDtypeStruct((t, d), F32),
        scratch_shapes=[pltpu.VMEM((tc * SUBLANES, LANES), F32), pltpu.SemaphoreType.DMA(())],
        compiler_params=_cparams(("arbitrary",)),
        name="moe_combine",
    )(pos, ys, x1, fin_g)


def _moe_sparse(x1, rows, code, counts, wgu, wd, fin_g, *, final_norm):
    t, d = x1.shape
    n_tiles = t // MOE_TILE + N_CLASSES
    cnt = counts[0, :N_CLASSES].astype(jnp.int32)
    padded = (cnt + MOE_TILE - 1) // MOE_TILE * MOE_TILE
    ends = jnp.cumsum(padded)
    starts = ends - padded
    code = code.reshape(t)
    pos = starts[code >> RANK_BITS] + (code & ((1 << RANK_BITS) - 1))
    tile_start = jnp.arange(n_tiles, dtype=jnp.int32) * MOE_TILE
    tile_cls = jnp.minimum(jnp.searchsorted(ends, tile_start, side="right"), N_CLASSES - 1).astype(jnp.int32)
    lo_ids, hi_ids = _class_experts()
    tile_lo = jnp.asarray(lo_ids, jnp.int32)[tile_cls]
    tile_hi = jnp.asarray(hi_ids, jnp.int32)[tile_cls]
    tile_used = (tile_start < ends[-1]).astype(jnp.int32)
    xs = _dispatch(pos, rows, n_tiles * MOE_TILE, td=2048 if t % 2048 == 0 else 1024)
    ys = _experts(tile_lo, tile_hi, tile_used, xs, wgu, wd, d=d)
    return _combine(pos, ys, x1, fin_g, tc=512, final_norm=final_norm)


def _gm_layer_kernel(x_ref, ng_ref, win_ref, wqm_ref, lng_ref, lnb_ref, wsp_ref, bsp_ref, kv_ref,
                     wo_mix_ref, wo_mem_ref, fg_ref, wr_ref, br_ref, x1_ref, rows_ref, code_ref, cnt_ref, carry_ref,
                     *, tm, n_groups):
    c = CHUNK
    tw = n_groups * HEAD_DIM
    x = x_ref[...]
    h = _rms(x, ng_ref[...]).astype(BF16)
    proj = _dot(h, win_ref[...])
    uv = 0.5 * proj * (1.0 + lax.erf(proj * (2.0 ** -0.5)))
    u, v = uv[:, :tw], uv[:, tw:]
    mu = jnp.mean(v, axis=-1, keepdims=True)
    vc = v - mu
    var = jnp.mean(vc * vc, axis=-1, keepdims=True)
    vn = (vc * lax.rsqrt(var + EPS) * lng_ref[...] + lnb_ref[...]).astype(BF16)
    ii = lax.broadcasted_iota(jnp.int32, (c, c), 0)
    jj = lax.broadcasted_iota(jnp.int32, (c, c), 1)
    tril = ii >= jj
    row_parts = []
    for n in range(tm // c):
        col_parts = []
        for g in range(n_groups):
            wc = jnp.where(tril, wsp_ref[g], jnp.zeros((c, c), BF16))
            sg = _dot(wc, vn[n * c:(n + 1) * c, g * HEAD_DIM:(g + 1) * HEAD_DIM]) + bsp_ref[g]
            col_parts.append(sg)
        row_parts.append(jnp.concatenate(col_parts, axis=1))
    sgate = jnp.concatenate(row_parts, axis=0)
    mix = (u * sgate).astype(BF16)
    qm = _dot(h, wqm_ref[...])
    kv = kv_ref[0]
    cross = _cross_attn(qm, kv[:, :MEM_WIDTH], kv[:, MEM_WIDTH:]).astype(BF16)
    _mixer_tail(x, mix, cross, wo_mix_ref, wo_mem_ref, fg_ref, wr_ref, br_ref,
                x1_ref, rows_ref, code_ref, cnt_ref, carry_ref)


def _gm_layer(x2d, ng, win, wqm, lng, lnb, wsp, bsp, kv, kv_layer, wo_mix, wo_mem, fg, wr, br, *, tm, seq):
    t, d = x2d.shape
    tw = lng.shape[1]
    n_groups = tw // HEAD_DIM
    mlen = kv.shape[1]
    tiles_per_seq = seq // tm
    const = lambda shape: pl.BlockSpec(shape, lambda i: (0,) * len(shape))
    tok = lambda w: pl.BlockSpec((tm, w), lambda i: (i, 0))
    return pl.pallas_call(
        functools.partial(_gm_layer_kernel, tm=tm, n_groups=n_groups),
        grid=(t // tm,),
        in_specs=[tok(d), const((1, d)), const((d, 2 * tw)), const((d, MEM_WIDTH)), const((1, tw)), const((1, tw)),
                  const((n_groups, CHUNK, CHUNK)), const((n_groups, CHUNK, HEAD_DIM)),
                  pl.BlockSpec((1, mlen, 2 * MEM_WIDTH), lambda i: (i // tiles_per_seq, 0, kv_layer)),
                  const((tw, d)), const((MEM_WIDTH, d)), const((1, d)), const((d, LANES)), const((1, LANES))],
        out_specs=_tail_out_specs(tm, d),
        out_shape=_tail_out_shapes(t, d),
        scratch_shapes=[pltpu.VMEM((1, LANES), F32)],
        compiler_params=_cparams(("arbitrary",)),
        name="gm_layer",
    )(x2d, ng, win, wqm, lng, lnb, wsp, bsp, kv, wo_mix, wo_mem, fg, wr, br)


def _router_params(w_group, b_group, w_router, b_router):
    d = w_group.shape[0]
    pad = LANES - N_EXPERTS - MOE_GROUPS
    wr = jnp.concatenate([w_router, w_group, jnp.zeros((d, pad), F32)], axis=1).astype(BF16)
    br = jnp.concatenate([b_router, b_group, jnp.zeros((pad,), F32)])[None, :]
    return wr, br


def _col16(vec):
    return jnp.concatenate([vec, jnp.zeros((16 - vec.shape[0],), F32)])[:, None]


def kernel(x, mem, mem_norm_g, mix_norm_g, w_out, w_mem_kv, dn_w_in, dn_conv_w, dn_a_log, dn_dt_bias, dn_o_norm_g,
           gm_w_in, gm_ln_g, gm_ln_b, gm_w_spatial, gm_b_spatial, ffn_norm_g, moe_w_group, moe_b_group,
           moe_w_router, moe_b_router, moe_w_gate, moe_w_up, moe_w_down, final_norm_g):
    b, s, d = x.shape
    tw = d - MEM_WIDTH
    n_heads = tw // HEAD_DIM
    t = b * s

    kv = _memkv(mem, mem_norm_g[None, :], jnp.concatenate([w_mem_kv[0], w_mem_kv[1]], axis=1).astype(BF16))

    w_in = dn_w_in[0]
    o1, o2, o3, o4 = 3 * tw, 4 * tw, 4 * tw + n_heads, 4 * tw + 2 * n_heads
    wabt = jnp.concatenate([w_in[:, o2:o4].T, jnp.zeros((16 - 2 * n_heads, d), F32)], axis=0).astype(BF16)
    q, k, v, zg, gb, cross = _dn_front(
        x, mix_norm_g[0][None, :], w_in[:, :o1].astype(BF16), w_in[:, o1:o2].astype(BF16),
        w_in[:, o4:].astype(BF16), wabt, dn_conv_w[0], _col16(dn_a_log[0]), _col16(dn_dt_bias[0]), kv, tm=512)
    u, wp, ae, kts, egl = _dn_intra(q, k, v, gb, tb=1024)
    o = _dn_scan(q, k, u, wp, ae, kts, egl, tb=512)
    wr0, br0 = _router_params(moe_w_group[0], moe_b_group[0], moe_w_router[0], moe_b_router[0])
    wo0 = w_out[0].astype(BF16)
    x1, rows0, code0, cnt0 = _dn_out(
        x.reshape(t, d), o.reshape(t, tw), zg.reshape(t, tw), cross.reshape(t, MEM_WIDTH),
        dn_o_norm_g[0][None, :], wo0[:tw], wo0[tw:], ffn_norm_g[0][None, :], wr0, br0, tm=512)
    wgu0 = jnp.concatenate([moe_w_gate[0], moe_w_up[0]], axis=2).astype(BF16)
    x2 = _moe_sparse(x1, rows0, code0, cnt0, wgu0, moe_w_down[0].astype(BF16), final_norm_g[None, :],
                     final_norm=False)

    win1 = gm_w_in[0]
    wr1, br1 = _router_params(moe_w_group[1], moe_b_group[1], moe_w_router[1], moe_b_router[1])
    wo1 = w_out[1].astype(BF16)
    bsp = jnp.broadcast_to(gm_b_spatial[0][:, :, None], (n_heads, CHUNK, HEAD_DIM))
    x3, rows1, code1, cnt1 = _gm_layer(
        x2, mix_norm_g[1][None, :], win1[:, :2 * tw].astype(BF16), win1[:, 2 * tw:].astype(BF16),
        gm_ln_g[0][None, :], gm_ln_b[0][None, :], gm_w_spatial[0].astype(BF16), bsp, kv, 1,
        wo1[:tw], wo1[tw:], ffn_norm_g[1][None, :], wr1, br1, tm=512, seq=s)
    wgu1 = jnp.concatenate([moe_w_gate[1], moe_w_up[1]], axis=2).astype(BF16)
    out = _moe_sparse(x3, rows1, code1, cnt1, wgu1, moe_w_down[1].astype(BF16), final_norm_g[None, :],
                      final_norm=True)
    return out.reshape(b, s, d)
```

```python
import functools

import jax
import jax.numpy as jnp
from jax import lax
from jax.experimental import pallas as pl
from jax.experimental.pallas import tpu as pltpu

F32 = jnp.float32
BF16 = jnp.bfloat16
EPS = 1e-6

MEM_HEADS = 4
MEM_HEAD_DIM = 64
MEM_WIDTH = MEM_HEADS * MEM_HEAD_DIM
HEAD_DIM = 128
CONV_K = 4
CHUNK = 128
MOE_GROUPS = 4
EXPERTS_PER_GROUP = 4
N_EXPERTS = MOE_GROUPS * EXPERTS_PER_GROUP
PAIRS_PER_GROUP = EXPERTS_PER_GROUP * (EXPERTS_PER_GROUP - 1) // 2
N_CLASSES = MOE_GROUPS * PAIRS_PER_GROUP
MOE_TILE = 256
LANES = 128
SUBLANES = 8
RANK_BITS = 16
DMA_PRIORITIES = 2
VMEM_LIMIT = 56 * 1024 * 1024

NT_DIMS = (((1,), (1,)), ((), ()))


def _dot(a, b):
    return jnp.dot(a, b, preferred_element_type=F32)


def _dot_nt(a, b):
    return lax.dot_general(a, b, NT_DIMS, preferred_element_type=F32)


def _rms(x, g):
    return x * lax.rsqrt(jnp.mean(x * x, axis=-1, keepdims=True) + EPS) * g


def _sigmoid(x):
    return 1.0 / (1.0 + jnp.exp(-x))


def _softplus(x):
    return jnp.maximum(x, 0.0) + jnp.log1p(jnp.exp(-jnp.abs(x)))


def _split3(x):
    hi = x.astype(BF16)
    r = x - hi.astype(F32)
    mid = r.astype(BF16)
    lo = (r - mid.astype(F32)).astype(BF16)
    return hi, mid, lo


def _cparams(sem):
    return pltpu.CompilerParams(dimension_semantics=sem, vmem_limit_bytes=VMEM_LIMIT)


def _memkv_kernel(mem_ref, g_ref, w_ref, kv_ref):
    mn = _rms(mem_ref[0], g_ref[...]).astype(BF16)
    kv_ref[0] = _dot(mn, w_ref[...]).astype(BF16)


def _memkv(mem, g, w_all):
    b, m, d = mem.shape
    n = w_all.shape[1]
    return pl.pallas_call(
        _memkv_kernel,
        grid=(b,),
        in_specs=[pl.BlockSpec((1, m, d), lambda i: (i, 0, 0)),
                  pl.BlockSpec((1, d), lambda i: (0, 0)),
                  pl.BlockSpec((d, n), lambda i: (0, 0))],
        out_specs=pl.BlockSpec((1, m, n), lambda i: (i, 0, 0)),
        out_shape=jax.ShapeDtypeStruct((b, m, n), BF16),
        compiler_params=_cparams(("parallel",)),
        name="memkv",
    )(mem, g, w_all)


def _cross_attn(qm, kmem, vmem):
    lane = lax.broadcasted_iota(jnp.int32, (1, MEM_WIDTH), 1)
    out = jnp.zeros(qm.shape, F32)
    for h in range(MEM_HEADS):
        msk = (lane >= h * MEM_HEAD_DIM) & (lane < (h + 1) * MEM_HEAD_DIM)
        qh = jnp.where(msk, qm, 0.0).astype(BF16)
        s = _dot_nt(qh, kmem) * (MEM_HEAD_DIM ** -0.5)
        p = jnp.exp(s - jnp.max(s, axis=-1, keepdims=True))
        inv = 1.0 / jnp.sum(p, axis=-1, keepdims=True)
        vh = jnp.where(msk, vmem, jnp.zeros_like(vmem))
        out = out + _dot(p.astype(BF16), vh) * inv
    return out


def _router_class(logits):
    lane = lax.broadcasted_iota(jnp.int32, logits.shape, 1)
    neg = jnp.float32(-jnp.inf)
    big = jnp.int32(1 << 20)
    is_g = (lane >= N_EXPERTS) & (lane < N_EXPERTS + MOE_GROUPS)
    gl = jnp.where(is_g, logits, neg)
    gmax = jnp.max(gl, axis=-1, keepdims=True)
    g_idx = jnp.min(jnp.where(gl == gmax, lane, big), axis=-1, keepdims=True) - N_EXPERTS
    lo = g_idx * EXPERTS_PER_GROUP
    in_grp = (lane >= lo) & (lane < lo + EXPERTS_PER_GROUP)
    el = jnp.where(in_grp, logits, neg)
    m1 = jnp.max(el, axis=-1, keepdims=True)
    i1 = jnp.min(jnp.where(el == m1, lane, big), axis=-1, keepdims=True)
    el2 = jnp.where(lane == i1, neg, el)
    m2 = jnp.max(el2, axis=-1, keepdims=True)
    i2 = jnp.min(jnp.where(el2 == m2, lane, big), axis=-1, keepdims=True)
    e_lo = jnp.minimum(i1, i2) - lo
    e_hi = jnp.maximum(i1, i2) - lo
    pair_off = jnp.where(e_lo == 0, 0, jnp.where(e_lo == 1, 3, 5))
    return g_idx * PAIRS_PER_GROUP + pair_off + e_hi - e_lo - 1


def _pair_gates(logits, e_lo, e_hi):
    lane = lax.broadcasted_iota(jnp.int32, logits.shape, 1)
    pick = lambda idx: jnp.sum(jnp.where(lane == idx, logits, 0.0), axis=-1, keepdims=True)
    l_lo, l_hi = pick(e_lo), pick(e_hi)
    g_sel = pick(N_EXPERTS + e_lo // EXPERTS_PER_GROUP)
    is_g = (lane >= N_EXPERTS) & (lane < N_EXPERTS + MOE_GROUPS)
    p_grp = 1.0 / jnp.sum(jnp.where(is_g, jnp.exp(logits - g_sel), 0.0), axis=-1, keepdims=True)
    return p_grp / (1.0 + jnp.exp(l_hi - l_lo)), p_grp / (1.0 + jnp.exp(l_lo - l_hi))


def _class_experts():
    lo_ids, hi_ids = [], []
    for g in range(MOE_GROUPS):
        for a in range(EXPERTS_PER_GROUP):
            for b in range(a + 1, EXPERTS_PER_GROUP):
                lo_ids.append(g * EXPERTS_PER_GROUP + a)
                hi_ids.append(g * EXPERTS_PER_GROUP + b)
    return lo_ids, hi_ids


def _dn_front_kernel(x_ref, ng_ref, wqkv_ref, wz_ref, wqm_ref, wabt_ref, convw_ref, alog_ref, dtb_ref, kv_ref,
                     q_ref, k_ref, v_ref, zg_ref, gb_ref, cross_ref, cbuf_ref, *, tm, n_heads):
    j = pl.program_id(1)
    tw = n_heads * HEAD_DIM
    h = _rms(x_ref[0], ng_ref[...]).astype(BF16)

    @pl.when(j == 0)
    def _():
        cbuf_ref[0:8, :] = jnp.zeros((8, 3 * tw), F32)

    gw = 2 * HEAD_DIM
    out_refs = (q_ref, k_ref, v_ref)
    scales = (HEAD_DIM ** -0.5, 1.0, None)
    for grp in range(3 * tw // gw):
        cols = slice(grp * gw, (grp + 1) * gw)
        pre = _dot(h, wqkv_ref[:, cols])
        cbuf_ref[8:8 + tm, cols] = pre
        w = convw_ref[:, cols]
        acc = w[CONV_K - 1:CONV_K, :] * pre
        for kk in range(CONV_K - 1):
            acc = acc + w[kk:kk + 1, :] * cbuf_ref[pl.ds(8 - (CONV_K - 1) + kk, tm), cols]
        cbuf_ref[0:8, cols] = cbuf_ref[tm:tm + 8, cols]
        act = acc * _sigmoid(acc)
        which, first = divmod(grp * gw, tw)
        for half in range(gw // HEAD_DIM):
            a = act[:, half * HEAD_DIM:(half + 1) * HEAD_DIM]
            if scales[which] is not None:
                a = a * (lax.rsqrt(jnp.sum(a * a, axis=-1, keepdims=True) + EPS) * scales[which])
            dst = first + half * HEAD_DIM
            out_refs[which][0, :, dst:dst + HEAD_DIM] = a.astype(BF16)

    z = _dot(h, wz_ref[...])
    zg_ref[0] = (z * _sigmoid(z)).astype(BF16)

    abt = _dot_nt(wabt_ref[...], h)
    g_t = -jnp.exp(alog_ref[...]) * _softplus(abt + dtb_ref[...])
    b_t = _sigmoid(abt)
    for hd in range(n_heads):
        gb_ref[0, hd, 0:1, :] = g_t[hd:hd + 1, :]
        gb_ref[0, hd, 1:2, :] = b_t[n_heads + hd:n_heads + hd + 1, :]

    qm = _dot(h, wqm_ref[...])
    kv = kv_ref[0]
    cross_ref[0] = _cross_attn(qm, kv[:, :MEM_WIDTH], kv[:, MEM_WIDTH:]).astype(BF16)


def _dn_front(x, ng, wqkv, wz, wqm, wabt, convw, alog, dtb, kv, *, tm):
    b, s, d = x.shape
    tw = wz.shape[1]
    n_heads = tw // HEAD_DIM
    mlen = kv.shape[1]
    const = lambda shape: pl.BlockSpec(shape, lambda i, j: (0,) * len(shape))
    tok = lambda w: pl.BlockSpec((1, tm, w), lambda i, j: (i, j, 0))
    return pl.pallas_call(
        functools.partial(_dn_front_kernel, tm=tm, n_heads=n_heads),
        grid=(b, s // tm),
        in_specs=[tok(d), const((1, d)), const((d, 3 * tw)), const((d, tw)), const((d, MEM_WIDTH)),
                  const((16, d)), const((CONV_K, 3 * tw)), const((16, 1)), const((16, 1)),
                  pl.BlockSpec((1, mlen, 2 * MEM_WIDTH), lambda i, j: (i, 0, 0))],
        out_specs=[tok(tw), tok(tw), tok(tw), tok(tw),
                   pl.BlockSpec((1, n_heads, 2, tm), lambda i, j: (i, 0, 0, j)),
                   tok(MEM_WIDTH)],
        out_shape=[jax.ShapeDtypeStruct((b, s, tw), BF16)] * 4
                  + [jax.ShapeDtypeStruct((b, n_heads, 2, s), F32),
                     jax.ShapeDtypeStruct((b, s, MEM_WIDTH), BF16)],
        scratch_shapes=[pltpu.VMEM((tm + 8, 3 * tw), F32)],
        compiler_params=_cparams(("parallel", "arbitrary")),
        name="dn_front",
    )(x, ng, wqkv, wz, wqm, wabt, convw, alog, dtb, kv)


def _tri_inverse(a_list, ii, jj):
    c = CHUNK
    eye = (ii == jj).astype(F32)
    blk = (ii >> 4) == (jj >> 4)
    n0 = [jnp.where(blk, -a, 0.0) for a in a_list]
    x = [eye + n for n in n0]
    nb = [n.astype(BF16) for n in n0]
    nb = [_dot(n, n).astype(BF16) for n in nb]
    for _ in range(2):
        r = [_dot(n, jnp.concatenate([n, xi.astype(BF16)], axis=1)) for n, xi in zip(nb, x)]
        nb = [ri[:, :c].astype(BF16) for ri in r]
        x = [xi + ri[:, c:] for xi, ri in zip(x, r)]
    x = [xi + _dot(n, xi.astype(BF16)) for n, xi in zip(nb, x)]
    shift = 4
    while (1 << shift) < c:
        inner = (ii >> shift) == (jj >> shift)
        outer = (ii >> (shift + 1)) == (jj >> (shift + 1))
        sel = outer & jnp.logical_not(inner)
        xb = [xi.astype(BF16) for xi in x]
        lx = [_dot(jnp.where(sel, a, 0.0).astype(BF16), xi).astype(BF16) for a, xi in zip(a_list, xb)]
        x = [xi - _dot(xbi, li) for xi, xbi, li in zip(x, xb, lx)]
        shift += 1
    return x


def _dn_intra_kernel(q_ref, k_ref, v_ref, gb_ref, u_ref, wp_ref, ae_ref, kts_ref, egl_ref, *, n_chunks):
    c = CHUNK
    ii = lax.broadcasted_iota(jnp.int32, (c, c), 0)
    jj = lax.broadcasted_iota(jnp.int32, (c, c), 1)
    tril = ii >= jj
    strict = ii > jj
    eye = ii == jj
    upper_incl = (ii <= jj).astype(BF16)
    chunks = range(n_chunks)
    rows = [slice(n * c, (n + 1) * c) for n in chunks]
    k = [k_ref[0, r, :] for r in rows]
    g_row = [gb_ref[0, 0, 0:1, r] for r in rows]
    b_row = [gb_ref[0, 0, 1:2, r] for r in rows]
    kq = [_dot_nt(jnp.concatenate([k[n], q_ref[0, rows[n], :]], axis=0), k[n]) for n in chunks]
    gc_row = []
    for n in chunks:
        hi, mid, lo = _split3(jnp.broadcast_to(g_row[n], (8, c)))
        gc8 = _dot(hi, upper_incl) + _dot(mid, upper_incl) + _dot(lo, upper_incl)
        gc_row.append(gc8[0:1, :])
    gl = [jnp.sum(g, axis=-1, keepdims=True) for g in g_row]
    a_list, b_rs, e_rs = [], [], []
    for n in chunks:
        g_r = jnp.broadcast_to(gc_row[n], (c, c))
        b_r = jnp.broadcast_to(b_row[n], (c, c))
        decay = jnp.where(tril, jnp.exp(g_r.T - g_r), 0.0)
        a_list.append(jnp.where(strict, kq[n][:c] * decay, 0.0) * b_r.T)
        ae_ref[0, 0, rows[n], 0:c] = jnp.where(tril, kq[n][c:] * decay, 0.0).astype(BF16)
        e_r = jnp.exp(g_r)
        ae_ref[0, 0, rows[n], c:2 * c] = jnp.where(eye, e_r, 0.0).astype(BF16)
        kts_ref[0, 0, :, rows[n]] = (k[n].astype(F32).T * jnp.exp(gl[n] - gc_row[n])).astype(BF16)
        egl_ref[0, 0, n] = jnp.broadcast_to(jnp.exp(gl[n]), (1, LANES))
        b_rs.append(b_r)
        e_rs.append(e_r)
    t = _tri_inverse(a_list, ii, jj)
    tb = [t[n] * b_rs[n] for n in chunks]
    u = [_dot(tb[n].astype(BF16), v_ref[0, rows[n], :]) for n in chunks]
    for n in chunks:
        u_ref[0, 0, rows[n], :] = u[n].astype(BF16)
        wp_ref[0, 0, rows[n], :] = (tb[n] * e_rs[n]).astype(BF16)


def _dn_intra(q, k, v, gb, *, tb):
    b, s, tw = q.shape
    n_heads = tw // HEAD_DIM
    n_chunks = tb // CHUNK
    qkv_spec = pl.BlockSpec((1, tb, HEAD_DIM), lambda i, h, j: (i, j, h))
    hs = lambda w: pl.BlockSpec((1, 1, tb, w), lambda i, h, j: (i, h, j, 0))
    return pl.pallas_call(
        functools.partial(_dn_intra_kernel, n_chunks=n_chunks),
        grid=(b, n_heads, s // tb),
        in_specs=[qkv_spec, qkv_spec, qkv_spec,
                  pl.BlockSpec((1, 1, 2, tb), lambda i, h, j: (i, h, 0, j))],
        out_specs=[hs(HEAD_DIM), hs(CHUNK), hs(2 * CHUNK),
                   pl.BlockSpec((1, 1, HEAD_DIM, tb), lambda i, h, j: (i, h, 0, j)),
                   pl.BlockSpec((1, 1, n_chunks, 1, LANES), lambda i, h, j: (i, h, j, 0, 0))],
        out_shape=[jax.ShapeDtypeStruct((b, n_heads, s, HEAD_DIM), BF16),
                   jax.ShapeDtypeStruct((b, n_heads, s, CHUNK), BF16),
                   jax.ShapeDtypeStruct((b, n_heads, s, 2 * CHUNK), BF16),
                   jax.ShapeDtypeStruct((b, n_heads, HEAD_DIM, s), BF16),
                   jax.ShapeDtypeStruct((b, n_heads, s // CHUNK, 1, LANES), F32)],
        compiler_params=_cparams(("parallel", "parallel", "parallel")),
        name="dn_intra",
    )(q, k, v, gb)


def _dn_scan_kernel(q_ref, k_ref, u_ref, wp_ref, ae_ref, kts_ref, egl_ref, o_ref, s_ref, *, n_chunks, n_heads):
    c = CHUNK

    @pl.when(pl.program_id(1) == 0)
    def _():
        s_ref[...] = jnp.zeros(s_ref.shape, F32)

    heads = range(n_heads)
    cols = [slice(hd * HEAD_DIM, (hd + 1) * HEAD_DIM) for hd in heads]
    states = [s_ref[hd] for hd in heads]
    for n in range(n_chunks):
        rows = slice(n * c, (n + 1) * c)
        kqs = [_dot(jnp.concatenate([k_ref[0, rows, cols[hd]], q_ref[0, rows, cols[hd]]], axis=0),
                    states[hd].astype(BF16)) for hd in heads]
        wks = [_dot(wp_ref[0, hd, rows, :], kqs[hd][:c].astype(BF16)) for hd in heads]
        xb = [(u_ref[0, hd, rows, :].astype(F32) - wks[hd]).astype(BF16) for hd in heads]
        upd = [_dot(kts_ref[0, hd, :, rows], xb[hd]) for hd in heads]
        states = [states[hd] * egl_ref[0, hd, n] + upd[hd] for hd in heads]
        out = [_dot(ae_ref[0, hd, rows, :], jnp.concatenate([xb[hd], kqs[hd][c:].astype(BF16)], axis=0))
               for hd in heads]
        for hd in heads:
            o_ref[0, rows, cols[hd]] = out[hd].astype(BF16)
    for hd in heads:
        s_ref[hd] = states[hd]


def _dn_scan(q, k, u, wp, ae, kts, egl, *, tb):
    b, s, tw = q.shape
    n_heads = tw // HEAD_DIM
    n_chunks = tb // CHUNK
    tok = pl.BlockSpec((1, tb, tw), lambda i, j: (i, j, 0))
    hs = lambda w: pl.BlockSpec((1, n_heads, tb, w), lambda i, j: (i, 0, j, 0))
    return pl.pallas_call(
        functools.partial(_dn_scan_kernel, n_chunks=n_chunks, n_heads=n_heads),
        grid=(b, s // tb),
        in_specs=[tok, tok, hs(HEAD_DIM), hs(CHUNK), hs(2 * CHUNK),
                  pl.BlockSpec((1, n_heads, HEAD_DIM, tb), lambda i, j: (i, 0, 0, j)),
                  pl.BlockSpec((1, n_heads, n_chunks, 1, LANES), lambda i, j: (i, 0, j, 0, 0))],
        out_specs=tok,
        out_shape=jax.ShapeDtypeStruct((b, s, tw), BF16),
        scratch_shapes=[pltpu.VMEM((n_heads, HEAD_DIM, HEAD_DIM), F32)],
        compiler_params=_cparams(("parallel", "arbitrary")),
        name="dn_scan",
    )(q, k, u, wp, ae, kts, egl)


def _mixer_tail(x, mix_b, cross_b, wo_mix_ref, wo_mem_ref, fg_ref, wr_ref, br_ref,
                x1_ref, rows_ref, code_ref, cnt_ref, carry_ref):
    tm, d = x.shape
    assert d == SUBLANES * LANES
    y = _dot(mix_b, wo_mix_ref[...]) + _dot(cross_b, wo_mem_ref[...])
    x1 = x + y
    x1_ref[...] = x1
    h2 = _rms(x1, fg_ref[...])
    logits = _dot(h2.astype(BF16), wr_ref[...]) + br_ref[...]
    cls = _router_class(logits)
    for c in range(SUBLANES):
        rows_ref[pl.ds(c, tm, stride=SUBLANES), :] = h2[:, c * LANES:(c + 1) * LANES]

    @pl.when(pl.program_id(0) == 0)
    def _():
        carry_ref[...] = jnp.zeros(carry_ref.shape, F32)

    lane = lax.broadcasted_iota(jnp.int32, (tm, LANES), 1)
    onehot = lane == cls
    ii = lax.broadcasted_iota(jnp.int32, (tm, tm), 0)
    jj = lax.broadcasted_iota(jnp.int32, (tm, tm), 1)
    earlier = (ii > jj).astype(BF16)
    before = _dot(earlier, onehot.astype(BF16)) + carry_ref[...]
    rank = jnp.sum(jnp.where(onehot, before, 0.0), axis=-1, keepdims=True).astype(jnp.int32)
    carry_ref[...] = carry_ref[...] + jnp.sum(onehot.astype(F32), axis=0, keepdims=True)
    cnt_ref[...] = carry_ref[...]
    cols = jnp.where(lane == 0, cls, jnp.where(lane == 1, rank >> 8, jnp.where(lane == 2, rank & 255, 0)))
    l8 = lax.broadcasted_iota(jnp.int32, (SUBLANES, LANES), 1)
    s8 = lax.broadcasted_iota(jnp.int32, (SUBLANES, LANES), 0)
    picked = _dot_nt((l8 == s8).astype(BF16), cols.astype(F32).astype(BF16)).astype(jnp.int32)
    code_ref[0] = (picked[0:1] << RANK_BITS) | (picked[1:2] << 8) | picked[2:3]


def _tail_out_specs(tm, d):
    return [pl.BlockSpec((tm, d), lambda i: (i, 0)),
            pl.BlockSpec((tm * SUBLANES, LANES), lambda i: (i, 0)),
            pl.BlockSpec((1, 1, tm), lambda i: (i, 0, 0)),
            pl.BlockSpec((1, LANES), lambda i: (0, 0))]


def _tail_out_shapes(t, d, tm):
    return [jax.ShapeDtypeStruct((t, d), F32), jax.ShapeDtypeStruct((t * SUBLANES, LANES), F32),
            jax.ShapeDtypeStruct((t // tm, 1, tm), jnp.int32), jax.ShapeDtypeStruct((1, LANES), F32)]


def _dn_out_kernel(x_ref, o_ref, zg_ref, cross_ref, og_ref, wo_mix_ref, wo_mem_ref, fg_ref, wr_ref, br_ref,
                   x1_ref, rows_ref, code_ref, cnt_ref, carry_ref, *, n_heads):
    og = og_ref[...]
    parts = []
    for hd in range(n_heads):
        cols = slice(hd * HEAD_DIM, (hd + 1) * HEAD_DIM)
        oh = o_ref[:, cols].astype(F32)
        on = oh * lax.rsqrt(jnp.mean(oh * oh, axis=-1, keepdims=True) + EPS) * og
        parts.append((on * zg_ref[:, cols].astype(F32)).astype(BF16))
    mix = jnp.concatenate(parts, axis=1)
    _mixer_tail(x_ref[...], mix, cross_ref[...], wo_mix_ref, wo_mem_ref, fg_ref, wr_ref, br_ref,
                x1_ref, rows_ref, code_ref, cnt_ref, carry_ref)


def _dn_out(x2d, o2d, zg2d, cross2d, og, wo_mix, wo_mem, fg, wr, br, *, tm):
    t, d = x2d.shape
    tw = o2d.shape[1]
    n_heads = tw // HEAD_DIM
    const = lambda shape: pl.BlockSpec(shape, lambda i: (0,) * len(shape))
    tok = lambda w: pl.BlockSpec((tm, w), lambda i: (i, 0))
    return pl.pallas_call(
        functools.partial(_dn_out_kernel, n_heads=n_heads),
        grid=(t // tm,),
        in_specs=[tok(d), tok(tw), tok(tw), tok(MEM_WIDTH), const((1, HEAD_DIM)), const((tw, d)),
                  const((MEM_WIDTH, d)), const((1, d)), const((d, LANES)), const((1, LANES))],
        out_specs=_tail_out_specs(tm, d),
        out_shape=_tail_out_shapes(t, d, tm),
        scratch_shapes=[pltpu.VMEM((1, LANES), F32)],
        compiler_params=_cparams(("arbitrary",)),
        name="dn_out",
    )(x2d, o2d, zg2d, cross2d, og, wo_mix, wo_mem, fg, wr, br)


def _row_tile(ref, r):
    return ref.at[pl.ds(pl.multiple_of(r * SUBLANES, SUBLANES), SUBLANES), :]


def _dispatch_kernel(code_ref, starts_ref, fill_ref, rows_ref, xs_hbm, pos_ref, zero_ref, sem, pad_sem, *, td):
    rank_mask = (1 << RANK_BITS) - 1

    @pl.when(pl.program_id(0) == 0)
    def _():
        zero_ref[...] = jnp.zeros(zero_ref.shape, F32)
        n_fill = fill_ref.shape[0] // 2
        for c in range(n_fill):
            first, n_pad = fill_ref[c], fill_ref[n_fill + c]

            def fill(kk, carry, first=first):
                pltpu.make_async_copy(zero_ref, _row_tile(xs_hbm, first + kk), pad_sem).start()
                return carry

            def drain(kk, carry, first=first):
                pltpu.make_async_copy(zero_ref, _row_tile(xs_hbm, first + kk), pad_sem).wait()
                return carry

            lax.fori_loop(0, n_pad, fill, 0)
            lax.fori_loop(0, n_pad, drain, 0)

    def issue(pair, carry):
        for prio in range(DMA_PRIORITIES):
            r = pair * DMA_PRIORITIES + prio
            code = code_ref[r]
            slot = starts_ref[code >> RANK_BITS] + (code & rank_mask)
            pos_ref[r] = slot
            pltpu.make_async_copy(_row_tile(rows_ref, r), _row_tile(xs_hbm, slot), sem).start(priority=prio)
        return carry

    lax.fori_loop(0, td // DMA_PRIORITIES, issue, 0, unroll=4)
    pltpu.make_async_copy(rows_ref, xs_hbm.at[pl.ds(0, td * SUBLANES), :], sem).wait()


def _dispatch(code, starts, fill, rows, n_slots, *, td):
    t = code.shape[0]
    smem_whole = lambda n: pl.BlockSpec((n,), lambda i: (0,), memory_space=pltpu.SMEM)
    return pl.pallas_call(
        functools.partial(_dispatch_kernel, td=td),
        grid=(t // td,),
        in_specs=[pl.BlockSpec((td,), lambda i: (i,), memory_space=pltpu.SMEM),
                  smem_whole(starts.shape[0]), smem_whole(fill.shape[0]),
                  pl.BlockSpec((td * SUBLANES, LANES), lambda i: (i, 0))],
        out_specs=[pl.BlockSpec(memory_space=pl.ANY),
                   pl.BlockSpec((td,), lambda i: (i,), memory_space=pltpu.SMEM)],
        out_shape=[jax.ShapeDtypeStruct((n_slots * SUBLANES, LANES), F32),
                   jax.ShapeDtypeStruct((t,), jnp.int32)],
        scratch_shapes=[pltpu.VMEM((SUBLANES, LANES), F32), pltpu.SemaphoreType.DMA(()),
                        pltpu.SemaphoreType.DMA(())],
        compiler_params=_cparams(("arbitrary",)),
        name="moe_dispatch",
    )(code, starts, fill, rows)


def _experts_kernel(ea_ref, eb_ref, used_ref, xs_ref, wr_ref, br_ref, wgu_a_ref, wgu_b_ref, wd_a_ref, wd_b_ref,
                    ys_ref, *, tile, d, d_expert):
    i = pl.program_id(0)

    @pl.when(used_ref[i] == 1)
    def _():
        h = jnp.concatenate([xs_ref[pl.ds(c, tile, stride=SUBLANES), :] for c in range(d // LANES)],
                            axis=1).astype(BF16)
        gates = _pair_gates(_dot(h, wr_ref[...]) + br_ref[...], ea_ref[i], eb_ref[i])
        parts = []
        for wgu_ref, wt in ((wgu_a_ref, gates[0]), (wgu_b_ref, gates[1])):
            gu = _dot(h, wgu_ref[0])
            gpart, upart = gu[:, :d_expert], gu[:, d_expert:]
            parts.append((gpart * _sigmoid(gpart) * upart * wt).astype(BF16))
        y = _dot(parts[0], wd_a_ref[0]) + _dot(parts[1], wd_b_ref[0])
        for c in range(d // LANES):
            ys_ref[pl.ds(c, tile, stride=SUBLANES), :] = y[:, c * LANES:(c + 1) * LANES]

    @pl.when(used_ref[i] == 0)
    def _():
        ys_ref[...] = jnp.zeros(ys_ref.shape, F32)


def _experts(tile_lo, tile_hi, tile_used, xs, wr, br, wgu, wd, *, d):
    n_tiles = tile_lo.shape[0]
    de2 = wgu.shape[2]
    blk = MOE_TILE * SUBLANES
    grid_spec = pltpu.PrefetchScalarGridSpec(
        num_scalar_prefetch=3,
        grid=(n_tiles,),
        in_specs=[pl.BlockSpec((blk, LANES), lambda i, ea, eb, us: (i * us[i], 0)),
                  pl.BlockSpec((d, LANES), lambda i, ea, eb, us: (0, 0)),
                  pl.BlockSpec((1, LANES), lambda i, ea, eb, us: (0, 0)),
                  pl.BlockSpec((1, d, de2), lambda i, ea, eb, us: (ea[i], 0, 0)),
                  pl.BlockSpec((1, d, de2), lambda i, ea, eb, us: (eb[i], 0, 0)),
                  pl.BlockSpec((1, de2 // 2, d), lambda i, ea, eb, us: (ea[i], 0, 0)),
                  pl.BlockSpec((1, de2 // 2, d), lambda i, ea, eb, us: (eb[i], 0, 0))],
        out_specs=pl.BlockSpec((blk, LANES), lambda i, ea, eb, us: (i, 0)))
    return pl.pallas_call(
        functools.partial(_experts_kernel, tile=MOE_TILE, d=d, d_expert=de2 // 2),
        grid_spec=grid_spec,
        out_shape=jax.ShapeDtypeStruct((n_tiles * blk, LANES), F32),
        compiler_params=_cparams(("arbitrary",)),
        name="moe_experts",
    )(tile_lo, tile_hi, tile_used, xs, wr, br, wgu, wgu, wd, wd)


def _combine_kernel(pos_ref, ys_hbm, x_ref, fin_ref, out_ref, buf_ref, sem, *, tc, d, final_norm):
    def issue(pair, carry):
        for prio in range(DMA_PRIORITIES):
            r = pair * DMA_PRIORITIES + prio
            pltpu.make_async_copy(_row_tile(ys_hbm, pos_ref[r]), _row_tile(buf_ref, r), sem).start(priority=prio)
        return carry

    lax.fori_loop(0, tc // DMA_PRIORITIES, issue, 0, unroll=4)
    pltpu.make_async_copy(ys_hbm.at[pl.ds(0, tc * SUBLANES), :], buf_ref, sem).wait()
    y = jnp.concatenate([buf_ref[pl.ds(c, tc, stride=SUBLANES), :] for c in range(d // LANES)], axis=1)
    out = x_ref[...] + y
    if final_norm:
        out = _rms(out, fin_ref[...])
    out_ref[...] = out


def _combine(pos, ys, x1, fin_g, *, tc, final_norm):
    t, d = x1.shape
    return pl.pallas_call(
        functools.partial(_combine_kernel, tc=tc, d=d, final_norm=final_norm),
        grid=(t // tc,),
        in_specs=[pl.BlockSpec((tc,), lambda i: (i,), memory_space=pltpu.SMEM),
                  pl.BlockSpec(memory_space=pl.ANY),
                  pl.BlockSpec((tc, d), lambda i: (i, 0)),
                  pl.BlockSpec((1, d), lambda i: (0, 0))],
        out_specs=pl.BlockSpec((tc, d), lambda i: (i, 0)),
        out_shape=jax.ShapeDtypeStruct((t, d), F32),
        scratch_shapes=[pltpu.VMEM((tc * SUBLANES, LANES), F32), pltpu.SemaphoreType.DMA(())],
        compiler_params=_cparams(("arbitrary",)),
        name="moe_combine",
    )(pos, ys, x1, fin_g)


def _moe_sparse(x1, rows, code, counts, wr, br, wgu, wd, fin_g, *, final_norm):
    t, d = x1.shape
    n_tiles = t // MOE_TILE + N_CLASSES
    cnt = counts[0, :N_CLASSES].astype(jnp.int32)
    padded = (cnt + MOE_TILE - 1) // MOE_TILE * MOE_TILE
    ends = jnp.cumsum(padded)
    starts = ends - padded
    n_slots = n_tiles * MOE_TILE
    fill = jnp.concatenate([starts + cnt, ends[-1:], padded - cnt, n_slots - ends[-1:]])
    tile_start = jnp.arange(n_tiles, dtype=jnp.int32) * MOE_TILE
    tile_cls = jnp.minimum(jnp.sum((tile_start[:, None] >= ends[None, :]).astype(jnp.int32), axis=1), N_CLASSES - 1)
    lo_ids, hi_ids = _class_experts()
    onehot = (tile_cls[:, None] == jnp.arange(N_CLASSES, dtype=jnp.int32)[None, :]).astype(jnp.int32)
    tile_lo = jnp.sum(onehot * jnp.asarray(lo_ids, jnp.int32)[None, :], axis=1)
    tile_hi = jnp.sum(onehot * jnp.asarray(hi_ids, jnp.int32)[None, :], axis=1)
    tile_used = (tile_start < ends[-1]).astype(jnp.int32)
    xs, pos = _dispatch(code.reshape(t), starts, fill, rows, n_slots, td=1024)
    ys = _experts(tile_lo, tile_hi, tile_used, xs, wr, br, wgu, wd, d=d)
    return _combine(pos, ys, x1, fin_g, tc=512, final_norm=final_norm)


def _gm_layer_kernel(x_ref, ng_ref, win_ref, wqm_ref, lng_ref, lnb_ref, wsp_ref, bsp_ref, kv_ref,
                     wo_mix_ref, wo_mem_ref, fg_ref, wr_ref, br_ref, x1_ref, rows_ref, code_ref, cnt_ref, carry_ref,
                     *, tm, n_groups):
    c = CHUNK
    tw = n_groups * HEAD_DIM
    x = x_ref[...]
    h = _rms(x, ng_ref[...]).astype(BF16)
    proj = _dot(h, win_ref[...])
    uv = 0.5 * proj * (1.0 + lax.erf(proj * (2.0 ** -0.5)))
    u, v = uv[:, :tw], uv[:, tw:]
    mu = jnp.mean(v, axis=-1, keepdims=True)
    vc = v - mu
    var = jnp.mean(vc * vc, axis=-1, keepdims=True)
    vn = (vc * lax.rsqrt(var + EPS) * lng_ref[...] + lnb_ref[...]).astype(BF16)
    ii = lax.broadcasted_iota(jnp.int32, (c, c), 0)
    jj = lax.broadcasted_iota(jnp.int32, (c, c), 1)
    tril = ii >= jj
    n_chunks = tm // c
    col_parts = []
    for g in range(n_groups):
        wc = jnp.where(tril, wsp_ref[g], jnp.zeros((c, c), BF16))
        cols = slice(g * HEAD_DIM, (g + 1) * HEAD_DIM)
        wide = _dot(wc, jnp.concatenate([vn[n * c:(n + 1) * c, cols] for n in range(n_chunks)], axis=1))
        bias = bsp_ref[g]
        col_parts.append(jnp.concatenate(
            [wide[:, n * HEAD_DIM:(n + 1) * HEAD_DIM] + bias for n in range(n_chunks)], axis=0))
    sgate = jnp.concatenate(col_parts, axis=1)
    mix = (u * sgate).astype(BF16)
    qm = _dot(h, wqm_ref[...])
    kv = kv_ref[0]
    cross = _cross_attn(qm, kv[:, :MEM_WIDTH], kv[:, MEM_WIDTH:]).astype(BF16)
    _mixer_tail(x, mix, cross, wo_mix_ref, wo_mem_ref, fg_ref, wr_ref, br_ref,
                x1_ref, rows_ref, code_ref, cnt_ref, carry_ref)


def _gm_layer(x2d, ng, win, wqm, lng, lnb, wsp, bsp, kv, kv_layer, wo_mix, wo_mem, fg, wr, br, *, tm, seq):
    t, d = x2d.shape
    tw = lng.shape[1]
    n_groups = tw // HEAD_DIM
    mlen = kv.shape[1]
    tiles_per_seq = seq // tm
    const = lambda shape: pl.BlockSpec(shape, lambda i: (0,) * len(shape))
    tok = lambda w: pl.BlockSpec((tm, w), lambda i: (i, 0))
    return pl.pallas_call(
        functools.partial(_gm_layer_kernel, tm=tm, n_groups=n_groups),
        grid=(t // tm,),
        in_specs=[tok(d), const((1, d)), const((d, 2 * tw)), const((d, MEM_WIDTH)), const((1, tw)), const((1, tw)),
                  const((n_groups, CHUNK, CHUNK)), const((n_groups, CHUNK, HEAD_DIM)),
                  pl.BlockSpec((1, mlen, 2 * MEM_WIDTH), lambda i: (i // tiles_per_seq, 0, kv_layer)),
                  const((tw, d)), const((MEM_WIDTH, d)), const((1, d)), const((d, LANES)), const((1, LANES))],
        out_specs=_tail_out_specs(tm, d),
        out_shape=_tail_out_shapes(t, d, tm),
        scratch_shapes=[pltpu.VMEM((1, LANES), F32)],
        compiler_params=_cparams(("arbitrary",)),
        name="gm_layer",
    )(x2d, ng, win, wqm, lng, lnb, wsp, bsp, kv, wo_mix, wo_mem, fg, wr, br)


def _router_params(w_group, b_group, w_router, b_router):
    d = w_group.shape[0]
    pad = LANES - N_EXPERTS - MOE_GROUPS
    wr = jnp.concatenate([w_router, w_group, jnp.zeros((d, pad), F32)], axis=1).astype(BF16)
    br = jnp.concatenate([b_router, b_group, jnp.zeros((pad,), F32)])[None, :]
    return wr, br


def _col16(vec):
    return jnp.concatenate([vec, jnp.zeros((16 - vec.shape[0],), F32)])[:, None]


def kernel(x, mem, mem_norm_g, mix_norm_g, w_out, w_mem_kv, dn_w_in, dn_conv_w, dn_a_log, dn_dt_bias, dn_o_norm_g,
           gm_w_in, gm_ln_g, gm_ln_b, gm_w_spatial, gm_b_spatial, ffn_norm_g, moe_w_group, moe_b_group,
           moe_w_router, moe_b_router, moe_w_gate, moe_w_up, moe_w_down, final_norm_g):
    b, s, d = x.shape
    tw = d - MEM_WIDTH
    n_heads = tw // HEAD_DIM
    t = b * s

    kv = _memkv(mem, mem_norm_g[None, :], jnp.concatenate([w_mem_kv[0], w_mem_kv[1]], axis=1).astype(BF16))

    w_in = dn_w_in[0]
    o1, o2, o3, o4 = 3 * tw, 4 * tw, 4 * tw + n_heads, 4 * tw + 2 * n_heads
    wabt = jnp.concatenate([w_in[:, o2:o4].T, jnp.zeros((16 - 2 * n_heads, d), F32)], axis=0).astype(BF16)
    q, k, v, zg, gb, cross = _dn_front(
        x, mix_norm_g[0][None, :], w_in[:, :o1].astype(BF16), w_in[:, o1:o2].astype(BF16),
        w_in[:, o4:].astype(BF16), wabt, dn_conv_w[0], _col16(dn_a_log[0]), _col16(dn_dt_bias[0]), kv, tm=512)
    u, wp, ae, kts, egl = _dn_intra(q, k, v, gb, tb=1024)
    o = _dn_scan(q, k, u, wp, ae, kts, egl, tb=512)
    wr0, br0 = _router_params(moe_w_group[0], moe_b_group[0], moe_w_router[0], moe_b_router[0])
    wo0 = w_out[0].astype(BF16)
    x1, rows0, code0, cnt0 = _dn_out(
        x.reshape(t, d), o.reshape(t, tw), zg.reshape(t, tw), cross.reshape(t, MEM_WIDTH),
        dn_o_norm_g[0][None, :], wo0[:tw], wo0[tw:], ffn_norm_g[0][None, :], wr0, br0, tm=512)
    wgu0 = jnp.concatenate([moe_w_gate[0], moe_w_up[0]], axis=2).astype(BF16)
    x2 = _moe_sparse(x1, rows0, code0, cnt0, wr0, br0, wgu0, moe_w_down[0].astype(BF16), final_norm_g[None, :],
                     final_norm=False)

    win1 = gm_w_in[0]
    wr1, br1 = _router_params(moe_w_group[1], moe_b_group[1], moe_w_router[1], moe_b_router[1])
    wo1 = w_out[1].astype(BF16)
    bsp = jnp.broadcast_to(gm_b_spatial[0][:, :, None], (n_heads, CHUNK, HEAD_DIM))
    x3, rows1, code1, cnt1 = _gm_layer(
        x2, mix_norm_g[1][None, :], win1[:, :2 * tw].astype(BF16), win1[:, 2 * tw:].astype(BF16),
        gm_ln_g[0][None, :], gm_ln_b[0][None, :], gm_w_spatial[0].astype(BF16), bsp, kv, 1,
        wo1[:tw], wo1[tw:], ffn_norm_g[1][None, :], wr1, br1, tm=512, seq=s)
    wgu1 = jnp.concatenate([moe_w_gate[1], moe_w_up[1]], axis=2).astype(BF16)
    out = _moe_sparse(x3, rows1, code1, cnt1, wr1, br1, wgu1, moe_w_down[1].astype(BF16), final_norm_g[None, :],
                      final_norm=True)
    return out.reshape(b, s, d)
```

```python
import functools

import jax
import jax.numpy as jnp
from jax import lax
from jax.experimental import pallas as pl
from jax.experimental.pallas import tpu as pltpu

F32 = jnp.float32
BF16 = jnp.bfloat16
EPS = 1e-6

MEM_HEADS = 4
MEM_HEAD_DIM = 64
MEM_WIDTH = MEM_HEADS * MEM_HEAD_DIM
HEAD_DIM = 128
CONV_K = 4
CHUNK = 128
MOE_GROUPS = 4
EXPERTS_PER_GROUP = 4
N_EXPERTS = MOE_GROUPS * EXPERTS_PER_GROUP
PAIRS_PER_GROUP = EXPERTS_PER_GROUP * (EXPERTS_PER_GROUP - 1) // 2
N_CLASSES = MOE_GROUPS * PAIRS_PER_GROUP
MOE_TILE = 256
LANES = 128
SUBLANES = 8
RANK_BITS = 16
DMA_PRIORITIES = 2
VMEM_LIMIT = 56 * 1024 * 1024

NT_DIMS = (((1,), (1,)), ((), ()))


def _dot(a, b):
    return jnp.dot(a, b, preferred_element_type=F32)


def _dot_nt(a, b):
    return lax.dot_general(a, b, NT_DIMS, preferred_element_type=F32)


def _rms(x, g):
    return x * lax.rsqrt(jnp.mean(x * x, axis=-1, keepdims=True) + EPS) * g


def _sigmoid(x):
    return 1.0 / (1.0 + jnp.exp(-x))


def _softplus(x):
    return jnp.maximum(x, 0.0) + jnp.log1p(jnp.exp(-jnp.abs(x)))


def _split3(x):
    hi = x.astype(BF16)
    r = x - hi.astype(F32)
    mid = r.astype(BF16)
    lo = (r - mid.astype(F32)).astype(BF16)
    return hi, mid, lo


def _cparams(sem):
    return pltpu.CompilerParams(dimension_semantics=sem, vmem_limit_bytes=VMEM_LIMIT)


def _memkv_kernel(mem_ref, g_ref, w_ref, kv_ref):
    mn = _rms(mem_ref[0], g_ref[...]).astype(BF16)
    kv_ref[0] = _dot(mn, w_ref[...]).astype(BF16)


def _memkv(mem, g, w_all):
    b, m, d = mem.shape
    n = w_all.shape[1]
    return pl.pallas_call(
        _memkv_kernel,
        grid=(b,),
        in_specs=[pl.BlockSpec((1, m, d), lambda i: (i, 0, 0)),
                  pl.BlockSpec((1, d), lambda i: (0, 0)),
                  pl.BlockSpec((d, n), lambda i: (0, 0))],
        out_specs=pl.BlockSpec((1, m, n), lambda i: (i, 0, 0)),
        out_shape=jax.ShapeDtypeStruct((b, m, n), BF16),
        compiler_params=_cparams(("parallel",)),
        name="memkv",
    )(mem, g, w_all)


def _cross_attn(qm, kmem, vmem):
    lane = lax.broadcasted_iota(jnp.int32, (1, MEM_WIDTH), 1)
    out = jnp.zeros(qm.shape, F32)
    for h in range(MEM_HEADS):
        msk = (lane >= h * MEM_HEAD_DIM) & (lane < (h + 1) * MEM_HEAD_DIM)
        qh = jnp.where(msk, qm, 0.0).astype(BF16)
        s = _dot_nt(qh, kmem) * (MEM_HEAD_DIM ** -0.5)
        p = jnp.exp(s - jnp.max(s, axis=-1, keepdims=True))
        inv = 1.0 / jnp.sum(p, axis=-1, keepdims=True)
        vh = jnp.where(msk, vmem, jnp.zeros_like(vmem))
        out = out + _dot(p.astype(BF16), vh) * inv
    return out


def _router_class(logits):
    lane = lax.broadcasted_iota(jnp.int32, logits.shape, 1)
    neg = jnp.float32(-jnp.inf)
    big = jnp.int32(1 << 20)
    is_g = (lane >= N_EXPERTS) & (lane < N_EXPERTS + MOE_GROUPS)
    gl = jnp.where(is_g, logits, neg)
    gmax = jnp.max(gl, axis=-1, keepdims=True)
    g_idx = jnp.min(jnp.where(gl == gmax, lane, big), axis=-1, keepdims=True) - N_EXPERTS
    lo = g_idx * EXPERTS_PER_GROUP
    in_grp = (lane >= lo) & (lane < lo + EXPERTS_PER_GROUP)
    el = jnp.where(in_grp, logits, neg)
    m1 = jnp.max(el, axis=-1, keepdims=True)
    i1 = jnp.min(jnp.where(el == m1, lane, big), axis=-1, keepdims=True)
    el2 = jnp.where(lane == i1, neg, el)
    m2 = jnp.max(el2, axis=-1, keepdims=True)
    i2 = jnp.min(jnp.where(el2 == m2, lane, big), axis=-1, keepdims=True)
    e_lo = jnp.minimum(i1, i2) - lo
    e_hi = jnp.maximum(i1, i2) - lo
    pair_off = jnp.where(e_lo == 0, 0, jnp.where(e_lo == 1, 3, 5))
    return g_idx * PAIRS_PER_GROUP + pair_off + e_hi - e_lo - 1


def _pair_gates(logits, e_lo, e_hi):
    lane = lax.broadcasted_iota(jnp.int32, logits.shape, 1)
    pick = lambda idx: jnp.sum(jnp.where(lane == idx, logits, 0.0), axis=-1, keepdims=True)
    l_lo, l_hi = pick(e_lo), pick(e_hi)
    g_sel = pick(N_EXPERTS + e_lo // EXPERTS_PER_GROUP)
    is_g = (lane >= N_EXPERTS) & (lane < N_EXPERTS + MOE_GROUPS)
    p_grp = 1.0 / jnp.sum(jnp.where(is_g, jnp.exp(logits - g_sel), 0.0), axis=-1, keepdims=True)
    return p_grp / (1.0 + jnp.exp(l_hi - l_lo)), p_grp / (1.0 + jnp.exp(l_lo - l_hi))


def _class_experts():
    lo_ids, hi_ids = [], []
    for g in range(MOE_GROUPS):
        for a in range(EXPERTS_PER_GROUP):
            for b in range(a + 1, EXPERTS_PER_GROUP):
                lo_ids.append(g * EXPERTS_PER_GROUP + a)
                hi_ids.append(g * EXPERTS_PER_GROUP + b)
    return lo_ids, hi_ids


def _dn_front_kernel(x_ref, ng_ref, wqkv_ref, wz_ref, wqm_ref, wabt_ref, convw_ref, alog_ref, dtb_ref, kv_ref,
                     q_ref, k_ref, v_ref, zg_ref, gb_ref, cross_ref, cbuf_ref, *, tm, n_heads):
    j = pl.program_id(1)
    tw = n_heads * HEAD_DIM
    h = _rms(x_ref[0], ng_ref[...]).astype(BF16)

    @pl.when(j == 0)
    def _():
        cbuf_ref[0:8, :] = jnp.zeros((8, 3 * tw), F32)

    gw = 2 * HEAD_DIM
    out_refs = (q_ref, k_ref, v_ref)
    scales = (HEAD_DIM ** -0.5, 1.0, None)
    for grp in range(3 * tw // gw):
        cols = slice(grp * gw, (grp + 1) * gw)
        pre = _dot(h, wqkv_ref[:, cols])
        cbuf_ref[8:8 + tm, cols] = pre
        w = convw_ref[:, cols]
        acc = w[CONV_K - 1:CONV_K, :] * pre
        for kk in range(CONV_K - 1):
            acc = acc + w[kk:kk + 1, :] * cbuf_ref[pl.ds(8 - (CONV_K - 1) + kk, tm), cols]
        cbuf_ref[0:8, cols] = cbuf_ref[tm:tm + 8, cols]
        act = acc * _sigmoid(acc)
        which, first = divmod(grp * gw, tw)
        for half in range(gw // HEAD_DIM):
            a = act[:, half * HEAD_DIM:(half + 1) * HEAD_DIM]
            if scales[which] is not None:
                a = a * (lax.rsqrt(jnp.sum(a * a, axis=-1, keepdims=True) + EPS) * scales[which])
            dst = first + half * HEAD_DIM
            out_refs[which][0, :, dst:dst + HEAD_DIM] = a.astype(BF16)

    z = _dot(h, wz_ref[...])
    zg_ref[0] = (z * _sigmoid(z)).astype(BF16)

    abt = _dot_nt(wabt_ref[...], h)
    g_t = -jnp.exp(alog_ref[...]) * _softplus(abt + dtb_ref[...])
    b_t = _sigmoid(abt)
    for hd in range(n_heads):
        gb_ref[0, hd, 0:1, :] = g_t[hd:hd + 1, :]
        gb_ref[0, hd, 1:2, :] = b_t[n_heads + hd:n_heads + hd + 1, :]

    qm = _dot(h, wqm_ref[...])
    kv = kv_ref[0]
    cross_ref[0] = _cross_attn(qm, kv[:, :MEM_WIDTH], kv[:, MEM_WIDTH:]).astype(BF16)


def _dn_front(x, ng, wqkv, wz, wqm, wabt, convw, alog, dtb, kv, *, tm):
    b, s, d = x.shape
    tw = wz.shape[1]
    n_heads = tw // HEAD_DIM
    mlen = kv.shape[1]
    const = lambda shape: pl.BlockSpec(shape, lambda i, j: (0,) * len(shape))
    tok = lambda w: pl.BlockSpec((1, tm, w), lambda i, j: (i, j, 0))
    return pl.pallas_call(
        functools.partial(_dn_front_kernel, tm=tm, n_heads=n_heads),
        grid=(b, s // tm),
        in_specs=[tok(d), const((1, d)), const((d, 3 * tw)), const((d, tw)), const((d, MEM_WIDTH)),
                  const((16, d)), const((CONV_K, 3 * tw)), const((16, 1)), const((16, 1)),
                  pl.BlockSpec((1, mlen, 2 * MEM_WIDTH), lambda i, j: (i, 0, 0))],
        out_specs=[tok(tw), tok(tw), tok(tw), tok(tw),
                   pl.BlockSpec((1, n_heads, 2, tm), lambda i, j: (i, 0, 0, j)),
                   tok(MEM_WIDTH)],
        out_shape=[jax.ShapeDtypeStruct((b, s, tw), BF16)] * 4
                  + [jax.ShapeDtypeStruct((b, n_heads, 2, s), F32),
                     jax.ShapeDtypeStruct((b, s, MEM_WIDTH), BF16)],
        scratch_shapes=[pltpu.VMEM((tm + 8, 3 * tw), F32)],
        compiler_params=_cparams(("parallel", "arbitrary")),
        name="dn_front",
    )(x, ng, wqkv, wz, wqm, wabt, convw, alog, dtb, kv)


def _cat_lanes(a, b):
    return jnp.concatenate([a, b], axis=1)


def _block_diag(a, b):
    z = jnp.zeros(a.shape, a.dtype)
    return jnp.concatenate([_cat_lanes(a, z), _cat_lanes(z, b)], axis=0)


def _pair_dot(a0, a1, b0, b1):
    w = a0.shape[1]
    r = _dot(_cat_lanes(a0, a1), _block_diag(b0, b1))
    return r[:, :w], r[:, w:]


def _tri_inverse_pairs(a_pairs, ii, jj):
    c = CHUNK
    eye = (ii == jj).astype(F32)
    blk = (ii >> 4) == (jj >> 4)
    n0 = [[jnp.where(blk, -a, 0.0) for a in pair] for pair in a_pairs]
    x = [[eye + n for n in pair] for pair in n0]
    nb = [[n.astype(BF16) for n in pair] for pair in n0]
    nb = [[r.astype(BF16) for r in _pair_dot(p[0], p[1], p[0], p[1])] for p in nb]
    for _ in range(2):
        r = [[_dot(n, _cat_lanes(n, xi.astype(BF16))) for n, xi in zip(pn, px)] for pn, px in zip(nb, x)]
        nb = [[ri[:, :c].astype(BF16) for ri in pr] for pr in r]
        x = [[xi + ri[:, c:] for xi, ri in zip(px, pr)] for px, pr in zip(x, r)]
    fin = [_pair_dot(pn[0], pn[1], px[0].astype(BF16), px[1].astype(BF16)) for pn, px in zip(nb, x)]
    x = [[xi + fi for xi, fi in zip(px, pf)] for px, pf in zip(x, fin)]
    shift = 4
    while (1 << shift) < c:
        inner = (ii >> shift) == (jj >> shift)
        outer = (ii >> (shift + 1)) == (jj >> (shift + 1))
        sel = outer & jnp.logical_not(inner)
        xb = [[xi.astype(BF16) for xi in px] for px in x]
        lo = [[jnp.where(sel, a, 0.0).astype(BF16) for a in pa] for pa in a_pairs]
        lx = [[r.astype(BF16) for r in _pair_dot(pl_[0], pl_[1], pb[0], pb[1])] for pl_, pb in zip(lo, xb)]
        cor = [_pair_dot(pb[0], pb[1], pl_[0], pl_[1]) for pb, pl_ in zip(xb, lx)]
        x = [[xi - ci for xi, ci in zip(px, pc)] for px, pc in zip(x, cor)]
        shift += 1
    return x


def _dn_intra_kernel(q_ref, k_ref, v_ref, gb_ref, u_ref, wp_ref, at_ref, ec_ref, kts_ref, egl_ref, *, n_chunks):
    c = CHUNK
    hw = HEAD_DIM
    ii = lax.broadcasted_iota(jnp.int32, (c, c), 0)
    jj = lax.broadcasted_iota(jnp.int32, (c, c), 1)
    tril = ii >= jj
    strict = ii > jj
    lane2 = lax.broadcasted_iota(jnp.int32, (c, 2 * hw), 1)
    upper_incl = (ii <= jj).astype(BF16)
    chunks = range(n_chunks)
    heads = range(2)
    rows = [slice(n * c, (n + 1) * c) for n in chunks]
    hcol = [slice(hd * hw, (hd + 1) * hw) for hd in heads]
    kp = [k_ref[0, r, :] for r in rows]
    zero = jnp.zeros((c, 2 * hw), BF16)
    kq = [_dot_nt(jnp.concatenate([kp[n], q_ref[0, rows[n], :]], axis=0),
                  jnp.concatenate([jnp.where(lane2 < hw, kp[n], zero), jnp.where(lane2 >= hw, kp[n], zero)], axis=0))
          for n in chunks]
    g_rows = [gb_ref[0, hd, 0:1, rows[n]] for n in chunks for hd in heads]
    hi, mid, lo = _split3(jnp.concatenate(g_rows, axis=0))
    n_rows = len(g_rows)
    cum = _dot(jnp.concatenate([hi, mid, lo], axis=0), upper_incl)
    gc_all = cum[:n_rows] + cum[n_rows:2 * n_rows] + cum[2 * n_rows:]
    a_pairs, b_rs, e_rs = [], [], []
    for n in chunks:
        a_pair, b_pair, e_pair = [], [], []
        for hd in heads:
            g_row = g_rows[n * 2 + hd]
            b_row = gb_ref[0, hd, 1:2, rows[n]]
            gc_row = gc_all[n * 2 + hd:n * 2 + hd + 1, :]
            gl = jnp.sum(g_row, axis=-1, keepdims=True)
            g_r = jnp.broadcast_to(gc_row, (c, c))
            g_c = g_r.T
            b_r = jnp.broadcast_to(b_row, (c, c))
            decay = jnp.where(tril, jnp.exp(g_c - g_r), 0.0)
            kqh = kq[n][:, hd * c:(hd + 1) * c]
            a_pair.append(jnp.where(strict, kqh[:c] * decay, 0.0) * b_r.T)
            at_ref[0, rows[n], hcol[hd]] = jnp.where(tril, kqh[c:] * decay, 0.0).astype(BF16)
            ec_ref[0, rows[n], hcol[hd]] = jnp.exp(g_c).astype(BF16)
            kh = kp[n][:, hcol[hd]].astype(F32)
            kts_ref[0, hcol[hd], rows[n]] = (kh.T * jnp.exp(gl - gc_row)).astype(BF16)
            egl_ref[0, n, :, hcol[hd]] = jnp.broadcast_to(jnp.exp(gl), (1, hw))
            b_pair.append(b_r)
            e_pair.append(jnp.exp(g_r))
        a_pairs.append(a_pair)
        b_rs.append(b_pair)
        e_rs.append(e_pair)
    t = _tri_inverse_pairs(a_pairs, ii, jj)
    tb = [[t[n][hd] * b_rs[n][hd] for hd in heads] for n in chunks]
    u = [_pair_dot(tb[n][0].astype(BF16), tb[n][1].astype(BF16),
                   v_ref[0, rows[n], hcol[0]], v_ref[0, rows[n], hcol[1]]) for n in chunks]
    for n in chunks:
        for hd in heads:
            u_ref[0, rows[n], hcol[hd]] = u[n][hd].astype(BF16)
            wp_ref[0, rows[n], hcol[hd]] = (tb[n][hd] * e_rs[n][hd]).astype(BF16)


def _dn_intra(q, k, v, gb, *, tb):
    b, s, tw = q.shape
    n_chunks = tb // CHUNK
    pw = 2 * HEAD_DIM
    tok = pl.BlockSpec((1, tb, pw), lambda i, h, j: (i, j, h))
    return pl.pallas_call(
        functools.partial(_dn_intra_kernel, n_chunks=n_chunks),
        grid=(b, tw // pw, s // tb),
        in_specs=[tok, tok, tok, pl.BlockSpec((1, 2, 2, tb), lambda i, h, j: (i, h, 0, j))],
        out_specs=[tok, tok, tok, tok,
                   pl.BlockSpec((1, pw, tb), lambda i, h, j: (i, h, j)),
                   pl.BlockSpec((1, n_chunks, 1, pw), lambda i, h, j: (i, j, 0, h))],
        out_shape=[jax.ShapeDtypeStruct((b, s, tw), BF16)] * 4
                  + [jax.ShapeDtypeStruct((b, tw, s), BF16),
                     jax.ShapeDtypeStruct((b, s // CHUNK, 1, tw), F32)],
        compiler_params=_cparams(("parallel", "parallel", "parallel")),
        name="dn_intra",
    )(q, k, v, gb)


def _dn_scan_kernel(q_ref, k_ref, u_ref, wp_ref, at_ref, ec_ref, kts_ref, egl_ref, o_ref, s_ref,
                    *, n_chunks, n_pairs, n_batch):
    c = CHUNK
    pw = 2 * HEAD_DIM

    @pl.when(pl.program_id(1) == 0)
    def _():
        s_ref[...] = jnp.zeros(s_ref.shape, F32)

    items = [(bi, p) for bi in range(n_batch) for p in range(n_pairs)]
    cols = [slice(p * pw, (p + 1) * pw) for p in range(n_pairs)]
    lane = lax.broadcasted_iota(jnp.int32, (c, pw), 1)
    ri = lax.broadcasted_iota(jnp.int32, (pw, pw), 0)
    ci = lax.broadcasted_iota(jnp.int32, (pw, pw), 1)
    same_head = (ri >= HEAD_DIM) == (ci >= HEAD_DIM)
    zero = jnp.zeros((c, pw), BF16)

    def stacked_diag(m):
        return jnp.concatenate([jnp.where(lane < HEAD_DIM, m, zero), jnp.where(lane >= HEAD_DIM, m, zero)], axis=0)

    states = [s_ref[bi * n_pairs + p] for bi, p in items]
    for n in range(n_chunks):
        rows = slice(n * c, (n + 1) * c)
        kqs = [_dot(jnp.concatenate([k_ref[bi, rows, cols[p]], q_ref[bi, rows, cols[p]]], axis=0), st.astype(BF16))
               for (bi, p), st in zip(items, states)]
        wks = [_dot(wp_ref[bi, rows, cols[p]], stacked_diag(kq[:c].astype(BF16))) for (bi, p), kq in zip(items, kqs)]
        xb = [(u_ref[bi, rows, cols[p]].astype(F32) - wk).astype(BF16) for (bi, p), wk in zip(items, wks)]
        upd = [_dot(kts_ref[bi, cols[p], rows], x) for (bi, p), x in zip(items, xb)]
        states = [st * egl_ref[bi, n, :, cols[p]] + jnp.where(same_head, up, 0.0)
                  for (bi, p), st, up in zip(items, states, upd)]
        intra = [_dot(at_ref[bi, rows, cols[p]], stacked_diag(x)) for (bi, p), x in zip(items, xb)]
        for (bi, p), it, kq in zip(items, intra, kqs):
            o_ref[bi, rows, cols[p]] = (it + ec_ref[bi, rows, cols[p]].astype(F32) * kq[c:]).astype(BF16)
    for (bi, p), st in zip(items, states):
        s_ref[bi * n_pairs + p] = st


def _dn_scan(q, k, u, wp, at, ec, kts, egl, *, tb, n_batch):
    b, s, tw = q.shape
    n_chunks = tb // CHUNK
    pw = 2 * HEAD_DIM
    tok = pl.BlockSpec((n_batch, tb, tw), lambda i, j: (i, j, 0))
    return pl.pallas_call(
        functools.partial(_dn_scan_kernel, n_chunks=n_chunks, n_pairs=tw // pw, n_batch=n_batch),
        grid=(b // n_batch, s // tb),
        in_specs=[tok, tok, tok, tok, tok, tok,
                  pl.BlockSpec((n_batch, tw, tb), lambda i, j: (i, 0, j)),
                  pl.BlockSpec((n_batch, n_chunks, 1, tw), lambda i, j: (i, j, 0, 0))],
        out_specs=tok,
        out_shape=jax.ShapeDtypeStruct((b, s, tw), BF16),
        scratch_shapes=[pltpu.VMEM((n_batch * (tw // pw), pw, pw), F32)],
        compiler_params=_cparams(("parallel", "arbitrary")),
        name="dn_scan",
    )(q, k, u, wp, at, ec, kts, egl)


def _mixer_tail(x, mix_b, cross_b, wo_mix_ref, wo_mem_ref, fg_ref, wr_ref, br_ref,
                x1_ref, rows_ref, code_ref, cnt_ref, carry_ref):
    tm, d = x.shape
    assert d == SUBLANES * LANES
    y = _dot(mix_b, wo_mix_ref[...]) + _dot(cross_b, wo_mem_ref[...])
    x1 = x + y
    x1_ref[...] = x1
    h2 = _rms(x1, fg_ref[...])
    logits = _dot(h2.astype(BF16), wr_ref[...]) + br_ref[...]
    cls = _router_class(logits)
    for c in range(SUBLANES):
        rows_ref[pl.ds(c, tm, stride=SUBLANES), :] = h2[:, c * LANES:(c + 1) * LANES]

    @pl.when(pl.program_id(0) == 0)
    def _():
        carry_ref[...] = jnp.zeros(carry_ref.shape, F32)

    lane = lax.broadcasted_iota(jnp.int32, (tm, LANES), 1)
    onehot = lane == cls
    ii = lax.broadcasted_iota(jnp.int32, (tm, tm), 0)
    jj = lax.broadcasted_iota(jnp.int32, (tm, tm), 1)
    earlier = (ii > jj).astype(BF16)
    before = _dot(earlier, onehot.astype(BF16)) + carry_ref[...]
    rank = jnp.sum(jnp.where(onehot, before, 0.0), axis=-1, keepdims=True).astype(jnp.int32)
    carry_ref[...] = carry_ref[...] + jnp.sum(onehot.astype(F32), axis=0, keepdims=True)
    cnt_ref[...] = carry_ref[...]
    cols = jnp.where(lane == 0, cls, jnp.where(lane == 1, rank >> 8, jnp.where(lane == 2, rank & 255, 0)))
    l8 = lax.broadcasted_iota(jnp.int32, (SUBLANES, LANES), 1)
    s8 = lax.broadcasted_iota(jnp.int32, (SUBLANES, LANES), 0)
    picked = _dot_nt((l8 == s8).astype(BF16), cols.astype(F32).astype(BF16)).astype(jnp.int32)
    code_ref[0] = (picked[0:1] << RANK_BITS) | (picked[1:2] << 8) | picked[2:3]


def _tail_out_specs(tm, d):
    return [pl.BlockSpec((tm, d), lambda i: (i, 0)),
            pl.BlockSpec((tm * SUBLANES, LANES), lambda i: (i, 0)),
            pl.BlockSpec((1, 1, tm), lambda i: (i, 0, 0)),
            pl.BlockSpec((1, LANES), lambda i: (0, 0))]


def _tail_out_shapes(t, d, tm):
    return [jax.ShapeDtypeStruct((t, d), F32), jax.ShapeDtypeStruct((t * SUBLANES, LANES), F32),
            jax.ShapeDtypeStruct((t // tm, 1, tm), jnp.int32), jax.ShapeDtypeStruct((1, LANES), F32)]


def _dn_out_kernel(x_ref, o_ref, zg_ref, cross_ref, og_ref, wo_mix_ref, wo_mem_ref, fg_ref, wr_ref, br_ref,
                   x1_ref, rows_ref, code_ref, cnt_ref, carry_ref, *, n_heads):
    og = og_ref[...]
    parts = []
    for hd in range(n_heads):
        cols = slice(hd * HEAD_DIM, (hd + 1) * HEAD_DIM)
        oh = o_ref[:, cols].astype(F32)
        on = oh * lax.rsqrt(jnp.mean(oh * oh, axis=-1, keepdims=True) + EPS) * og
        parts.append((on * zg_ref[:, cols].astype(F32)).astype(BF16))
    mix = jnp.concatenate(parts, axis=1)
    _mixer_tail(x_ref[...], mix, cross_ref[...], wo_mix_ref, wo_mem_ref, fg_ref, wr_ref, br_ref,
                x1_ref, rows_ref, code_ref, cnt_ref, carry_ref)


def _dn_out(x2d, o2d, zg2d, cross2d, og, wo_mix, wo_mem, fg, wr, br, *, tm):
    t, d = x2d.shape
    tw = o2d.shape[1]
    n_heads = tw // HEAD_DIM
    const = lambda shape: pl.BlockSpec(shape, lambda i: (0,) * len(shape))
    tok = lambda w: pl.BlockSpec((tm, w), lambda i: (i, 0))
    return pl.pallas_call(
        functools.partial(_dn_out_kernel, n_heads=n_heads),
        grid=(t // tm,),
        in_specs=[tok(d), tok(tw), tok(tw), tok(MEM_WIDTH), const((1, HEAD_DIM)), const((tw, d)),
                  const((MEM_WIDTH, d)), const((1, d)), const((d, LANES)), const((1, LANES))],
        out_specs=_tail_out_specs(tm, d),
        out_shape=_tail_out_shapes(t, d, tm),
        scratch_shapes=[pltpu.VMEM((1, LANES), F32)],
        compiler_params=_cparams(("arbitrary",)),
        name="dn_out",
    )(x2d, o2d, zg2d, cross2d, og, wo_mix, wo_mem, fg, wr, br)


def _row_tile(ref, r):
    return ref.at[pl.ds(pl.multiple_of(r * SUBLANES, SUBLANES), SUBLANES), :]


def _dispatch_kernel(code_ref, starts_ref, fill_ref, rows_ref, xs_hbm, pos_ref, zero_ref, sem, pad_sem, *, td):
    rank_mask = (1 << RANK_BITS) - 1

    @pl.when(pl.program_id(0) == 0)
    def _():
        zero_ref[...] = jnp.zeros(zero_ref.shape, F32)
        n_fill = fill_ref.shape[0] // 2
        for c in range(n_fill):
            first, n_pad = fill_ref[c], fill_ref[n_fill + c]

            def fill(kk, carry, first=first):
                pltpu.make_async_copy(zero_ref, _row_tile(xs_hbm, first + kk), pad_sem).start()
                return carry

            def drain(kk, carry, first=first):
                pltpu.make_async_copy(zero_ref, _row_tile(xs_hbm, first + kk), pad_sem).wait()
                return carry

            lax.fori_loop(0, n_pad, fill, 0)
            lax.fori_loop(0, n_pad, drain, 0)

    def issue(pair, carry):
        for prio in range(DMA_PRIORITIES):
            r = pair * DMA_PRIORITIES + prio
            code = code_ref[r]
            slot = starts_ref[code >> RANK_BITS] + (code & rank_mask)
            pos_ref[r] = slot
            pltpu.make_async_copy(_row_tile(rows_ref, r), _row_tile(xs_hbm, slot), sem).start(priority=prio)
        return carry

    lax.fori_loop(0, td // DMA_PRIORITIES, issue, 0, unroll=4)
    pltpu.make_async_copy(rows_ref, xs_hbm.at[pl.ds(0, td * SUBLANES), :], sem).wait()


def _dispatch(code, starts, fill, rows, n_slots, *, td):
    t = code.shape[0]
    smem_whole = lambda n: pl.BlockSpec((n,), lambda i: (0,), memory_space=pltpu.SMEM)
    return pl.pallas_call(
        functools.partial(_dispatch_kernel, td=td),
        grid=(t // td,),
        in_specs=[pl.BlockSpec((td,), lambda i: (i,), memory_space=pltpu.SMEM),
                  smem_whole(starts.shape[0]), smem_whole(fill.shape[0]),
                  pl.BlockSpec((td * SUBLANES, LANES), lambda i: (i, 0))],
        out_specs=[pl.BlockSpec(memory_space=pl.ANY),
                   pl.BlockSpec((td,), lambda i: (i,), memory_space=pltpu.SMEM)],
        out_shape=[jax.ShapeDtypeStruct((n_slots * SUBLANES, LANES), F32),
                   jax.ShapeDtypeStruct((t,), jnp.int32)],
        scratch_shapes=[pltpu.VMEM((SUBLANES, LANES), F32), pltpu.SemaphoreType.DMA(()),
                        pltpu.SemaphoreType.DMA(())],
        compiler_params=_cparams(("arbitrary",)),
        name="moe_dispatch",
    )(code, starts, fill, rows)


def _experts_kernel(ea_ref, eb_ref, used_ref, xs_ref, wr_ref, br_ref, wgu_a_ref, wgu_b_ref, wd_a_ref, wd_b_ref,
                    ys_ref, *, tile, d, d_expert):
    i = pl.program_id(0)

    @pl.when(used_ref[i] == 1)
    def _():
        h = jnp.concatenate([xs_ref[pl.ds(c, tile, stride=SUBLANES), :] for c in range(d // LANES)],
                            axis=1).astype(BF16)
        gates = _pair_gates(_dot(h, wr_ref[...]) + br_ref[...], ea_ref[i], eb_ref[i])
        parts = []
        for wgu_ref, wt in ((wgu_a_ref, gates[0]), (wgu_b_ref, gates[1])):
            gu = _dot(h, wgu_ref[0])
            gpart, upart = gu[:, :d_expert], gu[:, d_expert:]
            parts.append((gpart * _sigmoid(gpart) * upart * wt).astype(BF16))
        y = _dot(parts[0], wd_a_ref[0]) + _dot(parts[1], wd_b_ref[0])
        for c in range(d // LANES):
            ys_ref[pl.ds(c, tile, stride=SUBLANES), :] = y[:, c * LANES:(c + 1) * LANES]

    @pl.when(used_ref[i] == 0)
    def _():
        ys_ref[...] = jnp.zeros(ys_ref.shape, F32)


def _experts(tile_lo, tile_hi, tile_used, xs, wr, br, wgu, wd, *, d):
    n_tiles = tile_lo.shape[0]
    de2 = wgu.shape[2]
    blk = MOE_TILE * SUBLANES
    grid_spec = pltpu.PrefetchScalarGridSpec(
        num_scalar_prefetch=3,
        grid=(n_tiles,),
        in_specs=[pl.BlockSpec((blk, LANES), lambda i, ea, eb, us: (i * us[i], 0)),
                  pl.BlockSpec((d, LANES), lambda i, ea, eb, us: (0, 0)),
                  pl.BlockSpec((1, LANES), lambda i, ea, eb, us: (0, 0)),
                  pl.BlockSpec((1, d, de2), lambda i, ea, eb, us: (ea[i], 0, 0)),
                  pl.BlockSpec((1, d, de2), lambda i, ea, eb, us: (eb[i], 0, 0)),
                  pl.BlockSpec((1, de2 // 2, d), lambda i, ea, eb, us: (ea[i], 0, 0)),
                  pl.BlockSpec((1, de2 // 2, d), lambda i, ea, eb, us: (eb[i], 0, 0))],
        out_specs=pl.BlockSpec((blk, LANES), lambda i, ea, eb, us: (i, 0)))
    return pl.pallas_call(
        functools.partial(_experts_kernel, tile=MOE_TILE, d=d, d_expert=de2 // 2),
        grid_spec=grid_spec,
        out_shape=jax.ShapeDtypeStruct((n_tiles * blk, LANES), F32),
        compiler_params=_cparams(("arbitrary",)),
        name="moe_experts",
    )(tile_lo, tile_hi, tile_used, xs, wr, br, wgu, wgu, wd, wd)


def _gather_start(pos_ref, ys_hbm, buf_ref, sem, n):
    def issue(pair, carry):
        for prio in range(DMA_PRIORITIES):
            r = pair * DMA_PRIORITIES + prio
            pltpu.make_async_copy(_row_tile(ys_hbm, pos_ref[r]), _row_tile(buf_ref, r), sem).start(priority=prio)
        return carry

    lax.fori_loop(0, n // DMA_PRIORITIES, issue, 0, unroll=4)


def _gather_wait(ys_hbm, buf_ref, sem, n):
    pltpu.make_async_copy(ys_hbm.at[pl.ds(0, n * SUBLANES), :], buf_ref, sem).wait()


def _token_major(buf_ref, n, d):
    return jnp.concatenate([buf_ref[pl.ds(c, n, stride=SUBLANES), :] for c in range(d // LANES)], axis=1)


def _combine_kernel(pos_ref, ys_hbm, x_ref, fin_ref, out_ref, buf_ref, sem, *, tc, d, final_norm):
    _gather_start(pos_ref, ys_hbm, buf_ref, sem, tc)
    _gather_wait(ys_hbm, buf_ref, sem, tc)
    out = x_ref[...] + _token_major(buf_ref, tc, d)
    if final_norm:
        out = _rms(out, fin_ref[...])
    out_ref[...] = out


def _combine(pos, ys, x1, fin_g, *, tc, final_norm):
    t, d = x1.shape
    return pl.pallas_call(
        functools.partial(_combine_kernel, tc=tc, d=d, final_norm=final_norm),
        grid=(t // tc,),
        in_specs=[pl.BlockSpec((tc,), lambda i: (i,), memory_space=pltpu.SMEM),
                  pl.BlockSpec(memory_space=pl.ANY),
                  pl.BlockSpec((tc, d), lambda i: (i, 0)),
                  pl.BlockSpec((1, d), lambda i: (0, 0))],
        out_specs=pl.BlockSpec((tc, d), lambda i: (i, 0)),
        out_shape=jax.ShapeDtypeStruct((t, d), F32),
        scratch_shapes=[pltpu.VMEM((tc * SUBLANES, LANES), F32), pltpu.SemaphoreType.DMA(())],
        compiler_params=_cparams(("arbitrary",)),
        name="moe_combine",
    )(pos, ys, x1, fin_g)


def _moe_sparse(x1, rows, code, counts, wr, br, wgu, wd):
    t, d = x1.shape
    n_tiles = t // MOE_TILE + N_CLASSES
    cnt = counts[0, :N_CLASSES].astype(jnp.int32)
    padded = (cnt + MOE_TILE - 1) // MOE_TILE * MOE_TILE
    ends = jnp.cumsum(padded)
    starts = ends - padded
    n_slots = n_tiles * MOE_TILE
    fill = jnp.concatenate([starts + cnt, ends[-1:], padded - cnt, n_slots - ends[-1:]])
    tile_start = jnp.arange(n_tiles, dtype=jnp.int32) * MOE_TILE
    tile_cls = jnp.minimum(jnp.sum((tile_start[:, None] >= ends[None, :]).astype(jnp.int32), axis=1), N_CLASSES - 1)
    lo_ids, hi_ids = _class_experts()
    onehot = (tile_cls[:, None] == jnp.arange(N_CLASSES, dtype=jnp.int32)[None, :]).astype(jnp.int32)
    tile_lo = jnp.sum(onehot * jnp.asarray(lo_ids, jnp.int32)[None, :], axis=1)
    tile_hi = jnp.sum(onehot * jnp.asarray(hi_ids, jnp.int32)[None, :], axis=1)
    tile_used = (tile_start < ends[-1]).astype(jnp.int32)
    xs, pos = _dispatch(code.reshape(t), starts, fill, rows, n_slots, td=1024)
    ys = _experts(tile_lo, tile_hi, tile_used, xs, wr, br, wgu, wd, d=d)
    return ys, pos


def _gm_layer_kernel(pos_ref, pos_next_ref, ys_hbm, x_ref, ng_ref, win_ref, wqm_ref, lng_ref, lnb_ref, wsp_ref,
                     bsp_ref, kv_ref, wo_mix_ref, wo_mem_ref, fg_ref, wr_ref, br_ref,
                     x1_ref, rows_ref, code_ref, cnt_ref, carry_ref, buf_ref, sems, *, tm, n_groups):
    c = CHUNK
    tw = n_groups * HEAD_DIM
    d = x_ref.shape[1]
    i = pl.program_id(0)
    slot = lax.rem(i, 2)

    @pl.when(i == 0)
    def _():
        _gather_start(pos_ref, ys_hbm, buf_ref.at[0], sems.at[0], tm)

    @pl.when(i + 1 < pl.num_programs(0))
    def _():
        _gather_start(pos_next_ref, ys_hbm, buf_ref.at[1 - slot], sems.at[1 - slot], tm)

    _gather_wait(ys_hbm, buf_ref.at[slot], sems.at[slot], tm)
    x = x_ref[...] + _token_major(buf_ref.at[slot], tm, d)
    h = _rms(x, ng_ref[...]).astype(BF16)
    proj = _dot(h, win_ref[...])
    uv = 0.5 * proj * (1.0 + lax.erf(proj * (2.0 ** -0.5)))
    u, v = uv[:, :tw], uv[:, tw:]
    mu = jnp.mean(v, axis=-1, keepdims=True)
    vc = v - mu
    var = jnp.mean(vc * vc, axis=-1, keepdims=True)
    vn = (vc * lax.rsqrt(var + EPS) * lng_ref[...] + lnb_ref[...]).astype(BF16)
    ii = lax.broadcasted_iota(jnp.int32, (c, c), 0)
    jj = lax.broadcasted_iota(jnp.int32, (c, c), 1)
    tril = ii >= jj
    n_chunks = tm // c
    col_parts = []
    for g in range(n_groups):
        wc = jnp.where(tril, wsp_ref[g], jnp.zeros((c, c), BF16))
        cols = slice(g * HEAD_DIM, (g + 1) * HEAD_DIM)
        wide = _dot(wc, jnp.concatenate([vn[n * c:(n + 1) * c, cols] for n in range(n_chunks)], axis=1))
        bias = bsp_ref[g]
        col_parts.append(jnp.concatenate(
            [wide[:, n * HEAD_DIM:(n + 1) * HEAD_DIM] + bias for n in range(n_chunks)], axis=0))
    sgate = jnp.concatenate(col_parts, axis=1)
    mix = (u * sgate).astype(BF16)
    qm = _dot(h, wqm_ref[...])
    kv = kv_ref[0]
    cross = _cross_attn(qm, kv[:, :MEM_WIDTH], kv[:, MEM_WIDTH:]).astype(BF16)
    _mixer_tail(x, mix, cross, wo_mix_ref, wo_mem_ref, fg_ref, wr_ref, br_ref,
                x1_ref, rows_ref, code_ref, cnt_ref, carry_ref)


def _gm_layer(pos, ys, x2d, ng, win, wqm, lng, lnb, wsp, bsp, kv, kv_layer, wo_mix, wo_mem, fg, wr, br, *, tm, seq):
    t, d = x2d.shape
    n_steps = t // tm
    tw = lng.shape[1]
    n_groups = tw // HEAD_DIM
    mlen = kv.shape[1]
    tiles_per_seq = seq // tm
    const = lambda shape: pl.BlockSpec(shape, lambda i: (0,) * len(shape))
    tok = lambda w: pl.BlockSpec((tm, w), lambda i: (i, 0))
    return pl.pallas_call(
        functools.partial(_gm_layer_kernel, tm=tm, n_groups=n_groups),
        grid=(t // tm,),
        in_specs=[pl.BlockSpec((tm,), lambda i: (i,), memory_space=pltpu.SMEM),
                  pl.BlockSpec((tm,), lambda i: (jnp.minimum(i + 1, n_steps - 1),), memory_space=pltpu.SMEM),
                  pl.BlockSpec(memory_space=pl.ANY),
                  tok(d), const((1, d)), const((d, 2 * tw)), const((d, MEM_WIDTH)), const((1, tw)), const((1, tw)),
                  const((n_groups, CHUNK, CHUNK)), const((n_groups, CHUNK, HEAD_DIM)),
                  pl.BlockSpec((1, mlen, 2 * MEM_WIDTH), lambda i: (i // tiles_per_seq, 0, kv_layer)),
                  const((tw, d)), const((MEM_WIDTH, d)), const((1, d)), const((d, LANES)), const((1, LANES))],
        out_specs=_tail_out_specs(tm, d),
        out_shape=_tail_out_shapes(t, d, tm),
        scratch_shapes=[pltpu.VMEM((1, LANES), F32), pltpu.VMEM((2, tm * SUBLANES, LANES), F32),
                        pltpu.SemaphoreType.DMA((2,))],
        compiler_params=_cparams(("arbitrary",)),
        name="gm_layer",
    )(pos, pos, ys, x2d, ng, win, wqm, lng, lnb, wsp, bsp, kv, wo_mix, wo_mem, fg, wr, br)


def _router_params(w_group, b_group, w_router, b_router):
    d = w_group.shape[0]
    pad = LANES - N_EXPERTS - MOE_GROUPS
    wr = jnp.concatenate([w_router, w_group, jnp.zeros((d, pad), F32)], axis=1).astype(BF16)
    br = jnp.concatenate([b_router, b_group, jnp.zeros((pad,), F32)])[None, :]
    return wr, br


def _col16(vec):
    return jnp.concatenate([vec, jnp.zeros((16 - vec.shape[0],), F32)])[:, None]


def kernel(x, mem, mem_norm_g, mix_norm_g, w_out, w_mem_kv, dn_w_in, dn_conv_w, dn_a_log, dn_dt_bias, dn_o_norm_g,
           gm_w_in, gm_ln_g, gm_ln_b, gm_w_spatial, gm_b_spatial, ffn_norm_g, moe_w_group, moe_b_group,
           moe_w_router, moe_b_router, moe_w_gate, moe_w_up, moe_w_down, final_norm_g):
    b, s, d = x.shape
    tw = d - MEM_WIDTH
    n_heads = tw // HEAD_DIM
    t = b * s

    kv = _memkv(mem, mem_norm_g[None, :], jnp.concatenate([w_mem_kv[0], w_mem_kv[1]], axis=1).astype(BF16))

    w_in = dn_w_in[0]
    o1, o2, o3, o4 = 3 * tw, 4 * tw, 4 * tw + n_heads, 4 * tw + 2 * n_heads
    wabt = jnp.concatenate([w_in[:, o2:o4].T, jnp.zeros((16 - 2 * n_heads, d), F32)], axis=0).astype(BF16)
    q, k, v, zg, gb, cross = _dn_front(
        x, mix_norm_g[0][None, :], w_in[:, :o1].astype(BF16), w_in[:, o1:o2].astype(BF16),
        w_in[:, o4:].astype(BF16), wabt, dn_conv_w[0], _col16(dn_a_log[0]), _col16(dn_dt_bias[0]), kv, tm=512)
    u, wp, at, ec, kts, egl = _dn_intra(q, k, v, gb, tb=1024)
    o = _dn_scan(q, k, u, wp, at, ec, kts, egl, tb=512, n_batch=2 if b % 2 == 0 else 1)
    wr0, br0 = _router_params(moe_w_group[0], moe_b_group[0], moe_w_router[0], moe_b_router[0])
    wo0 = w_out[0].astype(BF16)
    x1, rows0, code0, cnt0 = _dn_out(
        x.reshape(t, d), o.reshape(t, tw), zg.reshape(t, tw), cross.reshape(t, MEM_WIDTH),
        dn_o_norm_g[0][None, :], wo0[:tw], wo0[tw:], ffn_norm_g[0][None, :], wr0, br0, tm=512)
    wgu0 = jnp.concatenate([moe_w_gate[0], moe_w_up[0]], axis=2).astype(BF16)
    ys0, pos0 = _moe_sparse(x1, rows0, code0, cnt0, wr0, br0, wgu0, moe_w_down[0].astype(BF16))

    win1 = gm_w_in[0]
    wr1, br1 = _router_params(moe_w_group[1], moe_b_group[1], moe_w_router[1], moe_b_router[1])
    wo1 = w_out[1].astype(BF16)
    bsp = jnp.broadcast_to(gm_b_spatial[0][:, :, None], (n_heads, CHUNK, HEAD_DIM))
    x3, rows1, code1, cnt1 = _gm_layer(
        pos0, ys0, x1, mix_norm_g[1][None, :], win1[:, :2 * tw].astype(BF16), win1[:, 2 * tw:].astype(BF16),
        gm_ln_g[0][None, :], gm_ln_b[0][None, :], gm_w_spatial[0].astype(BF16), bsp, kv, 1,
        wo1[:tw], wo1[tw:], ffn_norm_g[1][None, :], wr1, br1, tm=512, seq=s)
    wgu1 = jnp.concatenate([moe_w_gate[1], moe_w_up[1]], axis=2).astype(BF16)
    ys1, pos1 = _moe_sparse(x3, rows1, code1, cnt1, wr1, br1, wgu1, moe_w_down[1].astype(BF16))
    out = _combine(pos1, ys1, x3, final_norm_g[None, :], tc=512, final_norm=True)
    return out.reshape(b, s, d)
```

```python
import functools

import jax
import jax.numpy as jnp
from jax import lax
from jax.experimental import pallas as pl
from jax.experimental.pallas import tpu as pltpu

F32 = jnp.float32
BF16 = jnp.bfloat16
EPS = 1e-6

MEM_HEADS = 4
MEM_HEAD_DIM = 64
MEM_WIDTH = MEM_HEADS * MEM_HEAD_DIM
HEAD_DIM = 128
CONV_K = 4
CHUNK = 128
MOE_GROUPS = 4
EXPERTS_PER_GROUP = 4
N_EXPERTS = MOE_GROUPS * EXPERTS_PER_GROUP
PAIRS_PER_GROUP = EXPERTS_PER_GROUP * (EXPERTS_PER_GROUP - 1) // 2
N_CLASSES = MOE_GROUPS * PAIRS_PER_GROUP
MOE_TILE = 256
LANES = 128
SUBLANES = 8
RANK_BITS = 16
DMA_PRIORITIES = 2
VMEM_LIMIT = 56 * 1024 * 1024

NT_DIMS = (((1,), (1,)), ((), ()))
NEG_LOG2_E = -1.4426950408889634


def _dot(a, b):
    return jnp.dot(a, b, preferred_element_type=F32)


def _dot_nt(a, b):
    return lax.dot_general(a, b, NT_DIMS, preferred_element_type=F32)


def _rms(x, g):
    return x * lax.rsqrt(jnp.mean(x * x, axis=-1, keepdims=True) + EPS) * g


def _sigmoid(x):
    return 1.0 / (1.0 + jnp.exp2(x * NEG_LOG2_E))


def _softplus(x):
    return jnp.maximum(x, 0.0) + jnp.log1p(jnp.exp(-jnp.abs(x)))


def _split3(x):
    hi = x.astype(BF16)
    r = x - hi.astype(F32)
    mid = r.astype(BF16)
    lo = (r - mid.astype(F32)).astype(BF16)
    return hi, mid, lo


def _cparams(sem):
    return pltpu.CompilerParams(dimension_semantics=sem, vmem_limit_bytes=VMEM_LIMIT)


def _memkv_kernel(mem_ref, g_ref, w_ref, kv_ref):
    mn = _rms(mem_ref[0], g_ref[...]).astype(BF16)
    kv_ref[0] = _dot(mn, w_ref[...]).astype(BF16)


def _memkv(mem, g, w_all):
    b, m, d = mem.shape
    n = w_all.shape[1]
    return pl.pallas_call(
        _memkv_kernel,
        grid=(b,),
        in_specs=[pl.BlockSpec((1, m, d), lambda i: (i, 0, 0)),
                  pl.BlockSpec((1, d), lambda i: (0, 0)),
                  pl.BlockSpec((d, n), lambda i: (0, 0))],
        out_specs=pl.BlockSpec((1, m, n), lambda i: (i, 0, 0)),
        out_shape=jax.ShapeDtypeStruct((b, m, n), BF16),
        compiler_params=_cparams(("parallel",)),
        name="memkv",
    )(mem, g, w_all)


def _cross_attn(qm, kmem, vmem):
    lane = lax.broadcasted_iota(jnp.int32, (1, MEM_WIDTH), 1)
    out = jnp.zeros(qm.shape, F32)
    for h in range(MEM_HEADS):
        msk = (lane >= h * MEM_HEAD_DIM) & (lane < (h + 1) * MEM_HEAD_DIM)
        qh = jnp.where(msk, qm, 0.0).astype(BF16)
        s = _dot_nt(qh, kmem) * (MEM_HEAD_DIM ** -0.5)
        p = jnp.exp(s - jnp.max(s, axis=-1, keepdims=True))
        inv = 1.0 / jnp.sum(p, axis=-1, keepdims=True)
        vh = jnp.where(msk, vmem, jnp.zeros_like(vmem))
        out = out + _dot(p.astype(BF16), vh) * inv
    return out


def _router_class(logits):
    lane = lax.broadcasted_iota(jnp.int32, logits.shape, 1).astype(F32)
    neg = jnp.float32(-jnp.inf)
    big = jnp.float32(1 << 20)
    first = lambda hit: jnp.min(jnp.where(hit, lane, big), axis=-1, keepdims=True)
    is_g = (lane >= N_EXPERTS) & (lane < N_EXPERTS + MOE_GROUPS)
    gl = jnp.where(is_g, logits, neg)
    g_idx = first(gl == jnp.max(gl, axis=-1, keepdims=True)) - N_EXPERTS
    lo = g_idx * EXPERTS_PER_GROUP
    in_grp = (lane >= lo) & (lane < lo + EXPERTS_PER_GROUP)
    el = jnp.where(in_grp, logits, neg)
    i1 = first(el == jnp.max(el, axis=-1, keepdims=True))
    el2 = jnp.where(lane == i1, neg, el)
    i2 = first(el2 == jnp.max(el2, axis=-1, keepdims=True))
    e_lo = jnp.minimum(i1, i2) - lo
    e_hi = jnp.maximum(i1, i2) - lo
    pair_off = jnp.where(e_lo == 0.0, 0.0, jnp.where(e_lo == 1.0, 3.0, 5.0))
    return (g_idx * PAIRS_PER_GROUP + pair_off + e_hi - e_lo - 1.0).astype(jnp.int32)


def _pair_gates(logits, e_lo, e_hi):
    lane = lax.broadcasted_iota(jnp.int32, logits.shape, 1)
    pick = lambda idx: jnp.sum(jnp.where(lane == idx, logits, 0.0), axis=-1, keepdims=True)
    l_lo, l_hi = pick(e_lo), pick(e_hi)
    g_sel = pick(N_EXPERTS + e_lo // EXPERTS_PER_GROUP)
    is_g = (lane >= N_EXPERTS) & (lane < N_EXPERTS + MOE_GROUPS)
    p_grp = 1.0 / jnp.sum(jnp.where(is_g, jnp.exp(logits - g_sel), 0.0), axis=-1, keepdims=True)
    return p_grp / (1.0 + jnp.exp(l_hi - l_lo)), p_grp / (1.0 + jnp.exp(l_lo - l_hi))


def _class_experts():
    lo_ids, hi_ids = [], []
    for g in range(MOE_GROUPS):
        for a in range(EXPERTS_PER_GROUP):
            for b in range(a + 1, EXPERTS_PER_GROUP):
                lo_ids.append(g * EXPERTS_PER_GROUP + a)
                hi_ids.append(g * EXPERTS_PER_GROUP + b)
    return lo_ids, hi_ids


def _dn_front_kernel(x_ref, ng_ref, wqkv_ref, wz_ref, wqm_ref, wabt_ref, convw_ref, alog_ref, dtb_ref, kv_ref,
                     q_ref, k_ref, v_ref, zg_ref, gb_ref, cross_ref, cbuf_ref, *, tm, n_heads):
    j = pl.program_id(1)
    tw = n_heads * HEAD_DIM
    h = _rms(x_ref[0], ng_ref[...]).astype(BF16)

    @pl.when(j == 0)
    def _():
        cbuf_ref[0:8, :] = jnp.zeros((8, 3 * tw), F32)

    gw = 2 * HEAD_DIM
    out_refs = (q_ref, k_ref, v_ref)
    scales = (HEAD_DIM ** -0.5, 1.0, None)
    for grp in range(3 * tw // gw):
        cols = slice(grp * gw, (grp + 1) * gw)
        pre = _dot(h, wqkv_ref[:, cols])
        cbuf_ref[8:8 + tm, cols] = pre
        w = convw_ref[:, cols]
        acc = w[CONV_K - 1:CONV_K, :] * pre
        for kk in range(CONV_K - 1):
            acc = acc + w[kk:kk + 1, :] * cbuf_ref[pl.ds(8 - (CONV_K - 1) + kk, tm), cols]
        cbuf_ref[0:8, cols] = cbuf_ref[tm:tm + 8, cols]
        act = acc * _sigmoid(acc)
        which, first = divmod(grp * gw, tw)
        for half in range(gw // HEAD_DIM):
            a = act[:, half * HEAD_DIM:(half + 1) * HEAD_DIM]
            if scales[which] is not None:
                a = a * (lax.rsqrt(jnp.sum(a * a, axis=-1, keepdims=True) + EPS) * scales[which])
            dst = first + half * HEAD_DIM
            out_refs[which][0, :, dst:dst + HEAD_DIM] = a.astype(BF16)

    z = _dot(h, wz_ref[...])
    zg_ref[0] = (z * _sigmoid(z)).astype(BF16)

    abt = _dot_nt(wabt_ref[...], h)
    g_t = -jnp.exp(alog_ref[...]) * _softplus(abt + dtb_ref[...])
    b_t = _sigmoid(abt)
    for hd in range(n_heads):
        gb_ref[0, hd, 0:1, :] = g_t[hd:hd + 1, :]
        gb_ref[0, hd, 1:2, :] = b_t[n_heads + hd:n_heads + hd + 1, :]

    qm = _dot(h, wqm_ref[...])
    kv = kv_ref[0]
    cross_ref[0] = _cross_attn(qm, kv[:, :MEM_WIDTH], kv[:, MEM_WIDTH:]).astype(BF16)


def _dn_front(x, ng, wqkv, wz, wqm, wabt, convw, alog, dtb, kv, *, tm):
    b, s, d = x.shape
    tw = wz.shape[1]
    n_heads = tw // HEAD_DIM
    mlen = kv.shape[1]
    const = lambda shape: pl.BlockSpec(shape, lambda i, j: (0,) * len(shape))
    tok = lambda w: pl.BlockSpec((1, tm, w), lambda i, j: (i, j, 0))
    return pl.pallas_call(
        functools.partial(_dn_front_kernel, tm=tm, n_heads=n_heads),
        grid=(b, s // tm),
        in_specs=[tok(d), const((1, d)), const((d, 3 * tw)), const((d, tw)), const((d, MEM_WIDTH)),
                  const((16, d)), const((CONV_K, 3 * tw)), const((16, 1)), const((16, 1)),
                  pl.BlockSpec((1, mlen, 2 * MEM_WIDTH), lambda i, j: (i, 0, 0))],
        out_specs=[tok(tw), tok(tw), tok(tw), tok(tw),
                   pl.BlockSpec((1, n_heads, 2, tm), lambda i, j: (i, 0, 0, j)),
                   tok(MEM_WIDTH)],
        out_shape=[jax.ShapeDtypeStruct((b, s, tw), BF16)] * 4
                  + [jax.ShapeDtypeStruct((b, n_heads, 2, s), F32),
                     jax.ShapeDtypeStruct((b, s, MEM_WIDTH), BF16)],
        scratch_shapes=[pltpu.VMEM((tm + 8, 3 * tw), F32)],
        compiler_params=_cparams(("parallel", "arbitrary")),
        name="dn_front",
    )(x, ng, wqkv, wz, wqm, wabt, convw, alog, dtb, kv)


def _cat_lanes(a, b):
    return jnp.concatenate([a, b], axis=1)


def _block_diag(a, b):
    z = jnp.zeros(a.shape, a.dtype)
    return jnp.concatenate([_cat_lanes(a, z), _cat_lanes(z, b)], axis=0)


def _pair_dot(a0, a1, b0, b1):
    w = a0.shape[1]
    r = _dot(_cat_lanes(a0, a1), _block_diag(b0, b1))
    return r[:, :w], r[:, w:]


def _tri_inverse_pairs(a_pairs, ii, jj):
    c = CHUNK
    eye = (ii == jj).astype(F32)
    blk = (ii >> 4) == (jj >> 4)
    n0 = [[jnp.where(blk, -a, 0.0) for a in pair] for pair in a_pairs]
    x = [[eye + n for n in pair] for pair in n0]
    nb = [[n.astype(BF16) for n in pair] for pair in n0]
    nb = [[r.astype(BF16) for r in _pair_dot(p[0], p[1], p[0], p[1])] for p in nb]
    for _ in range(2):
        r = [[_dot(n, _cat_lanes(n, xi.astype(BF16))) for n, xi in zip(pn, px)] for pn, px in zip(nb, x)]
        nb = [[ri[:, :c].astype(BF16) for ri in pr] for pr in r]
        x = [[xi + ri[:, c:] for xi, ri in zip(px, pr)] for px, pr in zip(x, r)]
    fin = [_pair_dot(pn[0], pn[1], px[0].astype(BF16), px[1].astype(BF16)) for pn, px in zip(nb, x)]
    x = [[xi + fi for xi, fi in zip(px, pf)] for px, pf in zip(x, fin)]
    shift = 4
    while (1 << shift) < c:
        inner = (ii >> shift) == (jj >> shift)
        outer = (ii >> (shift + 1)) == (jj >> (shift + 1))
        sel = outer & jnp.logical_not(inner)
        xb = [[xi.astype(BF16) for xi in px] for px in x]
        lo = [[jnp.where(sel, a, 0.0).astype(BF16) for a in pa] for pa in a_pairs]
        lx = [[r.astype(BF16) for r in _pair_dot(pl_[0], pl_[1], pb[0], pb[1])] for pl_, pb in zip(lo, xb)]
        cor = [_pair_dot(pb[0], pb[1], pl_[0], pl_[1]) for pb, pl_ in zip(xb, lx)]
        x = [[xi - ci for xi, ci in zip(px, pc)] for px, pc in zip(x, cor)]
        shift += 1
    return x


def _dn_intra_kernel(q_ref, k_ref, v_ref, gb_ref, u_ref, wp_ref, at_ref, qg_ref, kts_ref, egl_ref, *, n_chunks):
    c = CHUNK
    hw = HEAD_DIM
    ii = lax.broadcasted_iota(jnp.int32, (c, c), 0)
    jj = lax.broadcasted_iota(jnp.int32, (c, c), 1)
    tril = ii >= jj
    strict = ii > jj
    lane2 = lax.broadcasted_iota(jnp.int32, (c, 2 * hw), 1)
    upper_incl = (ii <= jj).astype(BF16)
    chunks = range(n_chunks)
    heads = range(2)
    rows = [slice(n * c, (n + 1) * c) for n in chunks]
    hcol = [slice(hd * hw, (hd + 1) * hw) for hd in heads]
    kp = [k_ref[0, r, :] for r in rows]
    zero = jnp.zeros((c, 2 * hw), BF16)
    kq = [_dot_nt(jnp.concatenate([kp[n], q_ref[0, rows[n], :]], axis=0),
                  jnp.concatenate([jnp.where(lane2 < hw, kp[n], zero), jnp.where(lane2 >= hw, kp[n], zero)], axis=0))
          for n in chunks]
    g_rows = [gb_ref[0, hd, 0:1, rows[n]] for n in chunks for hd in heads]
    hi, mid, lo = _split3(jnp.concatenate(g_rows, axis=0))
    n_rows = len(g_rows)
    cum = _dot(jnp.concatenate([hi, mid, lo], axis=0), upper_incl)
    gc_all = cum[:n_rows] + cum[n_rows:2 * n_rows] + cum[2 * n_rows:]
    a_pairs, b_rs, e_rs = [], [], []
    for n in chunks:
        a_pair, b_pair, e_pair = [], [], []
        for hd in heads:
            g_row = g_rows[n * 2 + hd]
            b_row = gb_ref[0, hd, 1:2, rows[n]]
            gc_row = gc_all[n * 2 + hd:n * 2 + hd + 1, :]
            gl = jnp.sum(g_row, axis=-1, keepdims=True)
            g_r = jnp.broadcast_to(gc_row, (c, c))
            g_c = g_r.T
            b_r = jnp.broadcast_to(b_row, (c, c))
            decay = jnp.where(tril, jnp.exp(g_c - g_r), 0.0)
            kqh = kq[n][:, hd * c:(hd + 1) * c]
            a_pair.append(jnp.where(strict, kqh[:c] * decay, 0.0) * b_r.T)
            at_ref[0, rows[n], hcol[hd]] = jnp.where(tril, kqh[c:] * decay, 0.0).astype(BF16)
            qg_ref[0, rows[n], hcol[hd]] = (q_ref[0, rows[n], hcol[hd]].astype(F32) * jnp.exp(g_c)).astype(BF16)
            kh = kp[n][:, hcol[hd]].astype(F32)
            kts_ref[0, hcol[hd], rows[n]] = (kh.T * jnp.exp(gl - gc_row)).astype(BF16)
            egl_ref[0, n, :, hcol[hd]] = jnp.broadcast_to(jnp.exp(gl), (1, hw))
            b_pair.append(b_r)
            e_pair.append(jnp.exp(g_r))
        a_pairs.append(a_pair)
        b_rs.append(b_pair)
        e_rs.append(e_pair)
    t = _tri_inverse_pairs(a_pairs, ii, jj)
    tb = [[t[n][hd] * b_rs[n][hd] for hd in heads] for n in chunks]
    u = [_pair_dot(tb[n][0].astype(BF16), tb[n][1].astype(BF16),
                   v_ref[0, rows[n], hcol[0]], v_ref[0, rows[n], hcol[1]]) for n in chunks]
    for n in chunks:
        for hd in heads:
            u_ref[0, rows[n], hcol[hd]] = u[n][hd].astype(BF16)
            wp_ref[0, rows[n], hcol[hd]] = (tb[n][hd] * e_rs[n][hd]).astype(BF16)


def _dn_intra(q, k, v, gb, *, tb):
    b, s, tw = q.shape
    n_chunks = tb // CHUNK
    pw = 2 * HEAD_DIM
    tok = pl.BlockSpec((1, tb, pw), lambda i, h, j: (i, j, h))
    return pl.pallas_call(
        functools.partial(_dn_intra_kernel, n_chunks=n_chunks),
        grid=(b, tw // pw, s // tb),
        in_specs=[tok, tok, tok, pl.BlockSpec((1, 2, 2, tb), lambda i, h, j: (i, h, 0, j))],
        out_specs=[tok, tok, tok, tok,
                   pl.BlockSpec((1, pw, tb), lambda i, h, j: (i, h, j)),
                   pl.BlockSpec((1, n_chunks, 1, pw), lambda i, h, j: (i, j, 0, h))],
        out_shape=[jax.ShapeDtypeStruct((b, s, tw), BF16)] * 4
                  + [jax.ShapeDtypeStruct((b, tw, s), BF16),
                     jax.ShapeDtypeStruct((b, s // CHUNK, 1, tw), F32)],
        compiler_params=_cparams(("parallel", "parallel", "parallel")),
        name="dn_intra",
    )(q, k, v, gb)


def _dn_scan_kernel(qg_ref, k_ref, u_ref, wp_ref, at_ref, kts_ref, egl_ref, o_ref, s_ref,
                    *, n_chunks, n_pairs, n_batch):
    c = CHUNK
    pw = 2 * HEAD_DIM

    @pl.when(pl.program_id(1) == 0)
    def _():
        s_ref[...] = jnp.zeros(s_ref.shape, F32)

    items = [(bi, p) for bi in range(n_batch) for p in range(n_pairs)]
    cols = [slice(p * pw, (p + 1) * pw) for p in range(n_pairs)]
    lane = lax.broadcasted_iota(jnp.int32, (c, pw), 1)
    ri = lax.broadcasted_iota(jnp.int32, (pw, pw), 0)
    ci = lax.broadcasted_iota(jnp.int32, (pw, pw), 1)
    same_head = (ri >= HEAD_DIM) == (ci >= HEAD_DIM)
    zero = jnp.zeros((c, pw), BF16)

    def stacked_diag(m):
        return jnp.concatenate([jnp.where(lane < HEAD_DIM, m, zero), jnp.where(lane >= HEAD_DIM, m, zero)], axis=0)

    states = [s_ref[bi * n_pairs + p] for bi, p in items]
    for n in range(n_chunks):
        rows = slice(n * c, (n + 1) * c)
        kqs = [_dot(jnp.concatenate([k_ref[bi, rows, cols[p]], qg_ref[bi, rows, cols[p]]], axis=0), st.astype(BF16))
               for (bi, p), st in zip(items, states)]
        wks = [_dot(wp_ref[bi, rows, cols[p]], stacked_diag(kq[:c].astype(BF16))) for (bi, p), kq in zip(items, kqs)]
        xb = [(u_ref[bi, rows, cols[p]].astype(F32) - wk).astype(BF16) for (bi, p), wk in zip(items, wks)]
        upd = [_dot(kts_ref[bi, cols[p], rows], x) for (bi, p), x in zip(items, xb)]
        states = [st * egl_ref[bi, n, :, cols[p]] + jnp.where(same_head, up, 0.0)
                  for (bi, p), st, up in zip(items, states, upd)]
        intra = [_dot(at_ref[bi, rows, cols[p]], stacked_diag(x)) for (bi, p), x in zip(items, xb)]
        for (bi, p), it, kq in zip(items, intra, kqs):
            o_ref[bi, rows, cols[p]] = (it + kq[c:]).astype(BF16)
    for (bi, p), st in zip(items, states):
        s_ref[bi * n_pairs + p] = st


def _dn_scan(qg, k, u, wp, at, kts, egl, *, tb, n_batch):
    b, s, tw = qg.shape
    n_chunks = tb // CHUNK
    pw = 2 * HEAD_DIM
    tok = pl.BlockSpec((n_batch, tb, tw), lambda i, j: (i, j, 0))
    return pl.pallas_call(
        functools.partial(_dn_scan_kernel, n_chunks=n_chunks, n_pairs=tw // pw, n_batch=n_batch),
        grid=(b // n_batch, s // tb),
        in_specs=[tok, tok, tok, tok, tok,
                  pl.BlockSpec((n_batch, tw, tb), lambda i, j: (i, 0, j)),
                  pl.BlockSpec((n_batch, n_chunks, 1, tw), lambda i, j: (i, j, 0, 0))],
        out_specs=tok,
        out_shape=jax.ShapeDtypeStruct((b, s, tw), BF16),
        scratch_shapes=[pltpu.VMEM((n_batch * (tw // pw), pw, pw), F32)],
        compiler_params=_cparams(("parallel", "arbitrary")),
        name="dn_scan",
    )(qg, k, u, wp, at, kts, egl)


def _mixer_tail(x, mix_b, cross_b, wo_mix_ref, wo_mem_ref, fg_ref, wr_ref, br_ref,
                x1_ref, rows_ref, code_ref, cnt_ref, carry_ref):
    tm, d = x.shape
    assert d == SUBLANES * LANES
    y = _dot(mix_b, wo_mix_ref[...]) + _dot(cross_b, wo_mem_ref[...])
    x1 = x + y
    x1_ref[...] = x1
    h2 = _rms(x1, fg_ref[...])
    logits = _dot(h2.astype(BF16), wr_ref[...]) + br_ref[...]
    cls = _router_class(logits)
    for c in range(SUBLANES):
        rows_ref[pl.ds(c, tm, stride=SUBLANES), :] = h2[:, c * LANES:(c + 1) * LANES]

    @pl.when(pl.program_id(0) == 0)
    def _():
        carry_ref[...] = jnp.zeros(carry_ref.shape, F32)

    lane = lax.broadcasted_iota(jnp.int32, (tm, LANES), 1)
    onehot = lane == cls
    ii = lax.broadcasted_iota(jnp.int32, (tm, tm), 0)
    jj = lax.broadcasted_iota(jnp.int32, (tm, tm), 1)
    earlier = (ii > jj).astype(BF16)
    before = _dot(earlier, onehot.astype(BF16)) + carry_ref[...]
    rank = jnp.sum(jnp.where(onehot, before, 0.0), axis=-1, keepdims=True).astype(jnp.int32)
    carry_ref[...] = carry_ref[...] + jnp.sum(onehot.astype(F32), axis=0, keepdims=True)
    cnt_ref[...] = carry_ref[...]
    cols = jnp.where(lane == 0, cls, jnp.where(lane == 1, rank >> 8, jnp.where(lane == 2, rank & 255, 0)))
    l8 = lax.broadcasted_iota(jnp.int32, (SUBLANES, LANES), 1)
    s8 = lax.broadcasted_iota(jnp.int32, (SUBLANES, LANES), 0)
    picked = _dot_nt((l8 == s8).astype(BF16), cols.astype(F32).astype(BF16)).astype(jnp.int32)
    code_ref[0] = (picked[0:1] << RANK_BITS) | (picked[1:2] << 8) | picked[2:3]


def _tail_out_specs(tm, d):
    return [pl.BlockSpec((tm, d), lambda i: (i, 0)),
            pl.BlockSpec((tm * SUBLANES, LANES), lambda i: (i, 0)),
            pl.BlockSpec((1, 1, tm), lambda i: (i, 0, 0)),
            pl.BlockSpec((1, LANES), lambda i: (0, 0))]


def _tail_out_shapes(t, d, tm):
    return [jax.ShapeDtypeStruct((t, d), F32), jax.ShapeDtypeStruct((t * SUBLANES, LANES), F32),
            jax.ShapeDtypeStruct((t // tm, 1, tm), jnp.int32), jax.ShapeDtypeStruct((1, LANES), F32)]


def _dn_out_kernel(x_ref, o_ref, zg_ref, cross_ref, og_ref, wo_mix_ref, wo_mem_ref, fg_ref, wr_ref, br_ref,
                   x1_ref, rows_ref, code_ref, cnt_ref, carry_ref, *, n_heads):
    og = og_ref[...]
    parts = []
    for hd in range(n_heads):
        cols = slice(hd * HEAD_DIM, (hd + 1) * HEAD_DIM)
        oh = o_ref[:, cols].astype(F32)
        on = oh * lax.rsqrt(jnp.mean(oh * oh, axis=-1, keepdims=True) + EPS) * og
        parts.append((on * zg_ref[:, cols].astype(F32)).astype(BF16))
    mix = jnp.concatenate(parts, axis=1)
    _mixer_tail(x_ref[...], mix, cross_ref[...], wo_mix_ref, wo_mem_ref, fg_ref, wr_ref, br_ref,
                x1_ref, rows_ref, code_ref, cnt_ref, carry_ref)


def _dn_out(x2d, o2d, zg2d, cross2d, og, wo_mix, wo_mem, fg, wr, br, *, tm):
    t, d = x2d.shape
    tw = o2d.shape[1]
    n_heads = tw // HEAD_DIM
    const = lambda shape: pl.BlockSpec(shape, lambda i: (0,) * len(shape))
    tok = lambda w: pl.BlockSpec((tm, w), lambda i: (i, 0))
    return pl.pallas_call(
        functools.partial(_dn_out_kernel, n_heads=n_heads),
        grid=(t // tm,),
        in_specs=[tok(d), tok(tw), tok(tw), tok(MEM_WIDTH), const((1, HEAD_DIM)), const((tw, d)),
                  const((MEM_WIDTH, d)), const((1, d)), const((d, LANES)), const((1, LANES))],
        out_specs=_tail_out_specs(tm, d),
        out_shape=_tail_out_shapes(t, d, tm),
        scratch_shapes=[pltpu.VMEM((1, LANES), F32)],
        compiler_params=_cparams(("arbitrary",)),
        name="dn_out",
    )(x2d, o2d, zg2d, cross2d, og, wo_mix, wo_mem, fg, wr, br)


def _row_tile(ref, r):
    return ref.at[pl.ds(pl.multiple_of(r * SUBLANES, SUBLANES), SUBLANES), :]


def _dispatch_kernel(code_ref, starts_ref, fill_ref, rows_ref, xs_hbm, pos_ref, zero_ref, sem, pad_sem, *, td):
    rank_mask = (1 << RANK_BITS) - 1

    @pl.when(pl.program_id(0) == 0)
    def _():
        zero_ref[...] = jnp.zeros(zero_ref.shape, F32)
        n_fill = fill_ref.shape[0] // 2

        def slots(ref, first, n):
            return ref.at[pl.ds(pl.multiple_of(first * SUBLANES, SUBLANES), n * SUBLANES), :]

        def walk(act):
            for c in range(n_fill):
                first, count = fill_ref[c], fill_ref[n_fill + c]
                n_whole = count // MOE_TILE

                def whole(kk, carry, first=first):
                    act(pltpu.make_async_copy(zero_ref, slots(xs_hbm, first + kk * MOE_TILE, MOE_TILE), pad_sem))
                    return carry

                lax.fori_loop(0, n_whole, whole, 0)
                off = first + n_whole * MOE_TILE
                rem = count - n_whole * MOE_TILE
                bit = MOE_TILE // 2
                while bit:
                    @pl.when((rem & bit) != 0)
                    def _(off=off, bit=bit):
                        act(pltpu.make_async_copy(slots(zero_ref, 0, bit), slots(xs_hbm, off, bit), pad_sem))

                    off = off + (rem & bit)
                    bit //= 2

        walk(lambda cp: cp.start())
        walk(lambda cp: cp.wait())

    def issue(pair, carry):
        for prio in range(DMA_PRIORITIES):
            r = pair * DMA_PRIORITIES + prio
            code = code_ref[r]
            slot = starts_ref[code >> RANK_BITS] + (code & rank_mask)
            pos_ref[r] = slot
            pltpu.make_async_copy(_row_tile(rows_ref, r), _row_tile(xs_hbm, slot), sem).start(priority=prio)
        return carry

    lax.fori_loop(0, td // DMA_PRIORITIES, issue, 0, unroll=4)
    pltpu.make_async_copy(rows_ref, xs_hbm.at[pl.ds(0, td * SUBLANES), :], sem).wait()


def _dispatch(code, starts, fill, rows, n_slots, *, td):
    t = code.shape[0]
    smem_whole = lambda n: pl.BlockSpec((n,), lambda i: (0,), memory_space=pltpu.SMEM)
    return pl.pallas_call(
        functools.partial(_dispatch_kernel, td=td),
        grid=(t // td,),
        in_specs=[pl.BlockSpec((td,), lambda i: (i,), memory_space=pltpu.SMEM),
                  smem_whole(starts.shape[0]), smem_whole(fill.shape[0]),
                  pl.BlockSpec((td * SUBLANES, LANES), lambda i: (i, 0))],
        out_specs=[pl.BlockSpec(memory_space=pl.ANY),
                   pl.BlockSpec((td,), lambda i: (i,), memory_space=pltpu.SMEM)],
        out_shape=[jax.ShapeDtypeStruct((n_slots * SUBLANES, LANES), F32),
                   jax.ShapeDtypeStruct((t,), jnp.int32)],
        scratch_shapes=[pltpu.VMEM((MOE_TILE * SUBLANES, LANES), F32), pltpu.SemaphoreType.DMA(()),
                        pltpu.SemaphoreType.DMA(())],
        compiler_params=_cparams(("arbitrary",)),
        name="moe_dispatch",
    )(code, starts, fill, rows)


def _experts_kernel(ea_ref, eb_ref, used_ref, xs_ref, wr_ref, br_ref, wgu_a_ref, wgu_b_ref, wd_a_ref, wd_b_ref,
                    ys_ref, *, tile, d, d_expert):
    i = pl.program_id(0)

    @pl.when(used_ref[i] == 1)
    def _():
        h = jnp.concatenate([xs_ref[pl.ds(c, tile, stride=SUBLANES), :] for c in range(d // LANES)],
                            axis=1).astype(BF16)
        gates = _pair_gates(_dot(h, wr_ref[...]) + br_ref[...], ea_ref[i], eb_ref[i])
        parts = []
        for wgu_ref, wt in ((wgu_a_ref, gates[0]), (wgu_b_ref, gates[1])):
            gu = _dot(h, wgu_ref[0])
            gpart, upart = gu[:, :d_expert], gu[:, d_expert:]
            parts.append((gpart * _sigmoid(gpart) * upart * wt).astype(BF16))
        y = _dot(parts[0], wd_a_ref[0]) + _dot(parts[1], wd_b_ref[0])
        for c in range(d // LANES):
            ys_ref[pl.ds(c, tile, stride=SUBLANES), :] = y[:, c * LANES:(c + 1) * LANES]

    @pl.when(used_ref[i] == 0)
    def _():
        ys_ref[...] = jnp.zeros(ys_ref.shape, F32)


def _experts(tile_lo, tile_hi, tile_used, xs, wr, br, wgu, wd, *, d):
    n_tiles = tile_lo.shape[0]
    de2 = wgu.shape[2]
    blk = MOE_TILE * SUBLANES
    grid_spec = pltpu.PrefetchScalarGridSpec(
        num_scalar_prefetch=3,
        grid=(n_tiles,),
        in_specs=[pl.BlockSpec((blk, LANES), lambda i, ea, eb, us: (i * us[i], 0)),
                  pl.BlockSpec((d, LANES), lambda i, ea, eb, us: (0, 0)),
                  pl.BlockSpec((1, LANES), lambda i, ea, eb, us: (0, 0)),
                  pl.BlockSpec((1, d, de2), lambda i, ea, eb, us: (ea[i], 0, 0)),
                  pl.BlockSpec((1, d, de2), lambda i, ea, eb, us: (eb[i], 0, 0)),
                  pl.BlockSpec((1, de2 // 2, d), lambda i, ea, eb, us: (ea[i], 0, 0)),
                  pl.BlockSpec((1, de2 // 2, d), lambda i, ea, eb, us: (eb[i], 0, 0))],
        out_specs=pl.BlockSpec((blk, LANES), lambda i, ea, eb, us: (i, 0)))
    return pl.pallas_call(
        functools.partial(_experts_kernel, tile=MOE_TILE, d=d, d_expert=de2 // 2),
        grid_spec=grid_spec,
        out_shape=jax.ShapeDtypeStruct((n_tiles * blk, LANES), F32),
        compiler_params=_cparams(("arbitrary",)),
        name="moe_experts",
    )(tile_lo, tile_hi, tile_used, xs, wr, br, wgu, wgu, wd, wd)


def _gather_start(pos_ref, ys_hbm, buf_ref, sem, n):
    def issue(pair, carry):
        for prio in range(DMA_PRIORITIES):
            r = pair * DMA_PRIORITIES + prio
            pltpu.make_async_copy(_row_tile(ys_hbm, pos_ref[r]), _row_tile(buf_ref, r), sem).start(priority=prio)
        return carry

    lax.fori_loop(0, n // DMA_PRIORITIES, issue, 0, unroll=4)


def _gather_wait(ys_hbm, buf_ref, sem, n):
    pltpu.make_async_copy(ys_hbm.at[pl.ds(0, n * SUBLANES), :], buf_ref, sem).wait()


def _token_major(buf_ref, n, d):
    return jnp.concatenate([buf_ref[pl.ds(c, n, stride=SUBLANES), :] for c in range(d // LANES)], axis=1)


def _combine_kernel(pos_ref, ys_hbm, x_ref, fin_ref, out_ref, buf_ref, sem, *, tc, d, final_norm):
    _gather_start(pos_ref, ys_hbm, buf_ref, sem, tc)
    _gather_wait(ys_hbm, buf_ref, sem, tc)
    out = x_ref[...] + _token_major(buf_ref, tc, d)
    if final_norm:
        out = _rms(out, fin_ref[...])
    out_ref[...] = out


def _combine(pos, ys, x1, fin_g, *, tc, final_norm):
    t, d = x1.shape
    return pl.pallas_call(
        functools.partial(_combine_kernel, tc=tc, d=d, final_norm=final_norm),
        grid=(t // tc,),
        in_specs=[pl.BlockSpec((tc,), lambda i: (i,), memory_space=pltpu.SMEM),
                  pl.BlockSpec(memory_space=pl.ANY),
                  pl.BlockSpec((tc, d), lambda i: (i, 0)),
                  pl.BlockSpec((1, d), lambda i: (0, 0))],
        out_specs=pl.BlockSpec((tc, d), lambda i: (i, 0)),
        out_shape=jax.ShapeDtypeStruct((t, d), F32),
        scratch_shapes=[pltpu.VMEM((tc * SUBLANES, LANES), F32), pltpu.SemaphoreType.DMA(())],
        compiler_params=_cparams(("arbitrary",)),
        name="moe_combine",
    )(pos, ys, x1, fin_g)


def _moe_sparse(x1, rows, code, counts, wr, br, wgu, wd):
    t, d = x1.shape
    n_tiles = t // MOE_TILE + N_CLASSES
    cnt = counts[0, :N_CLASSES].astype(jnp.int32)
    padded = (cnt + MOE_TILE - 1) // MOE_TILE * MOE_TILE
    ends = jnp.cumsum(padded)
    starts = ends - padded
    n_slots = n_tiles * MOE_TILE
    fill = jnp.concatenate([starts + cnt, ends[-1:], padded - cnt, n_slots - ends[-1:]])
    tile_start = jnp.arange(n_tiles, dtype=jnp.int32) * MOE_TILE
    tile_cls = jnp.minimum(jnp.sum((tile_start[:, None] >= ends[None, :]).astype(jnp.int32), axis=1), N_CLASSES - 1)
    lo_ids, hi_ids = _class_experts()
    onehot = (tile_cls[:, None] == jnp.arange(N_CLASSES, dtype=jnp.int32)[None, :]).astype(jnp.int32)
    tile_lo = jnp.sum(onehot * jnp.asarray(lo_ids, jnp.int32)[None, :], axis=1)
    tile_hi = jnp.sum(onehot * jnp.asarray(hi_ids, jnp.int32)[None, :], axis=1)
    tile_used = (tile_start < ends[-1]).astype(jnp.int32)
    xs, pos = _dispatch(code.reshape(t), starts, fill, rows, n_slots, td=1024)
    ys = _experts(tile_lo, tile_hi, tile_used, xs, wr, br, wgu, wd, d=d)
    return ys, pos


def _gm_layer_kernel(pos_ref, pos_next_ref, ys_hbm, x_ref, ng_ref, win_ref, wqm_ref, lng_ref, lnb_ref, wsp_ref,
                     bsp_ref, kv_ref, wo_mix_ref, wo_mem_ref, fg_ref, wr_ref, br_ref,
                     x1_ref, rows_ref, code_ref, cnt_ref, carry_ref, buf_ref, sems, *, tm, n_groups):
    c = CHUNK
    tw = n_groups * HEAD_DIM
    d = x_ref.shape[1]
    i = pl.program_id(0)
    slot = lax.rem(i, 2)

    @pl.when(i == 0)
    def _():
        _gather_start(pos_ref, ys_hbm, buf_ref.at[0], sems.at[0], tm)

    @pl.when(i + 1 < pl.num_programs(0))
    def _():
        _gather_start(pos_next_ref, ys_hbm, buf_ref.at[1 - slot], sems.at[1 - slot], tm)

    _gather_wait(ys_hbm, buf_ref.at[slot], sems.at[slot], tm)
    x = x_ref[...] + _token_major(buf_ref.at[slot], tm, d)
    h = _rms(x, ng_ref[...]).astype(BF16)
    proj = _dot(h, win_ref[...])
    uv = 0.5 * proj * (1.0 + lax.erf(proj * (2.0 ** -0.5)))
    u, v = uv[:, :tw], uv[:, tw:]
    mu = jnp.mean(v, axis=-1, keepdims=True)
    vc = v - mu
    var = jnp.mean(vc * vc, axis=-1, keepdims=True)
    vn = (vc * lax.rsqrt(var + EPS) * lng_ref[...] + lnb_ref[...]).astype(BF16)
    ii = lax.broadcasted_iota(jnp.int32, (c, c), 0)
    jj = lax.broadcasted_iota(jnp.int32, (c, c), 1)
    tril = ii >= jj
    n_chunks = tm // c
    col_parts = []
    for g in range(n_groups):
        wc = jnp.where(tril, wsp_ref[g], jnp.zeros((c, c), BF16))
        cols = slice(g * HEAD_DIM, (g + 1) * HEAD_DIM)
        wide = _dot(wc, jnp.concatenate([vn[n * c:(n + 1) * c, cols] for n in range(n_chunks)], axis=1))
        bias = bsp_ref[g]
        col_parts.append(jnp.concatenate(
            [wide[:, n * HEAD_DIM:(n + 1) * HEAD_DIM] + bias for n in range(n_chunks)], axis=0))
    sgate = jnp.concatenate(col_parts, axis=1)
    mix = (u * sgate).astype(BF16)
    qm = _dot(h, wqm_ref[...])
    kv = kv_ref[0]
    cross = _cross_attn(qm, kv[:, :MEM_WIDTH], kv[:, MEM_WIDTH:]).astype(BF16)
    _mixer_tail(x, mix, cross, wo_mix_ref, wo_mem_ref, fg_ref, wr_ref, br_ref,
                x1_ref, rows_ref, code_ref, cnt_ref, carry_ref)


def _gm_layer(pos, ys, x2d, ng, win, wqm, lng, lnb, wsp, bsp, kv, kv_layer, wo_mix, wo_mem, fg, wr, br, *, tm, seq):
    t, d = x2d.shape
    n_steps = t // tm
    tw = lng.shape[1]
    n_groups = tw // HEAD_DIM
    mlen = kv.shape[1]
    tiles_per_seq = seq // tm
    const = lambda shape: pl.BlockSpec(shape, lambda i: (0,) * len(shape))
    tok = lambda w: pl.BlockSpec((tm, w), lambda i: (i, 0))
    return pl.pallas_call(
        functools.partial(_gm_layer_kernel, tm=tm, n_groups=n_groups),
        grid=(t // tm,),
        in_specs=[pl.BlockSpec((tm,), lambda i: (i,), memory_space=pltpu.SMEM),
                  pl.BlockSpec((tm,), lambda i: (jnp.minimum(i + 1, n_steps - 1),), memory_space=pltpu.SMEM),
                  pl.BlockSpec(memory_space=pl.ANY),
                  tok(d), const((1, d)), const((d, 2 * tw)), const((d, MEM_WIDTH)), const((1, tw)), const((1, tw)),
                  const((n_groups, CHUNK, CHUNK)), const((n_groups, CHUNK, HEAD_DIM)),
                  pl.BlockSpec((1, mlen, 2 * MEM_WIDTH), lambda i: (i // tiles_per_seq, 0, kv_layer)),
                  const((tw, d)), const((MEM_WIDTH, d)), const((1, d)), const((d, LANES)), const((1, LANES))],
        out_specs=_tail_out_specs(tm, d),
        out_shape=_tail_out_shapes(t, d, tm),
        scratch_shapes=[pltpu.VMEM((1, LANES), F32), pltpu.VMEM((2, tm * SUBLANES, LANES), F32),
                        pltpu.SemaphoreType.DMA((2,))],
        compiler_params=_cparams(("arbitrary",)),
        name="gm_layer",
    )(pos, pos, ys, x2d, ng, win, wqm, lng, lnb, wsp, bsp, kv, wo_mix, wo_mem, fg, wr, br)


def _router_params(w_group, b_group, w_router, b_router):
    d = w_group.shape[0]
    pad = LANES - N_EXPERTS - MOE_GROUPS
    wr = jnp.concatenate([w_router, w_group, jnp.zeros((d, pad), F32)], axis=1).astype(BF16)
    br = jnp.concatenate([b_router, b_group, jnp.zeros((pad,), F32)])[None, :]
    return wr, br


def _col16(vec):
    return jnp.concatenate([vec, jnp.zeros((16 - vec.shape[0],), F32)])[:, None]


def kernel(x, mem, mem_norm_g, mix_norm_g, w_out, w_mem_kv, dn_w_in, dn_conv_w, dn_a_log, dn_dt_bias, dn_o_norm_g,
           gm_w_in, gm_ln_g, gm_ln_b, gm_w_spatial, gm_b_spatial, ffn_norm_g, moe_w_group, moe_b_group,
           moe_w_router, moe_b_router, moe_w_gate, moe_w_up, moe_w_down, final_norm_g):
    b, s, d = x.shape
    tw = d - MEM_WIDTH
    n_heads = tw // HEAD_DIM
    t = b * s

    kv = _memkv(mem, mem_norm_g[None, :], jnp.concatenate([w_mem_kv[0], w_mem_kv[1]], axis=1).astype(BF16))

    w_in = dn_w_in[0]
    o1, o2, o3, o4 = 3 * tw, 4 * tw, 4 * tw + n_heads, 4 * tw + 2 * n_heads
    wabt = jnp.concatenate([w_in[:, o2:o4].T, jnp.zeros((16 - 2 * n_heads, d), F32)], axis=0).astype(BF16)
    q, k, v, zg, gb, cross = _dn_front(
        x, mix_norm_g[0][None, :], w_in[:, :o1].astype(BF16), w_in[:, o1:o2].astype(BF16),
        w_in[:, o4:].astype(BF16), wabt, dn_conv_w[0], _col16(dn_a_log[0]), _col16(dn_dt_bias[0]), kv, tm=512)
    u, wp, at, qg, kts, egl = _dn_intra(q, k, v, gb, tb=1024)
    o = _dn_scan(qg, k, u, wp, at, kts, egl, tb=512, n_batch=2 if b % 2 == 0 else 1)
    wr0, br0 = _router_params(moe_w_group[0], moe_b_group[0], moe_w_router[0], moe_b_router[0])
    wo0 = w_out[0].astype(BF16)
    x1, rows0, code0, cnt0 = _dn_out(
        x.reshape(t, d), o.reshape(t, tw), zg.reshape(t, tw), cross.reshape(t, MEM_WIDTH),
        dn_o_norm_g[0][None, :], wo0[:tw], wo0[tw:], ffn_norm_g[0][None, :], wr0, br0, tm=512)
    wgu0 = jnp.concatenate([moe_w_gate[0], moe_w_up[0]], axis=2).astype(BF16)
    ys0, pos0 = _moe_sparse(x1, rows0, code0, cnt0, wr0, br0, wgu0, moe_w_down[0].astype(BF16))

    win1 = gm_w_in[0]
    wr1, br1 = _router_params(moe_w_group[1], moe_b_group[1], moe_w_router[1], moe_b_router[1])
    wo1 = w_out[1].astype(BF16)
    bsp = jnp.broadcast_to(gm_b_spatial[0][:, :, None], (n_heads, CHUNK, HEAD_DIM))
    x3, rows1, code1, cnt1 = _gm_layer(
        pos0, ys0, x1, mix_norm_g[1][None, :], win1[:, :2 * tw].astype(BF16), win1[:, 2 * tw:].astype(BF16),
        gm_ln_g[0][None, :], gm_ln_b[0][None, :], gm_w_spatial[0].astype(BF16), bsp, kv, 1,
        wo1[:tw], wo1[tw:], ffn_norm_g[1][None, :], wr1, br1, tm=512, seq=s)
    wgu1 = jnp.concatenate([moe_w_gate[1], moe_w_up[1]], axis=2).astype(BF16)
    ys1, pos1 = _moe_sparse(x3, rows1, code1, cnt1, wr1, br1, wgu1, moe_w_down[1].astype(BF16))
    out = _combine(pos1, ys1, x3, final_norm_g[None, :], tc=512, final_norm=True)
    return out.reshape(b, s, d)
```

```python
import functools

import jax
import jax.numpy as jnp
from jax import lax
from jax.experimental import pallas as pl
from jax.experimental.pallas import tpu as pltpu

F32 = jnp.float32
BF16 = jnp.bfloat16
EPS = 1e-6

MEM_HEADS = 4
MEM_HEAD_DIM = 64
MEM_WIDTH = MEM_HEADS * MEM_HEAD_DIM
HEAD_DIM = 128
CONV_K = 4
CHUNK = 128
MOE_GROUPS = 4
EXPERTS_PER_GROUP = 4
N_EXPERTS = MOE_GROUPS * EXPERTS_PER_GROUP
PAIRS_PER_GROUP = EXPERTS_PER_GROUP * (EXPERTS_PER_GROUP - 1) // 2
N_CLASSES = MOE_GROUPS * PAIRS_PER_GROUP
MOE_TILE = 256
LANES = 128
SUBLANES = 8
RANK_BITS = 16
DMA_PRIORITIES = 2
N_SUB = 2
VMEM_LIMIT = 56 * 1024 * 1024

NT_DIMS = (((1,), (1,)), ((), ()))
NEG_LOG2_E = -1.4426950408889634


def _dot(a, b):
    return jnp.dot(a, b, preferred_element_type=F32)


def _dot_nt(a, b):
    return lax.dot_general(a, b, NT_DIMS, preferred_element_type=F32)


def _rms(x, g):
    return x * lax.rsqrt(jnp.mean(x * x, axis=-1, keepdims=True) + EPS) * g


def _sigmoid(x):
    return 1.0 / (1.0 + jnp.exp2(x * NEG_LOG2_E))


def _softplus(x):
    return jnp.maximum(x, 0.0) + jnp.log1p(jnp.exp(-jnp.abs(x)))


def _split3(x):
    hi = x.astype(BF16)
    r = x - hi.astype(F32)
    mid = r.astype(BF16)
    lo = (r - mid.astype(F32)).astype(BF16)
    return hi, mid, lo


def _cparams(sem):
    return pltpu.CompilerParams(dimension_semantics=sem, vmem_limit_bytes=VMEM_LIMIT)


def _memkv_kernel(mem_ref, g_ref, w_ref, kv_ref):
    mn = _rms(mem_ref[0], g_ref[...]).astype(BF16)
    kv_ref[0] = _dot(mn, w_ref[...]).astype(BF16)


def _memkv(mem, g, w_all):
    b, m, d = mem.shape
    n = w_all.shape[1]
    return pl.pallas_call(
        _memkv_kernel,
        grid=(b,),
        in_specs=[pl.BlockSpec((1, m, d), lambda i: (i, 0, 0)),
                  pl.BlockSpec((1, d), lambda i: (0, 0)),
                  pl.BlockSpec((d, n), lambda i: (0, 0))],
        out_specs=pl.BlockSpec((1, m, n), lambda i: (i, 0, 0)),
        out_shape=jax.ShapeDtypeStruct((b, m, n), BF16),
        compiler_params=_cparams(("parallel",)),
        name="memkv",
    )(mem, g, w_all)


def _cross_attn(qm, kmem, vmem):
    lane = lax.broadcasted_iota(jnp.int32, (1, MEM_WIDTH), 1)
    out = jnp.zeros(qm.shape, F32)
    for h in range(MEM_HEADS):
        msk = (lane >= h * MEM_HEAD_DIM) & (lane < (h + 1) * MEM_HEAD_DIM)
        qh = jnp.where(msk, qm, 0.0).astype(BF16)
        s = _dot_nt(qh, kmem) * (MEM_HEAD_DIM ** -0.5)
        p = jnp.exp(s - jnp.max(s, axis=-1, keepdims=True))
        inv = 1.0 / jnp.sum(p, axis=-1, keepdims=True)
        vh = jnp.where(msk, vmem, jnp.zeros_like(vmem))
        out = out + _dot(p.astype(BF16), vh) * inv
    return out


def _router_class(logits):
    lane = lax.broadcasted_iota(jnp.int32, logits.shape, 1).astype(F32)
    neg = jnp.float32(-jnp.inf)
    big = jnp.float32(1 << 20)
    first = lambda hit: jnp.min(jnp.where(hit, lane, big), axis=-1, keepdims=True)
    is_g = (lane >= N_EXPERTS) & (lane < N_EXPERTS + MOE_GROUPS)
    gl = jnp.where(is_g, logits, neg)
    g_idx = first(gl == jnp.max(gl, axis=-1, keepdims=True)) - N_EXPERTS
    lo = g_idx * EXPERTS_PER_GROUP
    in_grp = (lane >= lo) & (lane < lo + EXPERTS_PER_GROUP)
    el = jnp.where(in_grp, logits, neg)
    i1 = first(el == jnp.max(el, axis=-1, keepdims=True))
    el2 = jnp.where(lane == i1, neg, el)
    i2 = first(el2 == jnp.max(el2, axis=-1, keepdims=True))
    e_lo = jnp.minimum(i1, i2) - lo
    e_hi = jnp.maximum(i1, i2) - lo
    pair_off = jnp.where(e_lo == 0.0, 0.0, jnp.where(e_lo == 1.0, 3.0, 5.0))
    return (g_idx * PAIRS_PER_GROUP + pair_off + e_hi - e_lo - 1.0).astype(jnp.int32)


def _pair_gates(logits, e_lo, e_hi):
    lane = lax.broadcasted_iota(jnp.int32, logits.shape, 1)
    pick = lambda idx: jnp.sum(jnp.where(lane == idx, logits, 0.0), axis=-1, keepdims=True)
    l_lo, l_hi = pick(e_lo), pick(e_hi)
    g_sel = pick(N_EXPERTS + e_lo // EXPERTS_PER_GROUP)
    is_g = (lane >= N_EXPERTS) & (lane < N_EXPERTS + MOE_GROUPS)
    p_grp = 1.0 / jnp.sum(jnp.where(is_g, jnp.exp(logits - g_sel), 0.0), axis=-1, keepdims=True)
    return p_grp / (1.0 + jnp.exp(l_hi - l_lo)), p_grp / (1.0 + jnp.exp(l_lo - l_hi))


def _class_experts():
    lo_ids, hi_ids = [], []
    for g in range(MOE_GROUPS):
        for a in range(EXPERTS_PER_GROUP):
            for b in range(a + 1, EXPERTS_PER_GROUP):
                lo_ids.append(g * EXPERTS_PER_GROUP + a)
                hi_ids.append(g * EXPERTS_PER_GROUP + b)
    return lo_ids, hi_ids


def _dn_front_kernel(x_ref, ng_ref, wqkv_ref, wz_ref, wqm_ref, wabt_ref, convw_ref, alog_ref, dtb_ref, kv_ref,
                     q_ref, k_ref, v_ref, zg_ref, gb_ref, cross_ref, cbuf_ref, *, tm, n_heads):
    j = pl.program_id(1)
    tw = n_heads * HEAD_DIM
    h = _rms(x_ref[0], ng_ref[...]).astype(BF16)

    @pl.when(j == 0)
    def _():
        cbuf_ref[0:8, :] = jnp.zeros((8, 3 * tw), F32)

    gw = 2 * HEAD_DIM
    out_refs = (q_ref, k_ref, v_ref)
    scales = (HEAD_DIM ** -0.5, 1.0, None)
    for grp in range(3 * tw // gw):
        cols = slice(grp * gw, (grp + 1) * gw)
        pre = _dot(h, wqkv_ref[:, cols])
        cbuf_ref[8:8 + tm, cols] = pre
        w = convw_ref[:, cols]
        acc = w[CONV_K - 1:CONV_K, :] * pre
        for kk in range(CONV_K - 1):
            acc = acc + w[kk:kk + 1, :] * cbuf_ref[pl.ds(8 - (CONV_K - 1) + kk, tm), cols]
        cbuf_ref[0:8, cols] = cbuf_ref[tm:tm + 8, cols]
        act = acc * _sigmoid(acc)
        which, first = divmod(grp * gw, tw)
        for half in range(gw // HEAD_DIM):
            a = act[:, half * HEAD_DIM:(half + 1) * HEAD_DIM]
            if scales[which] is not None:
                a = a * (lax.rsqrt(jnp.sum(a * a, axis=-1, keepdims=True) + EPS) * scales[which])
            dst = first + half * HEAD_DIM
            out_refs[which][0, :, dst:dst + HEAD_DIM] = a.astype(BF16)

    z = _dot(h, wz_ref[...])
    zg_ref[0] = (z * _sigmoid(z)).astype(BF16)

    abt = _dot_nt(wabt_ref[...], h)
    g_t = -jnp.exp(alog_ref[...]) * _softplus(abt + dtb_ref[...])
    b_t = _sigmoid(abt)
    for hd in range(n_heads):
        gb_ref[0, hd, 0:1, :] = g_t[hd:hd + 1, :]
        gb_ref[0, hd, 1:2, :] = b_t[n_heads + hd:n_heads + hd + 1, :]

    qm = _dot(h, wqm_ref[...])
    kv = kv_ref[0]
    cross_ref[0] = _cross_attn(qm, kv[:, :MEM_WIDTH], kv[:, MEM_WIDTH:]).astype(BF16)


def _dn_front(x, ng, wqkv, wz, wqm, wabt, convw, alog, dtb, kv, *, tm):
    b, s, d = x.shape
    tw = wz.shape[1]
    n_heads = tw // HEAD_DIM
    mlen = kv.shape[1]
    const = lambda shape: pl.BlockSpec(shape, lambda i, j: (0,) * len(shape))
    tok = lambda w: pl.BlockSpec((1, tm, w), lambda i, j: (i, j, 0))
    return pl.pallas_call(
        functools.partial(_dn_front_kernel, tm=tm, n_heads=n_heads),
        grid=(b, s // tm),
        in_specs=[tok(d), const((1, d)), const((d, 3 * tw)), const((d, tw)), const((d, MEM_WIDTH)),
                  const((16, d)), const((CONV_K, 3 * tw)), const((16, 1)), const((16, 1)),
                  pl.BlockSpec((1, mlen, 2 * MEM_WIDTH), lambda i, j: (i, 0, 0))],
        out_specs=[tok(tw), tok(tw), tok(tw), tok(tw),
                   pl.BlockSpec((1, n_heads, 2, tm), lambda i, j: (i, 0, 0, j)),
                   tok(MEM_WIDTH)],
        out_shape=[jax.ShapeDtypeStruct((b, s, tw), BF16)] * 4
                  + [jax.ShapeDtypeStruct((b, n_heads, 2, s), F32),
                     jax.ShapeDtypeStruct((b, s, MEM_WIDTH), BF16)],
        scratch_shapes=[pltpu.VMEM((tm + 8, 3 * tw), F32)],
        compiler_params=_cparams(("parallel", "arbitrary")),
        name="dn_front",
    )(x, ng, wqkv, wz, wqm, wabt, convw, alog, dtb, kv)


def _cat_lanes(a, b):
    return jnp.concatenate([a, b], axis=1)


def _block_diag(a, b):
    z = jnp.zeros(a.shape, a.dtype)
    return jnp.concatenate([_cat_lanes(a, z), _cat_lanes(z, b)], axis=0)


def _pair_dot(a0, a1, b0, b1):
    w = a0.shape[1]
    r = _dot(_cat_lanes(a0, a1), _block_diag(b0, b1))
    return r[:, :w], r[:, w:]


def _tri_inverse_pairs(a_pairs, ii, jj):
    c = CHUNK
    eye = (ii == jj).astype(F32)
    blk = (ii >> 4) == (jj >> 4)
    n0 = [[jnp.where(blk, -a, 0.0) for a in pair] for pair in a_pairs]
    x = [[eye + n for n in pair] for pair in n0]
    nb = [[n.astype(BF16) for n in pair] for pair in n0]
    nb = [[r.astype(BF16) for r in _pair_dot(p[0], p[1], p[0], p[1])] for p in nb]
    for _ in range(2):
        r = [[_dot(n, _cat_lanes(n, xi.astype(BF16))) for n, xi in zip(pn, px)] for pn, px in zip(nb, x)]
        nb = [[ri[:, :c].astype(BF16) for ri in pr] for pr in r]
        x = [[xi + ri[:, c:] for xi, ri in zip(px, pr)] for px, pr in zip(x, r)]
    fin = [_pair_dot(pn[0], pn[1], px[0].astype(BF16), px[1].astype(BF16)) for pn, px in zip(nb, x)]
    x = [[xi + fi for xi, fi in zip(px, pf)] for px, pf in zip(x, fin)]
    shift = 4
    while (1 << shift) < c:
        inner = (ii >> shift) == (jj >> shift)
        outer = (ii >> (shift + 1)) == (jj >> (shift + 1))
        sel = outer & jnp.logical_not(inner)
        xb = [[xi.astype(BF16) for xi in px] for px in x]
        lo = [[jnp.where(sel, a, 0.0).astype(BF16) for a in pa] for pa in a_pairs]
        lx = [[r.astype(BF16) for r in _pair_dot(pl_[0], pl_[1], pb[0], pb[1])] for pl_, pb in zip(lo, xb)]
        cor = [_pair_dot(pb[0], pb[1], pl_[0], pl_[1]) for pb, pl_ in zip(xb, lx)]
        x = [[xi - ci for xi, ci in zip(px, pc)] for px, pc in zip(x, cor)]
        shift += 1
    return x


def _dn_intra_kernel(q_ref, k_ref, v_ref, gb_ref, u_ref, wp_ref, at_ref, qg_ref, kts_ref, egl_ref, *, n_chunks):
    c = CHUNK
    hw = HEAD_DIM
    ii = lax.broadcasted_iota(jnp.int32, (c, c), 0)
    jj = lax.broadcasted_iota(jnp.int32, (c, c), 1)
    tril = ii >= jj
    strict = ii > jj
    lane2 = lax.broadcasted_iota(jnp.int32, (c, 2 * hw), 1)
    upper_incl = (ii <= jj).astype(BF16)
    chunks = range(n_chunks)
    heads = range(2)
    rows = [slice(n * c, (n + 1) * c) for n in chunks]
    hcol = [slice(hd * hw, (hd + 1) * hw) for hd in heads]
    kp = [k_ref[0, r, :] for r in rows]
    zero = jnp.zeros((c, 2 * hw), BF16)
    kq = [_dot_nt(jnp.concatenate([kp[n], q_ref[0, rows[n], :]], axis=0),
                  jnp.concatenate([jnp.where(lane2 < hw, kp[n], zero), jnp.where(lane2 >= hw, kp[n], zero)], axis=0))
          for n in chunks]
    g_rows = [gb_ref[0, hd, 0:1, rows[n]] for n in chunks for hd in heads]
    hi, mid, lo = _split3(jnp.concatenate(g_rows, axis=0))
    n_rows = len(g_rows)
    cum = _dot(jnp.concatenate([hi, mid, lo], axis=0), upper_incl)
    gc_all = cum[:n_rows] + cum[n_rows:2 * n_rows] + cum[2 * n_rows:]
    a_pairs, b_rs, e_rs = [], [], []
    for n in chunks:
        a_pair, b_pair, e_pair = [], [], []
        for hd in heads:
            g_row = g_rows[n * 2 + hd]
            b_row = gb_ref[0, hd, 1:2, rows[n]]
            gc_row = gc_all[n * 2 + hd:n * 2 + hd + 1, :]
            gl = jnp.sum(g_row, axis=-1, keepdims=True)
            g_r = jnp.broadcast_to(gc_row, (c, c))
            g_c = g_r.T
            b_r = jnp.broadcast_to(b_row, (c, c))
            decay = jnp.where(tril, jnp.exp(g_c - g_r), 0.0)
            kqh = kq[n][:, hd * c:(hd + 1) * c]
            a_pair.append(jnp.where(strict, kqh[:c] * decay, 0.0) * b_r.T)
            at_ref[0, rows[n], hcol[hd]] = jnp.where(tril, kqh[c:] * decay, 0.0).astype(BF16)
            qg_ref[0, rows[n], hcol[hd]] = (q_ref[0, rows[n], hcol[hd]].astype(F32) * jnp.exp(g_c)).astype(BF16)
            kh = kp[n][:, hcol[hd]].astype(F32)
            kts_ref[0, hcol[hd], rows[n]] = (kh.T * jnp.exp(gl - gc_row)).astype(BF16)
            egl_ref[0, n, :, hcol[hd]] = jnp.broadcast_to(jnp.exp(gl), (1, hw))
            b_pair.append(b_r)
            e_pair.append(jnp.exp(g_r))
        a_pairs.append(a_pair)
        b_rs.append(b_pair)
        e_rs.append(e_pair)
    t = _tri_inverse_pairs(a_pairs, ii, jj)
    tb = [[t[n][hd] * b_rs[n][hd] for hd in heads] for n in chunks]
    u = [_pair_dot(tb[n][0].astype(BF16), tb[n][1].astype(BF16),
                   v_ref[0, rows[n], hcol[0]], v_ref[0, rows[n], hcol[1]]) for n in chunks]
    for n in chunks:
        for hd in heads:
            u_ref[0, rows[n], hcol[hd]] = u[n][hd].astype(BF16)
            wp_ref[0, rows[n], hcol[hd]] = (tb[n][hd] * e_rs[n][hd]).astype(BF16)


def _dn_intra(q, k, v, gb, *, tb):
    b, s, tw = q.shape
    n_chunks = tb // CHUNK
    pw = 2 * HEAD_DIM
    tok = pl.BlockSpec((1, tb, pw), lambda i, h, j: (i, j, h))
    return pl.pallas_call(
        functools.partial(_dn_intra_kernel, n_chunks=n_chunks),
        grid=(b, tw // pw, s // tb),
        in_specs=[tok, tok, tok, pl.BlockSpec((1, 2, 2, tb), lambda i, h, j: (i, h, 0, j))],
        out_specs=[tok, tok, tok, tok,
                   pl.BlockSpec((1, pw, tb), lambda i, h, j: (i, h, j)),
                   pl.BlockSpec((1, n_chunks, 1, pw), lambda i, h, j: (i, j, 0, h))],
        out_shape=[jax.ShapeDtypeStruct((b, s, tw), BF16)] * 4
                  + [jax.ShapeDtypeStruct((b, tw, s), BF16),
                     jax.ShapeDtypeStruct((b, s // CHUNK, 1, tw), F32)],
        compiler_params=_cparams(("parallel", "parallel", "parallel")),
        name="dn_intra",
    )(q, k, v, gb)


def _dn_scan_kernel(qg_ref, k_ref, u_ref, wp_ref, at_ref, kts_ref, egl_ref, o_ref, s_ref,
                    *, n_chunks, n_pairs, n_batch):
    c = CHUNK
    pw = 2 * HEAD_DIM

    @pl.when(pl.program_id(1) == 0)
    def _():
        s_ref[...] = jnp.zeros(s_ref.shape, F32)

    items = [(bi, p) for bi in range(n_batch) for p in range(n_pairs)]
    cols = [slice(p * pw, (p + 1) * pw) for p in range(n_pairs)]
    lane = lax.broadcasted_iota(jnp.int32, (c, pw), 1)
    ri = lax.broadcasted_iota(jnp.int32, (pw, pw), 0)
    ci = lax.broadcasted_iota(jnp.int32, (pw, pw), 1)
    same_head = (ri >= HEAD_DIM) == (ci >= HEAD_DIM)
    zero = jnp.zeros((c, pw), BF16)

    def stacked_diag(m):
        return jnp.concatenate([jnp.where(lane < HEAD_DIM, m, zero), jnp.where(lane >= HEAD_DIM, m, zero)], axis=0)

    states = [s_ref[bi * n_pairs + p] for bi, p in items]
    for n in range(n_chunks):
        rows = slice(n * c, (n + 1) * c)
        kqs = [_dot(jnp.concatenate([k_ref[bi, rows, cols[p]], qg_ref[bi, rows, cols[p]]], axis=0), st.astype(BF16))
               for (bi, p), st in zip(items, states)]
        wks = [_dot(wp_ref[bi, rows, cols[p]], stacked_diag(kq[:c].astype(BF16))) for (bi, p), kq in zip(items, kqs)]
        xb = [(u_ref[bi, rows, cols[p]].astype(F32) - wk).astype(BF16) for (bi, p), wk in zip(items, wks)]
        upd = [_dot(kts_ref[bi, cols[p], rows], x) for (bi, p), x in zip(items, xb)]
        states = [st * egl_ref[bi, n, :, cols[p]] + jnp.where(same_head, up, 0.0)
                  for (bi, p), st, up in zip(items, states, upd)]
        intra = [_dot(at_ref[bi, rows, cols[p]], stacked_diag(x)) for (bi, p), x in zip(items, xb)]
        for (bi, p), it, kq in zip(items, intra, kqs):
            o_ref[bi, rows, cols[p]] = (it + kq[c:]).astype(BF16)
    for (bi, p), st in zip(items, states):
        s_ref[bi * n_pairs + p] = st


def _dn_scan(qg, k, u, wp, at, kts, egl, *, tb, n_batch):
    b, s, tw = qg.shape
    n_chunks = tb // CHUNK
    pw = 2 * HEAD_DIM
    tok = pl.BlockSpec((n_batch, tb, tw), lambda i, j: (i, j, 0))
    return pl.pallas_call(
        functools.partial(_dn_scan_kernel, n_chunks=n_chunks, n_pairs=tw // pw, n_batch=n_batch),
        grid=(b // n_batch, s // tb),
        in_specs=[tok, tok, tok, tok, tok,
                  pl.BlockSpec((n_batch, tw, tb), lambda i, j: (i, 0, j)),
                  pl.BlockSpec((n_batch, n_chunks, 1, tw), lambda i, j: (i, j, 0, 0))],
        out_specs=tok,
        out_shape=jax.ShapeDtypeStruct((b, s, tw), BF16),
        scratch_shapes=[pltpu.VMEM((n_batch * (tw // pw), pw, pw), F32)],
        compiler_params=_cparams(("parallel", "arbitrary")),
        name="dn_scan",
    )(qg, k, u, wp, at, kts, egl)


def _mixer_tail(x, mix_b, cross_b, wo_mix_ref, wo_mem_ref, fg_ref, wr_ref, br_ref,
                x1_ref, rows_ref, code_ref, cnt_ref, carry_ref):
    subs = range(len(x))
    sm, d = x[0].shape
    assert d == SUBLANES * LANES
    y = [_dot(mix_b[t], wo_mix_ref[...]) + _dot(cross_b[t], wo_mem_ref[...]) for t in subs]
    x1 = [x[t] + y[t] for t in subs]
    for t in subs:
        x1_ref[t * sm:(t + 1) * sm, :] = x1[t]
    h2 = [_rms(x1[t], fg_ref[...]) for t in subs]
    logits = [_dot(h2[t].astype(BF16), wr_ref[...]) + br_ref[...] for t in subs]
    cls = [_router_class(logits[t]) for t in subs]
    for t in subs:
        for c in range(SUBLANES):
            rows_ref[pl.ds(t * sm * SUBLANES + c, sm, stride=SUBLANES), :] = h2[t][:, c * LANES:(c + 1) * LANES]

    @pl.when(pl.program_id(0) == 0)
    def _():
        carry_ref[...] = jnp.zeros(carry_ref.shape, F32)

    lane = lax.broadcasted_iota(jnp.int32, (sm, LANES), 1)
    ii = lax.broadcasted_iota(jnp.int32, (sm, sm), 0)
    jj = lax.broadcasted_iota(jnp.int32, (sm, sm), 1)
    earlier = (ii > jj).astype(BF16)
    onehot = [lane == cls[t] for t in subs]
    within = [_dot(earlier, onehot[t].astype(BF16)) for t in subs]
    carry = carry_ref[...]
    l8 = lax.broadcasted_iota(jnp.int32, (SUBLANES, LANES), 1)
    s8 = lax.broadcasted_iota(jnp.int32, (SUBLANES, LANES), 0)
    for t in subs:
        rank = jnp.sum(jnp.where(onehot[t], within[t] + carry, 0.0), axis=-1, keepdims=True).astype(jnp.int32)
        carry = carry + jnp.sum(onehot[t].astype(F32), axis=0, keepdims=True)
        cols = jnp.where(lane == 0, cls[t], jnp.where(lane == 1, rank >> 8, jnp.where(lane == 2, rank & 255, 0)))
        picked = _dot_nt((l8 == s8).astype(BF16), cols.astype(F32).astype(BF16)).astype(jnp.int32)
        code_ref[0, :, t * sm:(t + 1) * sm] = (picked[0:1] << RANK_BITS) | (picked[1:2] << 8) | picked[2:3]
    carry_ref[...] = carry
    cnt_ref[...] = carry


def _tail_out_specs(tm, d):
    return [pl.BlockSpec((tm, d), lambda i: (i, 0)),
            pl.BlockSpec((tm * SUBLANES, LANES), lambda i: (i, 0)),
            pl.BlockSpec((1, 1, tm), lambda i: (i, 0, 0)),
            pl.BlockSpec((1, LANES), lambda i: (0, 0))]


def _tail_out_shapes(t, d, tm):
    return [jax.ShapeDtypeStruct((t, d), F32), jax.ShapeDtypeStruct((t * SUBLANES, LANES), F32),
            jax.ShapeDtypeStruct((t // tm, 1, tm), jnp.int32), jax.ShapeDtypeStruct((1, LANES), F32)]


def _dn_out_kernel(x_ref, o_ref, zg_ref, cross_ref, og_ref, wo_mix_ref, wo_mem_ref, fg_ref, wr_ref, br_ref,
                   x1_ref, rows_ref, code_ref, cnt_ref, carry_ref, *, n_heads):
    og = og_ref[...]
    sm = x_ref.shape[0] // N_SUB
    xs, mixes, crosses = [], [], []
    for t in range(N_SUB):
        rows = slice(t * sm, (t + 1) * sm)
        parts = []
        for hd in range(n_heads):
            cols = slice(hd * HEAD_DIM, (hd + 1) * HEAD_DIM)
            oh = o_ref[rows, cols].astype(F32)
            on = oh * lax.rsqrt(jnp.mean(oh * oh, axis=-1, keepdims=True) + EPS) * og
            parts.append((on * zg_ref[rows, cols].astype(F32)).astype(BF16))
        mixes.append(jnp.concatenate(parts, axis=1))
        xs.append(x_ref[rows, :])
        crosses.append(cross_ref[rows, :])
    _mixer_tail(xs, mixes, crosses, wo_mix_ref, wo_mem_ref, fg_ref, wr_ref, br_ref,
                x1_ref, rows_ref, code_ref, cnt_ref, carry_ref)


def _dn_out(x2d, o2d, zg2d, cross2d, og, wo_mix, wo_mem, fg, wr, br, *, tm):
    t, d = x2d.shape
    tw = o2d.shape[1]
    n_heads = tw // HEAD_DIM
    const = lambda shape: pl.BlockSpec(shape, lambda i: (0,) * len(shape))
    tok = lambda w: pl.BlockSpec((tm, w), lambda i: (i, 0))
    return pl.pallas_call(
        functools.partial(_dn_out_kernel, n_heads=n_heads),
        grid=(t // tm,),
        in_specs=[tok(d), tok(tw), tok(tw), tok(MEM_WIDTH), const((1, HEAD_DIM)), const((tw, d)),
                  const((MEM_WIDTH, d)), const((1, d)), const((d, LANES)), const((1, LANES))],
        out_specs=_tail_out_specs(tm, d),
        out_shape=_tail_out_shapes(t, d, tm),
        scratch_shapes=[pltpu.VMEM((1, LANES), F32)],
        compiler_params=_cparams(("arbitrary",)),
        name="dn_out",
    )(x2d, o2d, zg2d, cross2d, og, wo_mix, wo_mem, fg, wr, br)


def _row_tile(ref, r):
    return ref.at[pl.ds(pl.multiple_of(r * SUBLANES, SUBLANES), SUBLANES), :]


def _dispatch_kernel(code_ref, starts_ref, fill_ref, rows_ref, xs_hbm, pos_ref, zero_ref, sem, pad_sem, *, td):
    rank_mask = (1 << RANK_BITS) - 1

    @pl.when(pl.program_id(0) == 0)
    def _():
        zero_ref[...] = jnp.zeros(zero_ref.shape, F32)
        n_fill = fill_ref.shape[0] // 2

        def slots(ref, first, n):
            return ref.at[pl.ds(pl.multiple_of(first * SUBLANES, SUBLANES), n * SUBLANES), :]

        def walk(act):
            for c in range(n_fill):
                first, count = fill_ref[c], fill_ref[n_fill + c]
                n_whole = count // MOE_TILE

                def whole(kk, carry, first=first):
                    act(pltpu.make_async_copy(zero_ref, slots(xs_hbm, first + kk * MOE_TILE, MOE_TILE), pad_sem))
                    return carry

                lax.fori_loop(0, n_whole, whole, 0)
                off = first + n_whole * MOE_TILE
                rem = count - n_whole * MOE_TILE
                bit = MOE_TILE // 2
                while bit:
                    @pl.when((rem & bit) != 0)
                    def _(off=off, bit=bit):
                        act(pltpu.make_async_copy(slots(zero_ref, 0, bit), slots(xs_hbm, off, bit), pad_sem))

                    off = off + (rem & bit)
                    bit //= 2

        walk(lambda cp: cp.start())
        walk(lambda cp: cp.wait())

    def issue(pair, carry):
        for prio in range(DMA_PRIORITIES):
            r = pair * DMA_PRIORITIES + prio
            code = code_ref[r]
            slot = starts_ref[code >> RANK_BITS] + (code & rank_mask)
            pos_ref[r] = slot
            pltpu.make_async_copy(_row_tile(rows_ref, r), _row_tile(xs_hbm, slot), sem).start(priority=prio)
        return carry

    lax.fori_loop(0, td // DMA_PRIORITIES, issue, 0, unroll=4)
    pltpu.make_async_copy(rows_ref, xs_hbm.at[pl.ds(0, td * SUBLANES), :], sem).wait()


def _dispatch(code, starts, fill, rows, n_slots, *, td):
    t = code.shape[0]
    smem_whole = lambda n: pl.BlockSpec((n,), lambda i: (0,), memory_space=pltpu.SMEM)
    return pl.pallas_call(
        functools.partial(_dispatch_kernel, td=td),
        grid=(t // td,),
        in_specs=[pl.BlockSpec((td,), lambda i: (i,), memory_space=pltpu.SMEM),
                  smem_whole(starts.shape[0]), smem_whole(fill.shape[0]),
                  pl.BlockSpec((td * SUBLANES, LANES), lambda i: (i, 0))],
        out_specs=[pl.BlockSpec(memory_space=pl.ANY),
                   pl.BlockSpec((td,), lambda i: (i,), memory_space=pltpu.SMEM)],
        out_shape=[jax.ShapeDtypeStruct((n_slots * SUBLANES, LANES), F32),
                   jax.ShapeDtypeStruct((t,), jnp.int32)],
        scratch_shapes=[pltpu.VMEM((MOE_TILE * SUBLANES, LANES), F32), pltpu.SemaphoreType.DMA(()),
                        pltpu.SemaphoreType.DMA(())],
        compiler_params=_cparams(("arbitrary",)),
        name="moe_dispatch",
    )(code, starts, fill, rows)


def _experts_kernel(ea_ref, eb_ref, used_ref, xs_ref, wr_ref, br_ref, *refs, tile, d, d_expert, n_sub):
    ys_ref = refs[-1]
    i = pl.program_id(0)
    subs = range(n_sub)
    tid = [i * n_sub + t for t in subs]
    base = [t * tile * SUBLANES for t in subs]
    n_used = used_ref[tid[0]]
    for t in subs[1:]:
        n_used = n_used + used_ref[tid[t]]

    @pl.when(n_used > 0)
    def _():
        h = [jnp.concatenate([xs_ref[pl.ds(base[t] + c, tile, stride=SUBLANES), :] for c in range(d // LANES)],
                             axis=1).astype(BF16) for t in subs]
        gates = [_pair_gates(_dot(h[t], wr_ref[...]) + br_ref[...], ea_ref[tid[t]], eb_ref[tid[t]]) for t in subs]
        gu = [[_dot(h[t], refs[4 * t + e][0]) for e in range(2)] for t in subs]
        parts = [[(g[:, :d_expert] * _sigmoid(g[:, :d_expert]) * g[:, d_expert:] * gates[t][e]).astype(BF16)
                  for e, g in enumerate(gu[t])] for t in subs]
        y = [_dot(parts[t][0], refs[4 * t + 2][0]) + _dot(parts[t][1], refs[4 * t + 3][0]) for t in subs]
        for t in subs:
            for c in range(d // LANES):
                ys_ref[pl.ds(base[t] + c, tile, stride=SUBLANES), :] = y[t][:, c * LANES:(c + 1) * LANES]

    @pl.when(n_used == 0)
    def _():
        ys_ref[...] = jnp.zeros(ys_ref.shape, F32)


def _experts(tile_lo, tile_hi, tile_used, xs, wr, br, wgu, wd, *, d, n_sub=2):
    n_tiles = tile_lo.shape[0]
    de2 = wgu.shape[2]
    blk = n_sub * MOE_TILE * SUBLANES
    const = lambda shape: pl.BlockSpec(shape, lambda i, ea, eb, us: (0,) * len(shape))
    weight_specs, weights = [], []
    for t in range(n_sub):
        weight_specs += [pl.BlockSpec((1, d, de2), lambda i, ea, eb, us, t=t: (ea[i * n_sub + t], 0, 0)),
                         pl.BlockSpec((1, d, de2), lambda i, ea, eb, us, t=t: (eb[i * n_sub + t], 0, 0)),
                         pl.BlockSpec((1, de2 // 2, d), lambda i, ea, eb, us, t=t: (ea[i * n_sub + t], 0, 0)),
                         pl.BlockSpec((1, de2 // 2, d), lambda i, ea, eb, us, t=t: (eb[i * n_sub + t], 0, 0))]
        weights += [wgu, wgu, wd, wd]
    grid_spec = pltpu.PrefetchScalarGridSpec(
        num_scalar_prefetch=3,
        grid=(n_tiles // n_sub,),
        in_specs=[pl.BlockSpec((blk, LANES), lambda i, ea, eb, us: (i, 0)), const((d, LANES)), const((1, LANES))]
                 + weight_specs,
        out_specs=pl.BlockSpec((blk, LANES), lambda i, ea, eb, us: (i, 0)))
    return pl.pallas_call(
        functools.partial(_experts_kernel, tile=MOE_TILE, d=d, d_expert=de2 // 2, n_sub=n_sub),
        grid_spec=grid_spec,
        out_shape=jax.ShapeDtypeStruct((n_tiles * MOE_TILE * SUBLANES, LANES), F32),
        compiler_params=_cparams(("arbitrary",)),
        name="moe_experts",
    )(tile_lo, tile_hi, tile_used, xs, wr, br, *weights)


def _gather_start(pos_ref, ys_hbm, buf_ref, sem, n):
    def issue(pair, carry):
        for prio in range(DMA_PRIORITIES):
            r = pair * DMA_PRIORITIES + prio
            pltpu.make_async_copy(_row_tile(ys_hbm, pos_ref[r]), _row_tile(buf_ref, r), sem).start(priority=prio)
        return carry

    lax.fori_loop(0, n // DMA_PRIORITIES, issue, 0, unroll=4)


def _gather_wait(ys_hbm, buf_ref, sem, n):
    pltpu.make_async_copy(ys_hbm.at[pl.ds(0, n * SUBLANES), :], buf_ref, sem).wait()


def _token_major(buf_ref, n, d):
    return jnp.concatenate([buf_ref[pl.ds(c, n, stride=SUBLANES), :] for c in range(d // LANES)], axis=1)


def _combine_kernel(pos_ref, ys_hbm, x_ref, fin_ref, out_ref, buf_ref, sem, *, tc, d, final_norm):
    _gather_start(pos_ref, ys_hbm, buf_ref, sem, tc)
    _gather_wait(ys_hbm, buf_ref, sem, tc)
    out = x_ref[...] + _token_major(buf_ref, tc, d)
    if final_norm:
        out = _rms(out, fin_ref[...])
    out_ref[...] = out


def _combine(pos, ys, x1, fin_g, *, tc, final_norm):
    t, d = x1.shape
    return pl.pallas_call(
        functools.partial(_combine_kernel, tc=tc, d=d, final_norm=final_norm),
        grid=(t // tc,),
        in_specs=[pl.BlockSpec((tc,), lambda i: (i,), memory_space=pltpu.SMEM),
                  pl.BlockSpec(memory_space=pl.ANY),
                  pl.BlockSpec((tc, d), lambda i: (i, 0)),
                  pl.BlockSpec((1, d), lambda i: (0, 0))],
        out_specs=pl.BlockSpec((tc, d), lambda i: (i, 0)),
        out_shape=jax.ShapeDtypeStruct((t, d), F32),
        scratch_shapes=[pltpu.VMEM((tc * SUBLANES, LANES), F32), pltpu.SemaphoreType.DMA(())],
        compiler_params=_cparams(("arbitrary",)),
        name="moe_combine",
    )(pos, ys, x1, fin_g)


def _moe_sparse(x1, rows, code, counts, wr, br, wgu, wd):
    t, d = x1.shape
    n_tiles = t // MOE_TILE + N_CLASSES
    cnt = counts[0, :N_CLASSES].astype(jnp.int32)
    padded = (cnt + MOE_TILE - 1) // MOE_TILE * MOE_TILE
    ends = jnp.cumsum(padded)
    starts = ends - padded
    n_slots = n_tiles * MOE_TILE
    fill = jnp.concatenate([starts + cnt, ends[-1:], padded - cnt, n_slots - ends[-1:]])
    tile_start = jnp.arange(n_tiles, dtype=jnp.int32) * MOE_TILE
    tile_cls = jnp.minimum(jnp.sum((tile_start[:, None] >= ends[None, :]).astype(jnp.int32), axis=1), N_CLASSES - 1)
    lo_ids, hi_ids = _class_experts()
    onehot = (tile_cls[:, None] == jnp.arange(N_CLASSES, dtype=jnp.int32)[None, :]).astype(jnp.int32)
    tile_lo = jnp.sum(onehot * jnp.asarray(lo_ids, jnp.int32)[None, :], axis=1)
    tile_hi = jnp.sum(onehot * jnp.asarray(hi_ids, jnp.int32)[None, :], axis=1)
    tile_used = (tile_start < ends[-1]).astype(jnp.int32)
    xs, pos = _dispatch(code.reshape(t), starts, fill, rows, n_slots, td=1024)
    ys = _experts(tile_lo, tile_hi, tile_used, xs, wr, br, wgu, wd, d=d)
    return ys, pos


def _gm_layer_kernel(pos_ref, pos_next_ref, ys_hbm, x_ref, ng_ref, win_ref, wqm_ref, lng_ref, lnb_ref, wsp_ref,
                     bsp_ref, kv_ref, wo_mix_ref, wo_mem_ref, fg_ref, wr_ref, br_ref,
                     x1_ref, rows_ref, code_ref, cnt_ref, carry_ref, buf_ref, sems, *, tm, n_groups):
    c = CHUNK
    tw = n_groups * HEAD_DIM
    d = x_ref.shape[1]
    i = pl.program_id(0)
    slot = lax.rem(i, 2)

    @pl.when(i == 0)
    def _():
        _gather_start(pos_ref, ys_hbm, buf_ref.at[0], sems.at[0], tm)

    @pl.when(i + 1 < pl.num_programs(0))
    def _():
        _gather_start(pos_next_ref, ys_hbm, buf_ref.at[1 - slot], sems.at[1 - slot], tm)

    _gather_wait(ys_hbm, buf_ref.at[slot], sems.at[slot], tm)
    subs = range(N_SUB)
    sm = tm // N_SUB
    gathered = buf_ref.at[slot]
    x = [x_ref[t * sm:(t + 1) * sm, :]
         + jnp.concatenate([gathered[pl.ds(t * sm * SUBLANES + cc, sm, stride=SUBLANES), :]
                            for cc in range(d // LANES)], axis=1) for t in subs]
    h = [_rms(x[t], ng_ref[...]).astype(BF16) for t in subs]
    proj = [_dot(h[t], win_ref[...]) for t in subs]
    uv = [0.5 * p * (1.0 + lax.erf(p * (2.0 ** -0.5))) for p in proj]
    vn = []
    for t in subs:
        v = uv[t][:, tw:]
        vc = v - jnp.mean(v, axis=-1, keepdims=True)
        var = jnp.mean(vc * vc, axis=-1, keepdims=True)
        vn.append((vc * lax.rsqrt(var + EPS) * lng_ref[...] + lnb_ref[...]).astype(BF16))
    ii = lax.broadcasted_iota(jnp.int32, (c, c), 0)
    jj = lax.broadcasted_iota(jnp.int32, (c, c), 1)
    tril = ii >= jj
    n_chunks = sm // c
    wc = [jnp.where(tril, wsp_ref[g], jnp.zeros((c, c), BF16)) for g in range(n_groups)]
    mix = []
    for t in subs:
        col_parts = []
        for g in range(n_groups):
            cols = slice(g * HEAD_DIM, (g + 1) * HEAD_DIM)
            wide = _dot(wc[g], jnp.concatenate([vn[t][n * c:(n + 1) * c, cols] for n in range(n_chunks)], axis=1))
            bias = bsp_ref[g]
            col_parts.append(jnp.concatenate(
                [wide[:, n * HEAD_DIM:(n + 1) * HEAD_DIM] + bias for n in range(n_chunks)], axis=0))
        mix.append((uv[t][:, :tw] * jnp.concatenate(col_parts, axis=1)).astype(BF16))
    kv = kv_ref[0]
    qm = [_dot(h[t], wqm_ref[...]) for t in subs]
    cross = [_cross_attn(qm[t], kv[:, :MEM_WIDTH], kv[:, MEM_WIDTH:]).astype(BF16) for t in subs]
    _mixer_tail(x, mix, cross, wo_mix_ref, wo_mem_ref, fg_ref, wr_ref, br_ref,
                x1_ref, rows_ref, code_ref, cnt_ref, carry_ref)


def _gm_layer(pos, ys, x2d, ng, win, wqm, lng, lnb, wsp, bsp, kv, kv_layer, wo_mix, wo_mem, fg, wr, br, *, tm, seq):
    t, d = x2d.shape
    n_steps = t // tm
    tw = lng.shape[1]
    n_groups = tw // HEAD_DIM
    mlen = kv.shape[1]
    tiles_per_seq = seq // tm
    const = lambda shape: pl.BlockSpec(shape, lambda i: (0,) * len(shape))
    tok = lambda w: pl.BlockSpec((tm, w), lambda i: (i, 0))
    return pl.pallas_call(
        functools.partial(_gm_layer_kernel, tm=tm, n_groups=n_groups),
        grid=(t // tm,),
        in_specs=[pl.BlockSpec((tm,), lambda i: (i,), memory_space=pltpu.SMEM),
                  pl.BlockSpec((tm,), lambda i: (jnp.minimum(i + 1, n_steps - 1),), memory_space=pltpu.SMEM),
                  pl.BlockSpec(memory_space=pl.ANY),
                  tok(d), const((1, d)), const((d, 2 * tw)), const((d, MEM_WIDTH)), const((1, tw)), const((1, tw)),
                  const((n_groups, CHUNK, CHUNK)), const((n_groups, CHUNK, HEAD_DIM)),
                  pl.BlockSpec((1, mlen, 2 * MEM_WIDTH), lambda i: (i // tiles_per_seq, 0, kv_layer)),
                  const((tw, d)), const((MEM_WIDTH, d)), const((1, d)), const((d, LANES)), const((1, LANES))],
        out_specs=_tail_out_specs(tm, d),
        out_shape=_tail_out_shapes(t, d, tm),
        scratch_shapes=[pltpu.VMEM((1, LANES), F32), pltpu.VMEM((2, tm * SUBLANES, LANES), F32),
                        pltpu.SemaphoreType.DMA((2,))],
        compiler_params=_cparams(("arbitrary",)),
        name="gm_layer",
    )(pos, pos, ys, x2d, ng, win, wqm, lng, lnb, wsp, bsp, kv, wo_mix, wo_mem, fg, wr, br)


def _router_params(w_group, b_group, w_router, b_router):
    d = w_group.shape[0]
    pad = LANES - N_EXPERTS - MOE_GROUPS
    wr = jnp.concatenate([w_router, w_group, jnp.zeros((d, pad), F32)], axis=1).astype(BF16)
    br = jnp.concatenate([b_router, b_group, jnp.zeros((pad,), F32)])[None, :]
    return wr, br


def _col16(vec):
    return jnp.concatenate([vec, jnp.zeros((16 - vec.shape[0],), F32)])[:, None]


def kernel(x, mem, mem_norm_g, mix_norm_g, w_out, w_mem_kv, dn_w_in, dn_conv_w, dn_a_log, dn_dt_bias, dn_o_norm_g,
           gm_w_in, gm_ln_g, gm_ln_b, gm_w_spatial, gm_b_spatial, ffn_norm_g, moe_w_group, moe_b_group,
           moe_w_router, moe_b_router, moe_w_gate, moe_w_up, moe_w_down, final_norm_g):
    b, s, d = x.shape
    tw = d - MEM_WIDTH
    n_heads = tw // HEAD_DIM
    t = b * s

    kv = _memkv(mem, mem_norm_g[None, :], jnp.concatenate([w_mem_kv[0], w_mem_kv[1]], axis=1).astype(BF16))

    w_in = dn_w_in[0]
    o1, o2, o3, o4 = 3 * tw, 4 * tw, 4 * tw + n_heads, 4 * tw + 2 * n_heads
    wabt = jnp.concatenate([w_in[:, o2:o4].T, jnp.zeros((16 - 2 * n_heads, d), F32)], axis=0).astype(BF16)
    q, k, v, zg, gb, cross = _dn_front(
        x, mix_norm_g[0][None, :], w_in[:, :o1].astype(BF16), w_in[:, o1:o2].astype(BF16),
        w_in[:, o4:].astype(BF16), wabt, dn_conv_w[0], _col16(dn_a_log[0]), _col16(dn_dt_bias[0]), kv, tm=512)
    u, wp, at, qg, kts, egl = _dn_intra(q, k, v, gb, tb=1024)
    o = _dn_scan(qg, k, u, wp, at, kts, egl, tb=512, n_batch=2 if b % 2 == 0 else 1)
    wr0, br0 = _router_params(moe_w_group[0], moe_b_group[0], moe_w_router[0], moe_b_router[0])
    wo0 = w_out[0].astype(BF16)
    x1, rows0, code0, cnt0 = _dn_out(
        x.reshape(t, d), o.reshape(t, tw), zg.reshape(t, tw), cross.reshape(t, MEM_WIDTH),
        dn_o_norm_g[0][None, :], wo0[:tw], wo0[tw:], ffn_norm_g[0][None, :], wr0, br0, tm=1024)
    wgu0 = jnp.concatenate([moe_w_gate[0], moe_w_up[0]], axis=2).astype(BF16)
    ys0, pos0 = _moe_sparse(x1, rows0, code0, cnt0, wr0, br0, wgu0, moe_w_down[0].astype(BF16))

    win1 = gm_w_in[0]
    wr1, br1 = _router_params(moe_w_group[1], moe_b_group[1], moe_w_router[1], moe_b_router[1])
    wo1 = w_out[1].astype(BF16)
    bsp = jnp.broadcast_to(gm_b_spatial[0][:, :, None], (n_heads, CHUNK, HEAD_DIM))
    x3, rows1, code1, cnt1 = _gm_layer(
        pos0, ys0, x1, mix_norm_g[1][None, :], win1[:, :2 * tw].astype(BF16), win1[:, 2 * tw:].astype(BF16),
        gm_ln_g[0][None, :], gm_ln_b[0][None, :], gm_w_spatial[0].astype(BF16), bsp, kv, 1,
        wo1[:tw], wo1[tw:], ffn_norm_g[1][None, :], wr1, br1, tm=1024, seq=s)
    wgu1 = jnp.concatenate([moe_w_gate[1], moe_w_up[1]], axis=2).astype(BF16)
    ys1, pos1 = _moe_sparse(x3, rows1, code1, cnt1, wr1, br1, wgu1, moe_w_down[1].astype(BF16))
    out = _combine(pos1, ys1, x3, final_norm_g[None, :], tc=512, final_norm=True)
    return out.reshape(b, s, d)
```

```python
import functools

import jax
import jax.numpy as jnp
from jax import lax
from jax.experimental import pallas as pl
from jax.experimental.pallas import tpu as pltpu

F32 = jnp.float32
BF16 = jnp.bfloat16
EPS = 1e-6

MEM_HEADS = 4
MEM_HEAD_DIM = 64
MEM_WIDTH = MEM_HEADS * MEM_HEAD_DIM
HEAD_DIM = 128
CONV_K = 4
CHUNK = 128
MOE_GROUPS = 4
EXPERTS_PER_GROUP = 4
N_EXPERTS = MOE_GROUPS * EXPERTS_PER_GROUP
PAIRS_PER_GROUP = EXPERTS_PER_GROUP * (EXPERTS_PER_GROUP - 1) // 2
N_CLASSES = MOE_GROUPS * PAIRS_PER_GROUP
MOE_TILE = 256
LANES = 128
SUBLANES = 8
RANK_BITS = 16
DMA_PRIORITIES = 2
N_SUB = 2
VMEM_LIMIT = 56 * 1024 * 1024

NT_DIMS = (((1,), (1,)), ((), ()))
NEG_LOG2_E = -1.4426950408889634


def _dot(a, b):
    return jnp.dot(a, b, preferred_element_type=F32)


def _dot_nt(a, b):
    return lax.dot_general(a, b, NT_DIMS, preferred_element_type=F32)


def _rms(x, g):
    return x * lax.rsqrt(jnp.mean(x * x, axis=-1, keepdims=True) + EPS) * g


def _sigmoid(x):
    return 1.0 / (1.0 + jnp.exp2(x * NEG_LOG2_E))


def _softplus(x):
    return jnp.maximum(x, 0.0) + jnp.log1p(jnp.exp(-jnp.abs(x)))


def _split3(x):
    hi = x.astype(BF16)
    r = x - hi.astype(F32)
    mid = r.astype(BF16)
    lo = (r - mid.astype(F32)).astype(BF16)
    return hi, mid, lo


def _cparams(sem):
    return pltpu.CompilerParams(dimension_semantics=sem, vmem_limit_bytes=VMEM_LIMIT)


def _memkv_kernel(mem_ref, g_ref, w_ref, kv_ref):
    mn = _rms(mem_ref[0], g_ref[...]).astype(BF16)
    kv_ref[0] = _dot(mn, w_ref[...]).astype(BF16)


def _memkv(mem, g, w_all):
    b, m, d = mem.shape
    n = w_all.shape[1]
    return pl.pallas_call(
        _memkv_kernel,
        grid=(b,),
        in_specs=[pl.BlockSpec((1, m, d), lambda i: (i, 0, 0)),
                  pl.BlockSpec((1, d), lambda i: (0, 0)),
                  pl.BlockSpec((d, n), lambda i: (0, 0))],
        out_specs=pl.BlockSpec((1, m, n), lambda i: (i, 0, 0)),
        out_shape=jax.ShapeDtypeStruct((b, m, n), BF16),
        compiler_params=_cparams(("parallel",)),
        name="memkv",
    )(mem, g, w_all)


def _cross_attn(qm, kmem, vmem):
    lane = lax.broadcasted_iota(jnp.int32, (1, MEM_WIDTH), 1)
    out = jnp.zeros(qm.shape, F32)
    for h in range(MEM_HEADS):
        msk = (lane >= h * MEM_HEAD_DIM) & (lane < (h + 1) * MEM_HEAD_DIM)
        qh = jnp.where(msk, qm, 0.0).astype(BF16)
        s = _dot_nt(qh, kmem) * (MEM_HEAD_DIM ** -0.5)
        p = jnp.exp(s - jnp.max(s, axis=-1, keepdims=True))
        inv = 1.0 / jnp.sum(p, axis=-1, keepdims=True)
        vh = jnp.where(msk, vmem, jnp.zeros_like(vmem))
        out = out + _dot(p.astype(BF16), vh) * inv
    return out


def _router_class(logits):
    lane = lax.broadcasted_iota(jnp.int32, logits.shape, 1).astype(F32)
    neg = jnp.float32(-jnp.inf)
    big = jnp.float32(1 << 20)
    first = lambda hit: jnp.min(jnp.where(hit, lane, big), axis=-1, keepdims=True)
    is_g = (lane >= N_EXPERTS) & (lane < N_EXPERTS + MOE_GROUPS)
    gl = jnp.where(is_g, logits, neg)
    g_idx = first(gl == jnp.max(gl, axis=-1, keepdims=True)) - N_EXPERTS
    lo = g_idx * EXPERTS_PER_GROUP
    in_grp = (lane >= lo) & (lane < lo + EXPERTS_PER_GROUP)
    el = jnp.where(in_grp, logits, neg)
    i1 = first(el == jnp.max(el, axis=-1, keepdims=True))
    el2 = jnp.where(lane == i1, neg, el)
    i2 = first(el2 == jnp.max(el2, axis=-1, keepdims=True))
    e_lo = jnp.minimum(i1, i2) - lo
    e_hi = jnp.maximum(i1, i2) - lo
    pair_off = jnp.where(e_lo == 0.0, 0.0, jnp.where(e_lo == 1.0, 3.0, 5.0))
    return (g_idx * PAIRS_PER_GROUP + pair_off + e_hi - e_lo - 1.0).astype(jnp.int32)


def _pair_gates(logits, e_lo, e_hi):
    lane = lax.broadcasted_iota(jnp.int32, logits.shape, 1)
    pick = lambda idx: jnp.sum(jnp.where(lane == idx, logits, 0.0), axis=-1, keepdims=True)
    l_lo, l_hi = pick(e_lo), pick(e_hi)
    g_sel = pick(N_EXPERTS + e_lo // EXPERTS_PER_GROUP)
    is_g = (lane >= N_EXPERTS) & (lane < N_EXPERTS + MOE_GROUPS)
    p_grp = 1.0 / jnp.sum(jnp.where(is_g, jnp.exp(logits - g_sel), 0.0), axis=-1, keepdims=True)
    return p_grp / (1.0 + jnp.exp(l_hi - l_lo)), p_grp / (1.0 + jnp.exp(l_lo - l_hi))


def _class_experts():
    lo_ids, hi_ids = [], []
    for g in range(MOE_GROUPS):
        for a in range(EXPERTS_PER_GROUP):
            for b in range(a + 1, EXPERTS_PER_GROUP):
                lo_ids.append(g * EXPERTS_PER_GROUP + a)
                hi_ids.append(g * EXPERTS_PER_GROUP + b)
    return lo_ids, hi_ids


def _dn_front_kernel(x_ref, ng_ref, wqkv_ref, wz_ref, wqm_ref, wabt_ref, convw_ref, alog_ref, dtb_ref, kv_ref,
                     q_ref, k_ref, v_ref, zg_ref, gb_ref, cross_ref, cbuf_ref, *, tm, n_heads):
    j = pl.program_id(1)
    tw = n_heads * HEAD_DIM
    h = _rms(x_ref[0], ng_ref[...]).astype(BF16)

    @pl.when(j == 0)
    def _():
        cbuf_ref[0:8, :] = jnp.zeros((8, 3 * tw), F32)

    gw = 2 * HEAD_DIM
    out_refs = (q_ref, k_ref, v_ref)
    scales = (HEAD_DIM ** -0.5, 1.0, None)
    for grp in range(3 * tw // gw):
        cols = slice(grp * gw, (grp + 1) * gw)
        pre = _dot(h, wqkv_ref[:, cols])
        cbuf_ref[8:8 + tm, cols] = pre
        w = convw_ref[:, cols]
        acc = w[CONV_K - 1:CONV_K, :] * pre
        for kk in range(CONV_K - 1):
            acc = acc + w[kk:kk + 1, :] * cbuf_ref[pl.ds(8 - (CONV_K - 1) + kk, tm), cols]
        cbuf_ref[0:8, cols] = cbuf_ref[tm:tm + 8, cols]
        act = acc * _sigmoid(acc)
        which, first = divmod(grp * gw, tw)
        for half in range(gw // HEAD_DIM):
            a = act[:, half * HEAD_DIM:(half + 1) * HEAD_DIM]
            if scales[which] is not None:
                a = a * (lax.rsqrt(jnp.sum(a * a, axis=-1, keepdims=True) + EPS) * scales[which])
            dst = first + half * HEAD_DIM
            out_refs[which][0, :, dst:dst + HEAD_DIM] = a.astype(BF16)

    z = _dot(h, wz_ref[...])
    zg_ref[0] = (z * _sigmoid(z)).astype(BF16)

    abt = _dot_nt(wabt_ref[...], h)
    g_t = -jnp.exp(alog_ref[...]) * _softplus(abt + dtb_ref[...])
    b_t = _sigmoid(abt)
    for hd in range(n_heads):
        gb_ref[0, hd, 0:1, :] = g_t[hd:hd + 1, :]
        gb_ref[0, hd, 1:2, :] = b_t[n_heads + hd:n_heads + hd + 1, :]

    qm = _dot(h, wqm_ref[...])
    kv = kv_ref[0]
    cross_ref[0] = _cross_attn(qm, kv[:, :MEM_WIDTH], kv[:, MEM_WIDTH:]).astype(BF16)


def _dn_front(x, ng, wqkv, wz, wqm, wabt, convw, alog, dtb, kv, *, tm):
    b, s, d = x.shape
    tw = wz.shape[1]
    n_heads = tw // HEAD_DIM
    mlen = kv.shape[1]
    const = lambda shape: pl.BlockSpec(shape, lambda i, j: (0,) * len(shape))
    tok = lambda w: pl.BlockSpec((1, tm, w), lambda i, j: (i, j, 0))
    return pl.pallas_call(
        functools.partial(_dn_front_kernel, tm=tm, n_heads=n_heads),
        grid=(b, s // tm),
        in_specs=[tok(d), const((1, d)), const((d, 3 * tw)), const((d, tw)), const((d, MEM_WIDTH)),
                  const((16, d)), const((CONV_K, 3 * tw)), const((16, 1)), const((16, 1)),
                  pl.BlockSpec((1, mlen, 2 * MEM_WIDTH), lambda i, j: (i, 0, 0))],
        out_specs=[tok(tw), tok(tw), tok(tw), tok(tw),
                   pl.BlockSpec((1, n_heads, 2, tm), lambda i, j: (i, 0, 0, j)),
                   tok(MEM_WIDTH)],
        out_shape=[jax.ShapeDtypeStruct((b, s, tw), BF16)] * 4
                  + [jax.ShapeDtypeStruct((b, n_heads, 2, s), F32),
                     jax.ShapeDtypeStruct((b, s, MEM_WIDTH), BF16)],
        scratch_shapes=[pltpu.VMEM((tm + 8, 3 * tw), F32)],
        compiler_params=_cparams(("parallel", "arbitrary")),
        name="dn_front",
    )(x, ng, wqkv, wz, wqm, wabt, convw, alog, dtb, kv)


def _cat_lanes(a, b):
    return jnp.concatenate([a, b], axis=1)


def _block_diag(a, b):
    z = jnp.zeros(a.shape, a.dtype)
    return jnp.concatenate([_cat_lanes(a, z), _cat_lanes(z, b)], axis=0)


def _pair_dot(a0, a1, b0, b1):
    w = a0.shape[1]
    r = _dot(_cat_lanes(a0, a1), _block_diag(b0, b1))
    return r[:, :w], r[:, w:]


def _tri_inverse_pairs(a_pairs, ii, jj):
    c = CHUNK
    eye = (ii == jj).astype(F32)
    blk = (ii >> 4) == (jj >> 4)
    n0 = [[jnp.where(blk, -a, 0.0) for a in pair] for pair in a_pairs]
    x = [[eye + n for n in pair] for pair in n0]
    nb = [[n.astype(BF16) for n in pair] for pair in n0]
    nb = [[r.astype(BF16) for r in _pair_dot(p[0], p[1], p[0], p[1])] for p in nb]
    for _ in range(2):
        r = [[_dot(n, _cat_lanes(n, xi.astype(BF16))) for n, xi in zip(pn, px)] for pn, px in zip(nb, x)]
        nb = [[ri[:, :c].astype(BF16) for ri in pr] for pr in r]
        x = [[xi + ri[:, c:] for xi, ri in zip(px, pr)] for px, pr in zip(x, r)]
    fin = [_pair_dot(pn[0], pn[1], px[0].astype(BF16), px[1].astype(BF16)) for pn, px in zip(nb, x)]
    x = [[xi + fi for xi, fi in zip(px, pf)] for px, pf in zip(x, fin)]
    shift = 4
    while (1 << shift) < c:
        inner = (ii >> shift) == (jj >> shift)
        outer = (ii >> (shift + 1)) == (jj >> (shift + 1))
        sel = outer & jnp.logical_not(inner)
        xb = [[xi.astype(BF16) for xi in px] for px in x]
        lo = [[jnp.where(sel, a, 0.0).astype(BF16) for a in pa] for pa in a_pairs]
        lx = [[r.astype(BF16) for r in _pair_dot(pl_[0], pl_[1], pb[0], pb[1])] for pl_, pb in zip(lo, xb)]
        cor = [_pair_dot(pb[0], pb[1], pl_[0], pl_[1]) for pb, pl_ in zip(xb, lx)]
        x = [[xi - ci for xi, ci in zip(px, pc)] for px, pc in zip(x, cor)]
        shift += 1
    return x


def _dn_intra_kernel(q_ref, k_ref, v_ref, gb_ref, u_ref, wp_ref, at_ref, qg_ref, kts_ref, egl_ref, *, n_chunks):
    c = CHUNK
    hw = HEAD_DIM
    ii = lax.broadcasted_iota(jnp.int32, (c, c), 0)
    jj = lax.broadcasted_iota(jnp.int32, (c, c), 1)
    tril = ii >= jj
    strict = ii > jj
    lane2 = lax.broadcasted_iota(jnp.int32, (c, 2 * hw), 1)
    upper_incl = (ii <= jj).astype(BF16)
    chunks = range(n_chunks)
    heads = range(2)
    rows = [slice(n * c, (n + 1) * c) for n in chunks]
    hcol = [slice(hd * hw, (hd + 1) * hw) for hd in heads]
    kp = [k_ref[0, r, :] for r in rows]
    zero = jnp.zeros((c, 2 * hw), BF16)
    kq = [_dot_nt(jnp.concatenate([kp[n], q_ref[0, rows[n], :]], axis=0),
                  jnp.concatenate([jnp.where(lane2 < hw, kp[n], zero), jnp.where(lane2 >= hw, kp[n], zero)], axis=0))
          for n in chunks]
    g_rows = [gb_ref[0, hd, 0:1, rows[n]] for n in chunks for hd in heads]
    hi, mid, lo = _split3(jnp.concatenate(g_rows, axis=0))
    n_rows = len(g_rows)
    cum = _dot(jnp.concatenate([hi, mid, lo], axis=0), upper_incl)
    gc_all = cum[:n_rows] + cum[n_rows:2 * n_rows] + cum[2 * n_rows:]
    a_pairs, b_rs, e_rs = [], [], []
    for n in chunks:
        a_pair, b_pair, e_pair = [], [], []
        for hd in heads:
            g_row = g_rows[n * 2 + hd]
            b_row = gb_ref[0, hd, 1:2, rows[n]]
            gc_row = gc_all[n * 2 + hd:n * 2 + hd + 1, :]
            gl = jnp.sum(g_row, axis=-1, keepdims=True)
            g_r = jnp.broadcast_to(gc_row, (c, c))
            g_c = g_r.T
            b_r = jnp.broadcast_to(b_row, (c, c))
            decay = jnp.where(tril, jnp.exp(g_c - g_r), 0.0)
            kqh = kq[n][:, hd * c:(hd + 1) * c]
            a_pair.append(jnp.where(strict, kqh[:c] * decay, 0.0) * b_r.T)
            at_ref[0, rows[n], hcol[hd]] = jnp.where(tril, kqh[c:] * decay, 0.0).astype(BF16)
            qg_ref[0, rows[n], hcol[hd]] = (q_ref[0, rows[n], hcol[hd]].astype(F32) * jnp.exp(g_c)).astype(BF16)
            kh = kp[n][:, hcol[hd]].astype(F32)
            kts_ref[0, hcol[hd], rows[n]] = (kh.T * jnp.exp(gl - gc_row)).astype(BF16)
            egl_ref[0, n, :, hcol[hd]] = jnp.broadcast_to(jnp.exp(gl), (1, hw))
            b_pair.append(b_r)
            e_pair.append(jnp.exp(g_r))
        a_pairs.append(a_pair)
        b_rs.append(b_pair)
        e_rs.append(e_pair)
    t = _tri_inverse_pairs(a_pairs, ii, jj)
    tb = [[t[n][hd] * b_rs[n][hd] for hd in heads] for n in chunks]
    u = [_pair_dot(tb[n][0].astype(BF16), tb[n][1].astype(BF16),
                   v_ref[0, rows[n], hcol[0]], v_ref[0, rows[n], hcol[1]]) for n in chunks]
    for n in chunks:
        for hd in heads:
            u_ref[0, rows[n], hcol[hd]] = u[n][hd].astype(BF16)
            wp_ref[0, rows[n], hcol[hd]] = (tb[n][hd] * e_rs[n][hd]).astype(BF16)


def _dn_intra(q, k, v, gb, *, tb):
    b, s, tw = q.shape
    n_chunks = tb // CHUNK
    pw = 2 * HEAD_DIM
    tok = pl.BlockSpec((1, tb, pw), lambda i, h, j: (i, j, h))
    return pl.pallas_call(
        functools.partial(_dn_intra_kernel, n_chunks=n_chunks),
        grid=(b, tw // pw, s // tb),
        in_specs=[tok, tok, tok, pl.BlockSpec((1, 2, 2, tb), lambda i, h, j: (i, h, 0, j))],
        out_specs=[tok, tok, tok, tok,
                   pl.BlockSpec((1, pw, tb), lambda i, h, j: (i, h, j)),
                   pl.BlockSpec((1, n_chunks, 1, pw), lambda i, h, j: (i, j, 0, h))],
        out_shape=[jax.ShapeDtypeStruct((b, s, tw), BF16)] * 4
                  + [jax.ShapeDtypeStruct((b, tw, s), BF16),
                     jax.ShapeDtypeStruct((b, s // CHUNK, 1, tw), F32)],
        compiler_params=_cparams(("parallel", "parallel", "parallel")),
        name="dn_intra",
    )(q, k, v, gb)


def _dn_scan_kernel(qg_ref, k_ref, u_ref, wp_ref, at_ref, kts_ref, egl_ref, o_ref, s_ref,
                    *, n_chunks, n_pairs, n_batch):
    c = CHUNK
    pw = 2 * HEAD_DIM

    @pl.when(pl.program_id(1) == 0)
    def _():
        s_ref[...] = jnp.zeros(s_ref.shape, F32)

    items = [(bi, p) for bi in range(n_batch) for p in range(n_pairs)]
    cols = [slice(p * pw, (p + 1) * pw) for p in range(n_pairs)]
    lane = lax.broadcasted_iota(jnp.int32, (c, pw), 1)
    ri = lax.broadcasted_iota(jnp.int32, (pw, pw), 0)
    ci = lax.broadcasted_iota(jnp.int32, (pw, pw), 1)
    same_head = (ri >= HEAD_DIM) == (ci >= HEAD_DIM)
    zero = jnp.zeros((c, pw), BF16)

    def stacked_diag(m):
        return jnp.concatenate([jnp.where(lane < HEAD_DIM, m, zero), jnp.where(lane >= HEAD_DIM, m, zero)], axis=0)

    states = [s_ref[bi * n_pairs + p] for bi, p in items]
    for n in range(n_chunks):
        rows = slice(n * c, (n + 1) * c)
        kqs = [_dot(jnp.concatenate([k_ref[bi, rows, cols[p]], qg_ref[bi, rows, cols[p]]], axis=0), st.astype(BF16))
               for (bi, p), st in zip(items, states)]
        wks = [_dot(wp_ref[bi, rows, cols[p]], stacked_diag(kq[:c].astype(BF16))) for (bi, p), kq in zip(items, kqs)]
        xb = [(u_ref[bi, rows, cols[p]].astype(F32) - wk).astype(BF16) for (bi, p), wk in zip(items, wks)]
        upd = [_dot(kts_ref[bi, cols[p], rows], x) for (bi, p), x in zip(items, xb)]
        states = [st * egl_ref[bi, n, :, cols[p]] + jnp.where(same_head, up, 0.0)
                  for (bi, p), st, up in zip(items, states, upd)]
        intra = [_dot(at_ref[bi, rows, cols[p]], stacked_diag(x)) for (bi, p), x in zip(items, xb)]
        for (bi, p), it, kq in zip(items, intra, kqs):
            o_ref[bi, rows, cols[p]] = (it + kq[c:]).astype(BF16)
    for (bi, p), st in zip(items, states):
        s_ref[bi * n_pairs + p] = st


def _dn_scan(qg, k, u, wp, at, kts, egl, *, tb, n_batch):
    b, s, tw = qg.shape
    n_chunks = tb // CHUNK
    pw = 2 * HEAD_DIM
    tok = pl.BlockSpec((n_batch, tb, tw), lambda i, j: (i, j, 0))
    return pl.pallas_call(
        functools.partial(_dn_scan_kernel, n_chunks=n_chunks, n_pairs=tw // pw, n_batch=n_batch),
        grid=(b // n_batch, s // tb),
        in_specs=[tok, tok, tok, tok, tok,
                  pl.BlockSpec((n_batch, tw, tb), lambda i, j: (i, 0, j)),
                  pl.BlockSpec((n_batch, n_chunks, 1, tw), lambda i, j: (i, j, 0, 0))],
        out_specs=tok,
        out_shape=jax.ShapeDtypeStruct((b, s, tw), BF16),
        scratch_shapes=[pltpu.VMEM((n_batch * (tw // pw), pw, pw), F32)],
        compiler_params=_cparams(("parallel", "arbitrary")),
        name="dn_scan",
    )(qg, k, u, wp, at, kts, egl)


def _mixer_tail(x, mix_b, cross_b, wo_mix_ref, wo_mem_ref, fg_ref, wr_ref, br_ref,
                x1_ref, rows_ref, code_ref, cnt_ref, carry_ref):
    subs = range(len(x))
    sm, d = x[0].shape
    assert d == SUBLANES * LANES
    y = [_dot(mix_b[t], wo_mix_ref[...]) + _dot(cross_b[t], wo_mem_ref[...]) for t in subs]
    x1 = [x[t] + y[t] for t in subs]
    for t in subs:
        x1_ref[t * sm:(t + 1) * sm, :] = x1[t]
    h2 = [_rms(x1[t], fg_ref[...]) for t in subs]
    logits = [_dot(h2[t].astype(BF16), wr_ref[...]) + br_ref[...] for t in subs]
    cls = [_router_class(logits[t]) for t in subs]
    for t in subs:
        for c in range(SUBLANES):
            rows_ref[pl.ds(t * sm * SUBLANES + c, sm, stride=SUBLANES), :] = h2[t][:, c * LANES:(c + 1) * LANES]

    @pl.when(pl.program_id(0) == 0)
    def _():
        carry_ref[...] = jnp.zeros(carry_ref.shape, F32)

    lane = lax.broadcasted_iota(jnp.int32, (sm, LANES), 1)
    ii = lax.broadcasted_iota(jnp.int32, (sm, sm), 0)
    jj = lax.broadcasted_iota(jnp.int32, (sm, sm), 1)
    earlier = (ii > jj).astype(BF16)
    onehot = [lane == cls[t] for t in subs]
    within = [_dot(earlier, onehot[t].astype(BF16)) for t in subs]
    carry = carry_ref[...]
    l8 = lax.broadcasted_iota(jnp.int32, (SUBLANES, LANES), 1)
    s8 = lax.broadcasted_iota(jnp.int32, (SUBLANES, LANES), 0)
    for t in subs:
        rank = jnp.sum(jnp.where(onehot[t], within[t] + carry, 0.0), axis=-1, keepdims=True).astype(jnp.int32)
        carry = carry + jnp.sum(onehot[t].astype(F32), axis=0, keepdims=True)
        cols = jnp.where(lane == 0, cls[t], jnp.where(lane == 1, rank >> 8, jnp.where(lane == 2, rank & 255, 0)))
        picked = _dot_nt((l8 == s8).astype(BF16), cols.astype(F32).astype(BF16)).astype(jnp.int32)
        code_ref[0, :, t * sm:(t + 1) * sm] = (picked[0:1] << RANK_BITS) | (picked[1:2] << 8) | picked[2:3]
    carry_ref[...] = carry
    cnt_ref[...] = carry


def _tail_out_specs(tm, d):
    return [pl.BlockSpec((tm, d), lambda i: (i, 0)),
            pl.BlockSpec((tm * SUBLANES, LANES), lambda i: (i, 0)),
            pl.BlockSpec((1, 1, tm), lambda i: (i, 0, 0)),
            pl.BlockSpec((1, LANES), lambda i: (0, 0))]


def _tail_out_shapes(t, d, tm):
    return [jax.ShapeDtypeStruct((t, d), F32), jax.ShapeDtypeStruct((t * SUBLANES, LANES), F32),
            jax.ShapeDtypeStruct((t // tm, 1, tm), jnp.int32), jax.ShapeDtypeStruct((1, LANES), F32)]


def _dn_out_kernel(x_ref, o_ref, zg_ref, cross_ref, og_ref, wo_mix_ref, wo_mem_ref, fg_ref, wr_ref, br_ref,
                   x1_ref, rows_ref, code_ref, cnt_ref, carry_ref, *, n_heads):
    og = og_ref[...]
    sm = x_ref.shape[0] // N_SUB
    xs, mixes, crosses = [], [], []
    for t in range(N_SUB):
        rows = slice(t * sm, (t + 1) * sm)
        parts = []
        for hd in range(n_heads):
            cols = slice(hd * HEAD_DIM, (hd + 1) * HEAD_DIM)
            oh = o_ref[rows, cols].astype(F32)
            on = oh * lax.rsqrt(jnp.mean(oh * oh, axis=-1, keepdims=True) + EPS) * og
            parts.append((on * zg_ref[rows, cols].astype(F32)).astype(BF16))
        mixes.append(jnp.concatenate(parts, axis=1))
        xs.append(x_ref[rows, :])
        crosses.append(cross_ref[rows, :])
    _mixer_tail(xs, mixes, crosses, wo_mix_ref, wo_mem_ref, fg_ref, wr_ref, br_ref,
                x1_ref, rows_ref, code_ref, cnt_ref, carry_ref)


def _dn_out(x2d, o2d, zg2d, cross2d, og, wo_mix, wo_mem, fg, wr, br, *, tm):
    t, d = x2d.shape
    tw = o2d.shape[1]
    n_heads = tw // HEAD_DIM
    const = lambda shape: pl.BlockSpec(shape, lambda i: (0,) * len(shape))
    tok = lambda w: pl.BlockSpec((tm, w), lambda i: (i, 0))
    return pl.pallas_call(
        functools.partial(_dn_out_kernel, n_heads=n_heads),
        grid=(t // tm,),
        in_specs=[tok(d), tok(tw), tok(tw), tok(MEM_WIDTH), const((1, HEAD_DIM)), const((tw, d)),
                  const((MEM_WIDTH, d)), const((1, d)), const((d, LANES)), const((1, LANES))],
        out_specs=_tail_out_specs(tm, d),
        out_shape=_tail_out_shapes(t, d, tm),
        scratch_shapes=[pltpu.VMEM((1, LANES), F32)],
        compiler_params=_cparams(("arbitrary",)),
        name="dn_out",
    )(x2d, o2d, zg2d, cross2d, og, wo_mix, wo_mem, fg, wr, br)


def _row_tile(ref, r):
    return ref.at[pl.ds(pl.multiple_of(r * SUBLANES, SUBLANES), SUBLANES), :]


def _dispatch_kernel(code_ref, starts_ref, fill_ref, rows_ref, xs_hbm, pos_ref, zero_ref, sem, pad_sem, *, td):
    rank_mask = (1 << RANK_BITS) - 1

    @pl.when(pl.program_id(0) == 0)
    def _():
        zero_ref[...] = jnp.zeros(zero_ref.shape, F32)
        n_fill = fill_ref.shape[0] // 2

        def slots(ref, first, n):
            return ref.at[pl.ds(pl.multiple_of(first * SUBLANES, SUBLANES), n * SUBLANES), :]

        def walk(act):
            for c in range(n_fill):
                first, count = fill_ref[c], fill_ref[n_fill + c]
                n_whole = count // MOE_TILE

                def whole(kk, carry, first=first):
                    act(pltpu.make_async_copy(zero_ref, slots(xs_hbm, first + kk * MOE_TILE, MOE_TILE), pad_sem))
                    return carry

                lax.fori_loop(0, n_whole, whole, 0)
                off = first + n_whole * MOE_TILE
                rem = count - n_whole * MOE_TILE
                bit = MOE_TILE // 2
                while bit:
                    @pl.when((rem & bit) != 0)
                    def _(off=off, bit=bit):
                        act(pltpu.make_async_copy(slots(zero_ref, 0, bit), slots(xs_hbm, off, bit), pad_sem))

                    off = off + (rem & bit)
                    bit //= 2

        walk(lambda cp: cp.start())
        walk(lambda cp: cp.wait())

    def issue(pair, carry):
        for prio in range(DMA_PRIORITIES):
            r = pair * DMA_PRIORITIES + prio
            code = code_ref[r]
            slot = starts_ref[code >> RANK_BITS] + (code & rank_mask)
            pos_ref[r] = slot
            pltpu.make_async_copy(_row_tile(rows_ref, r), _row_tile(xs_hbm, slot), sem).start(priority=prio)
        return carry

    lax.fori_loop(0, td // DMA_PRIORITIES, issue, 0, unroll=4)
    pltpu.make_async_copy(rows_ref, xs_hbm.at[pl.ds(0, td * SUBLANES), :], sem).wait()


def _dispatch(code, starts, fill, rows, n_slots, *, td):
    t = code.shape[0]
    smem_whole = lambda n: pl.BlockSpec((n,), lambda i: (0,), memory_space=pltpu.SMEM)
    return pl.pallas_call(
        functools.partial(_dispatch_kernel, td=td),
        grid=(t // td,),
        in_specs=[pl.BlockSpec((td,), lambda i: (i,), memory_space=pltpu.SMEM),
                  smem_whole(starts.shape[0]), smem_whole(fill.shape[0]),
                  pl.BlockSpec((td * SUBLANES, LANES), lambda i: (i, 0))],
        out_specs=[pl.BlockSpec(memory_space=pl.ANY),
                   pl.BlockSpec((td,), lambda i: (i,), memory_space=pltpu.SMEM)],
        out_shape=[jax.ShapeDtypeStruct((n_slots * SUBLANES, LANES), F32),
                   jax.ShapeDtypeStruct((t,), jnp.int32)],
        scratch_shapes=[pltpu.VMEM((MOE_TILE * SUBLANES, LANES), F32), pltpu.SemaphoreType.DMA(()),
                        pltpu.SemaphoreType.DMA(())],
        compiler_params=_cparams(("arbitrary",)),
        name="moe_dispatch",
    )(code, starts, fill, rows)


def _experts_kernel(ea_ref, eb_ref, used_ref, xs_ref, wr_ref, br_ref, *refs, tile, d, d_expert, n_sub):
    n_w = 6 * n_sub
    w_refs, ys_ref = refs[:n_w], refs[n_w]
    gu_scr, dn_scr = refs[n_w + 1:n_w + 1 + 2 * n_sub], refs[n_w + 1 + 2 * n_sub:]
    i = pl.program_id(0)
    subs = range(n_sub)
    tid = [i * n_sub + t for t in subs]
    base = [t * tile * SUBLANES for t in subs]
    n_used = used_ref[tid[0]]
    for t in subs[1:]:
        n_used = n_used + used_ref[tid[t]]

    for t in subs:
        for e, ids in enumerate((ea_ref, eb_ref)):
            prev = ids[jnp.maximum(tid[t] - n_sub, 0)]

            @pl.when((i == 0) | (ids[tid[t]] != prev))
            def _(t=t, e=e):
                gate_ref, up_ref, down_ref = w_refs[6 * t + 3 * e:6 * t + 3 * e + 3]
                gu_scr[2 * t + e][:, :d_expert] = gate_ref[0, 0].astype(BF16)
                gu_scr[2 * t + e][:, d_expert:] = up_ref[0, 0].astype(BF16)
                dn_scr[2 * t + e][...] = down_ref[0, 0].astype(BF16)

    @pl.when(n_used > 0)
    def _():
        h = [jnp.concatenate([xs_ref[pl.ds(base[t] + c, tile, stride=SUBLANES), :] for c in range(d // LANES)],
                             axis=1).astype(BF16) for t in subs]
        gates = [_pair_gates(_dot(h[t], wr_ref[...]) + br_ref[...], ea_ref[tid[t]], eb_ref[tid[t]]) for t in subs]
        gu = [[_dot(h[t], gu_scr[2 * t + e][...]) for e in range(2)] for t in subs]
        parts = [[(g[:, :d_expert] * _sigmoid(g[:, :d_expert]) * g[:, d_expert:] * gates[t][e]).astype(BF16)
                  for e, g in enumerate(gu[t])] for t in subs]
        y = [_dot(parts[t][0], dn_scr[2 * t][...]) + _dot(parts[t][1], dn_scr[2 * t + 1][...]) for t in subs]
        for t in subs:
            for c in range(d // LANES):
                ys_ref[pl.ds(base[t] + c, tile, stride=SUBLANES), :] = y[t][:, c * LANES:(c + 1) * LANES]

    @pl.when(n_used == 0)
    def _():
        ys_ref[...] = jnp.zeros(ys_ref.shape, F32)


def _experts(tile_lo, tile_hi, tile_used, xs, wr, br, w_gate, w_up, w_down, layer, *, d, n_sub=2):
    n_tiles = tile_lo.shape[0]
    de = w_gate.shape[3]
    blk = n_sub * MOE_TILE * SUBLANES
    const = lambda shape: pl.BlockSpec(shape, lambda i, ea, eb, us: (0,) * len(shape))
    weight_specs, weights = [], []
    for t in range(n_sub):
        for which in range(2):
            pick = lambda i, ea, eb, us, t=t, which=which: (layer, (eb if which else ea)[i * n_sub + t], 0, 0)
            weight_specs += [pl.BlockSpec((1, 1, d, de), pick), pl.BlockSpec((1, 1, d, de), pick),
                             pl.BlockSpec((1, 1, de, d), pick)]
            weights += [w_gate, w_up, w_down]
    grid_spec = pltpu.PrefetchScalarGridSpec(
        num_scalar_prefetch=3,
        grid=(n_tiles // n_sub,),
        in_specs=[pl.BlockSpec((blk, LANES), lambda i, ea, eb, us: (i, 0)), const((d, LANES)), const((1, LANES))]
                 + weight_specs,
        out_specs=pl.BlockSpec((blk, LANES), lambda i, ea, eb, us: (i, 0)),
        scratch_shapes=[pltpu.VMEM((d, 2 * de), BF16)] * (2 * n_sub) + [pltpu.VMEM((de, d), BF16)] * (2 * n_sub))
    return pl.pallas_call(
        functools.partial(_experts_kernel, tile=MOE_TILE, d=d, d_expert=de, n_sub=n_sub),
        grid_spec=grid_spec,
        out_shape=jax.ShapeDtypeStruct((n_tiles * MOE_TILE * SUBLANES, LANES), F32),
        compiler_params=_cparams(("arbitrary",)),
        name="moe_experts",
    )(tile_lo, tile_hi, tile_used, xs, wr, br, *weights)


def _gather_start(pos_ref, ys_hbm, buf_ref, sem, n):
    def issue(pair, carry):
        for prio in range(DMA_PRIORITIES):
            r = pair * DMA_PRIORITIES + prio
            pltpu.make_async_copy(_row_tile(ys_hbm, pos_ref[r]), _row_tile(buf_ref, r), sem).start(priority=prio)
        return carry

    lax.fori_loop(0, n // DMA_PRIORITIES, issue, 0, unroll=4)


def _gather_wait(ys_hbm, buf_ref, sem, n):
    pltpu.make_async_copy(ys_hbm.at[pl.ds(0, n * SUBLANES), :], buf_ref, sem).wait()


def _token_major(buf_ref, n, d):
    return jnp.concatenate([buf_ref[pl.ds(c, n, stride=SUBLANES), :] for c in range(d // LANES)], axis=1)


def _combine_kernel(pos_ref, ys_hbm, x_ref, fin_ref, out_ref, buf_ref, sem, *, tc, d, final_norm):
    _gather_start(pos_ref, ys_hbm, buf_ref, sem, tc)
    _gather_wait(ys_hbm, buf_ref, sem, tc)
    out = x_ref[...] + _token_major(buf_ref, tc, d)
    if final_norm:
        out = _rms(out, fin_ref[...])
    out_ref[...] = out


def _combine(pos, ys, x1, fin_g, *, tc, final_norm):
    t, d = x1.shape
    return pl.pallas_call(
        functools.partial(_combine_kernel, tc=tc, d=d, final_norm=final_norm),
        grid=(t // tc,),
        in_specs=[pl.BlockSpec((tc,), lambda i: (i,), memory_space=pltpu.SMEM),
                  pl.BlockSpec(memory_space=pl.ANY),
                  pl.BlockSpec((tc, d), lambda i: (i, 0)),
                  pl.BlockSpec((1, d), lambda i: (0, 0))],
        out_specs=pl.BlockSpec((tc, d), lambda i: (i, 0)),
        out_shape=jax.ShapeDtypeStruct((t, d), F32),
        scratch_shapes=[pltpu.VMEM((tc * SUBLANES, LANES), F32), pltpu.SemaphoreType.DMA(())],
        compiler_params=_cparams(("arbitrary",)),
        name="moe_combine",
    )(pos, ys, x1, fin_g)


def _moe_sparse(x1, rows, code, counts, wr, br, w_gate, w_up, w_down, layer):
    t, d = x1.shape
    n_tiles = t // MOE_TILE + N_CLASSES
    cnt = counts[0, :N_CLASSES].astype(jnp.int32)
    padded = (cnt + MOE_TILE - 1) // MOE_TILE * MOE_TILE
    ends = jnp.cumsum(padded)
    starts = ends - padded
    n_slots = n_tiles * MOE_TILE
    fill = jnp.concatenate([starts + cnt, ends[-1:], padded - cnt, n_slots - ends[-1:]])
    tile_start = jnp.arange(n_tiles, dtype=jnp.int32) * MOE_TILE
    tile_cls = jnp.minimum(jnp.sum((tile_start[:, None] >= ends[None, :]).astype(jnp.int32), axis=1), N_CLASSES - 1)
    lo_ids, hi_ids = _class_experts()
    onehot = (tile_cls[:, None] == jnp.arange(N_CLASSES, dtype=jnp.int32)[None, :]).astype(jnp.int32)
    tile_lo = jnp.sum(onehot * jnp.asarray(lo_ids, jnp.int32)[None, :], axis=1)
    tile_hi = jnp.sum(onehot * jnp.asarray(hi_ids, jnp.int32)[None, :], axis=1)
    tile_used = (tile_start < ends[-1]).astype(jnp.int32)
    xs, pos = _dispatch(code.reshape(t), starts, fill, rows, n_slots, td=1024)
    ys = _experts(tile_lo, tile_hi, tile_used, xs, wr, br, w_gate, w_up, w_down, layer, d=d)
    return ys, pos


def _gm_layer_kernel(pos_ref, pos_next_ref, ys_hbm, x_ref, ng_ref, win_ref, wqm_ref, lng_ref, lnb_ref, wsp_ref,
                     bsp_ref, kv_ref, wo_mix_ref, wo_mem_ref, fg_ref, wr_ref, br_ref,
                     x1_ref, rows_ref, code_ref, cnt_ref, carry_ref, buf_ref, sems, *, tm, n_groups):
    c = CHUNK
    tw = n_groups * HEAD_DIM
    d = x_ref.shape[1]
    i = pl.program_id(0)
    slot = lax.rem(i, 2)

    @pl.when(i == 0)
    def _():
        _gather_start(pos_ref, ys_hbm, buf_ref.at[0], sems.at[0], tm)

    @pl.when(i + 1 < pl.num_programs(0))
    def _():
        _gather_start(pos_next_ref, ys_hbm, buf_ref.at[1 - slot], sems.at[1 - slot], tm)

    _gather_wait(ys_hbm, buf_ref.at[slot], sems.at[slot], tm)
    subs = range(N_SUB)
    sm = tm // N_SUB
    gathered = buf_ref.at[slot]
    x = [x_ref[t * sm:(t + 1) * sm, :]
         + jnp.concatenate([gathered[pl.ds(t * sm * SUBLANES + cc, sm, stride=SUBLANES), :]
                            for cc in range(d // LANES)], axis=1) for t in subs]
    h = [_rms(x[t], ng_ref[...]).astype(BF16) for t in subs]
    proj = [_dot(h[t], win_ref[...]) for t in subs]
    uv = [0.5 * p * (1.0 + lax.erf(p * (2.0 ** -0.5))) for p in proj]
    vn = []
    for t in subs:
        v = uv[t][:, tw:]
        vc = v - jnp.mean(v, axis=-1, keepdims=True)
        var = jnp.mean(vc * vc, axis=-1, keepdims=True)
        vn.append((vc * lax.rsqrt(var + EPS) * lng_ref[...] + lnb_ref[...]).astype(BF16))
    ii = lax.broadcasted_iota(jnp.int32, (c, c), 0)
    jj = lax.broadcasted_iota(jnp.int32, (c, c), 1)
    tril = ii >= jj
    n_chunks = sm // c
    wc = [jnp.where(tril, wsp_ref[g], jnp.zeros((c, c), BF16)) for g in range(n_groups)]
    mix = []
    for t in subs:
        col_parts = []
        for g in range(n_groups):
            cols = slice(g * HEAD_DIM, (g + 1) * HEAD_DIM)
            wide = _dot(wc[g], jnp.concatenate([vn[t][n * c:(n + 1) * c, cols] for n in range(n_chunks)], axis=1))
            bias = bsp_ref[g]
            col_parts.append(jnp.concatenate(
                [wide[:, n * HEAD_DIM:(n + 1) * HEAD_DIM] + bias for n in range(n_chunks)], axis=0))
        mix.append((uv[t][:, :tw] * jnp.concatenate(col_parts, axis=1)).astype(BF16))
    kv = kv_ref[0]
    qm = [_dot(h[t], wqm_ref[...]) for t in subs]
    cross = [_cross_attn(qm[t], kv[:, :MEM_WIDTH], kv[:, MEM_WIDTH:]).astype(BF16) for t in subs]
    _mixer_tail(x, mix, cross, wo_mix_ref, wo_mem_ref, fg_ref, wr_ref, br_ref,
                x1_ref, rows_ref, code_ref, cnt_ref, carry_ref)


def _gm_layer(pos, ys, x2d, ng, win, wqm, lng, lnb, wsp, bsp, kv, kv_layer, wo_mix, wo_mem, fg, wr, br, *, tm, seq):
    t, d = x2d.shape
    n_steps = t // tm
    tw = lng.shape[1]
    n_groups = tw // HEAD_DIM
    mlen = kv.shape[1]
    tiles_per_seq = seq // tm
    const = lambda shape: pl.BlockSpec(shape, lambda i: (0,) * len(shape))
    tok = lambda w: pl.BlockSpec((tm, w), lambda i: (i, 0))
    return pl.pallas_call(
        functools.partial(_gm_layer_kernel, tm=tm, n_groups=n_groups),
        grid=(t // tm,),
        in_specs=[pl.BlockSpec((tm,), lambda i: (i,), memory_space=pltpu.SMEM),
                  pl.BlockSpec((tm,), lambda i: (jnp.minimum(i + 1, n_steps - 1),), memory_space=pltpu.SMEM),
                  pl.BlockSpec(memory_space=pl.ANY),
                  tok(d), const((1, d)), const((d, 2 * tw)), const((d, MEM_WIDTH)), const((1, tw)), const((1, tw)),
                  const((n_groups, CHUNK, CHUNK)), const((n_groups, CHUNK, HEAD_DIM)),
                  pl.BlockSpec((1, mlen, 2 * MEM_WIDTH), lambda i: (i // tiles_per_seq, 0, kv_layer)),
                  const((tw, d)), const((MEM_WIDTH, d)), const((1, d)), const((d, LANES)), const((1, LANES))],
        out_specs=_tail_out_specs(tm, d),
        out_shape=_tail_out_shapes(t, d, tm),
        scratch_shapes=[pltpu.VMEM((1, LANES), F32), pltpu.VMEM((2, tm * SUBLANES, LANES), F32),
                        pltpu.SemaphoreType.DMA((2,))],
        compiler_params=_cparams(("arbitrary",)),
        name="gm_layer",
    )(pos, pos, ys, x2d, ng, win, wqm, lng, lnb, wsp, bsp, kv, wo_mix, wo_mem, fg, wr, br)


def _router_params(w_group, b_group, w_router, b_router):
    d = w_group.shape[0]
    pad = LANES - N_EXPERTS - MOE_GROUPS
    wr = jnp.concatenate([w_router, w_group, jnp.zeros((d, pad), F32)], axis=1).astype(BF16)
    br = jnp.concatenate([b_router, b_group, jnp.zeros((pad,), F32)])[None, :]
    return wr, br


def _col16(vec):
    return jnp.concatenate([vec, jnp.zeros((16 - vec.shape[0],), F32)])[:, None]


def kernel(x, mem, mem_norm_g, mix_norm_g, w_out, w_mem_kv, dn_w_in, dn_conv_w, dn_a_log, dn_dt_bias, dn_o_norm_g,
           gm_w_in, gm_ln_g, gm_ln_b, gm_w_spatial, gm_b_spatial, ffn_norm_g, moe_w_group, moe_b_group,
           moe_w_router, moe_b_router, moe_w_gate, moe_w_up, moe_w_down, final_norm_g):
    b, s, d = x.shape
    tw = d - MEM_WIDTH
    n_heads = tw // HEAD_DIM
    t = b * s

    kv = _memkv(mem, mem_norm_g[None, :], jnp.concatenate([w_mem_kv[0], w_mem_kv[1]], axis=1).astype(BF16))

    w_in = dn_w_in[0]
    o1, o2, o3, o4 = 3 * tw, 4 * tw, 4 * tw + n_heads, 4 * tw + 2 * n_heads
    wabt = jnp.concatenate([w_in[:, o2:o4].T, jnp.zeros((16 - 2 * n_heads, d), F32)], axis=0).astype(BF16)
    q, k, v, zg, gb, cross = _dn_front(
        x, mix_norm_g[0][None, :], w_in[:, :o1].astype(BF16), w_in[:, o1:o2].astype(BF16),
        w_in[:, o4:].astype(BF16), wabt, dn_conv_w[0], _col16(dn_a_log[0]), _col16(dn_dt_bias[0]), kv, tm=512)
    u, wp, at, qg, kts, egl = _dn_intra(q, k, v, gb, tb=1024)
    o = _dn_scan(qg, k, u, wp, at, kts, egl, tb=512, n_batch=2 if b % 2 == 0 else 1)
    wr0, br0 = _router_params(moe_w_group[0], moe_b_group[0], moe_w_router[0], moe_b_router[0])
    wo0 = w_out[0].astype(BF16)
    x1, rows0, code0, cnt0 = _dn_out(
        x.reshape(t, d), o.reshape(t, tw), zg.reshape(t, tw), cross.reshape(t, MEM_WIDTH),
        dn_o_norm_g[0][None, :], wo0[:tw], wo0[tw:], ffn_norm_g[0][None, :], wr0, br0, tm=1024)
    ys0, pos0 = _moe_sparse(x1, rows0, code0, cnt0, wr0, br0, moe_w_gate, moe_w_up, moe_w_down, 0)

    win1 = gm_w_in[0]
    wr1, br1 = _router_params(moe_w_group[1], moe_b_group[1], moe_w_router[1], moe_b_router[1])
    wo1 = w_out[1].astype(BF16)
    bsp = jnp.broadcast_to(gm_b_spatial[0][:, :, None], (n_heads, CHUNK, HEAD_DIM))
    x3, rows1, code1, cnt1 = _gm_layer(
        pos0, ys0, x1, mix_norm_g[1][None, :], win1[:, :2 * tw].astype(BF16), win1[:, 2 * tw:].astype(BF16),
        gm_ln_g[0][None, :], gm_ln_b[0][None, :], gm_w_spatial[0].astype(BF16), bsp, kv, 1,
        wo1[:tw], wo1[tw:], ffn_norm_g[1][None, :], wr1, br1, tm=1024, seq=s)
    ys1, pos1 = _moe_sparse(x3, rows1, code1, cnt1, wr1, br1, moe_w_gate, moe_w_up, moe_w_down, 1)
    out = _combine(pos1, ys1, x3, final_norm_g[None, :], tc=512, final_norm=True)
    return out.reshape(b, s, d)
```

```python
import functools

import jax
import jax.numpy as jnp
from jax import lax
from jax.experimental import pallas as pl
from jax.experimental.pallas import tpu as pltpu

F32 = jnp.float32
BF16 = jnp.bfloat16
EPS = 1e-6

MEM_HEADS = 4
MEM_HEAD_DIM = 64
MEM_WIDTH = MEM_HEADS * MEM_HEAD_DIM
HEAD_DIM = 128
CONV_K = 4
CHUNK = 128
MOE_GROUPS = 4
EXPERTS_PER_GROUP = 4
N_EXPERTS = MOE_GROUPS * EXPERTS_PER_GROUP
PAIRS_PER_GROUP = EXPERTS_PER_GROUP * (EXPERTS_PER_GROUP - 1) // 2
N_CLASSES = MOE_GROUPS * PAIRS_PER_GROUP
MOE_TILE = 256
LANES = 128
SUBLANES = 8
RANK_BITS = 16
DMA_PRIORITIES = 2
N_SUB = 2
VMEM_LIMIT = 56 * 1024 * 1024

NT_DIMS = (((1,), (1,)), ((), ()))
NEG_LOG2_E = -1.4426950408889634


def _dot(a, b):
    return jnp.dot(a, b, preferred_element_type=F32)


def _dot_nt(a, b):
    return lax.dot_general(a, b, NT_DIMS, preferred_element_type=F32)


def _rms(x, g):
    return x * lax.rsqrt(jnp.mean(x * x, axis=-1, keepdims=True) + EPS) * g


def _sigmoid(x):
    return 1.0 / (1.0 + jnp.exp2(x * NEG_LOG2_E))


def _softplus(x):
    return jnp.maximum(x, 0.0) + jnp.log1p(jnp.exp(-jnp.abs(x)))


def _split3(x):
    hi = x.astype(BF16)
    r = x - hi.astype(F32)
    mid = r.astype(BF16)
    lo = (r - mid.astype(F32)).astype(BF16)
    return hi, mid, lo


def _cparams(sem):
    return pltpu.CompilerParams(dimension_semantics=sem, vmem_limit_bytes=VMEM_LIMIT)


def _memkv_kernel(mem_ref, g_ref, w_ref, kv_ref):
    mn = _rms(mem_ref[0], g_ref[...]).astype(BF16)
    kv_ref[0] = _dot(mn, w_ref[...]).astype(BF16)


def _memkv(mem, g, w_all):
    b, m, d = mem.shape
    n = w_all.shape[1]
    return pl.pallas_call(
        _memkv_kernel,
        grid=(b,),
        in_specs=[pl.BlockSpec((1, m, d), lambda i: (i, 0, 0)),
                  pl.BlockSpec((1, d), lambda i: (0, 0)),
                  pl.BlockSpec((d, n), lambda i: (0, 0))],
        out_specs=pl.BlockSpec((1, m, n), lambda i: (i, 0, 0)),
        out_shape=jax.ShapeDtypeStruct((b, m, n), BF16),
        compiler_params=_cparams(("parallel",)),
        name="memkv",
    )(mem, g, w_all)


def _cross_attn(qm, kmem, vmem):
    lane = lax.broadcasted_iota(jnp.int32, (1, MEM_WIDTH), 1)
    out = jnp.zeros(qm.shape, F32)
    for h in range(MEM_HEADS):
        msk = (lane >= h * MEM_HEAD_DIM) & (lane < (h + 1) * MEM_HEAD_DIM)
        qh = jnp.where(msk, qm, 0.0).astype(BF16)
        s = _dot_nt(qh, kmem) * (MEM_HEAD_DIM ** -0.5)
        p = jnp.exp(s - jnp.max(s, axis=-1, keepdims=True))
        inv = 1.0 / jnp.sum(p, axis=-1, keepdims=True)
        vh = jnp.where(msk, vmem, jnp.zeros_like(vmem))
        out = out + _dot(p.astype(BF16), vh) * inv
    return out


def _router_class(logits):
    lane = lax.broadcasted_iota(jnp.int32, logits.shape, 1).astype(F32)
    neg = jnp.float32(-jnp.inf)
    big = jnp.float32(1 << 20)
    first = lambda hit: jnp.min(jnp.where(hit, lane, big), axis=-1, keepdims=True)
    is_g = (lane >= N_EXPERTS) & (lane < N_EXPERTS + MOE_GROUPS)
    gl = jnp.where(is_g, logits, neg)
    g_idx = first(gl == jnp.max(gl, axis=-1, keepdims=True)) - N_EXPERTS
    lo = g_idx * EXPERTS_PER_GROUP
    in_grp = (lane >= lo) & (lane < lo + EXPERTS_PER_GROUP)
    el = jnp.where(in_grp, logits, neg)
    i1 = first(el == jnp.max(el, axis=-1, keepdims=True))
    el2 = jnp.where(lane == i1, neg, el)
    i2 = first(el2 == jnp.max(el2, axis=-1, keepdims=True))
    e_lo = jnp.minimum(i1, i2) - lo
    e_hi = jnp.maximum(i1, i2) - lo
    pair_off = jnp.where(e_lo == 0.0, 0.0, jnp.where(e_lo == 1.0, 3.0, 5.0))
    return (g_idx * PAIRS_PER_GROUP + pair_off + e_hi - e_lo - 1.0).astype(jnp.int32)


def _pair_gates(logits, e_lo, e_hi):
    lane = lax.broadcasted_iota(jnp.int32, logits.shape, 1)
    pick = lambda idx: jnp.sum(jnp.where(lane == idx, logits, 0.0), axis=-1, keepdims=True)
    l_lo, l_hi = pick(e_lo), pick(e_hi)
    g_sel = pick(N_EXPERTS + e_lo // EXPERTS_PER_GROUP)
    is_g = (lane >= N_EXPERTS) & (lane < N_EXPERTS + MOE_GROUPS)
    p_grp = 1.0 / jnp.sum(jnp.where(is_g, jnp.exp(logits - g_sel), 0.0), axis=-1, keepdims=True)
    return p_grp / (1.0 + jnp.exp(l_hi - l_lo)), p_grp / (1.0 + jnp.exp(l_lo - l_hi))


def _class_experts():
    lo_ids, hi_ids = [], []
    for g in range(MOE_GROUPS):
        for a in range(EXPERTS_PER_GROUP):
            for b in range(a + 1, EXPERTS_PER_GROUP):
                lo_ids.append(g * EXPERTS_PER_GROUP + a)
                hi_ids.append(g * EXPERTS_PER_GROUP + b)
    return lo_ids, hi_ids


def _dn_front_kernel(x_ref, ng_ref, wqkv_ref, wz_ref, wqm_ref, wabt_ref, convw_ref, alog_ref, dtb_ref, kv_ref,
                     q_ref, k_ref, v_ref, zg_ref, gb_ref, cross_ref, cbuf_ref, *, tm, n_heads):
    j = pl.program_id(1)
    tw = n_heads * HEAD_DIM
    h = _rms(x_ref[0], ng_ref[...]).astype(BF16)

    @pl.when(j == 0)
    def _():
        cbuf_ref[0:8, :] = jnp.zeros((8, 3 * tw), F32)

    gw = 2 * HEAD_DIM
    out_refs = (q_ref, k_ref, v_ref)
    scales = (HEAD_DIM ** -0.5, 1.0, None)
    for grp in range(3 * tw // gw):
        cols = slice(grp * gw, (grp + 1) * gw)
        pre = _dot(h, wqkv_ref[:, cols])
        cbuf_ref[8:8 + tm, cols] = pre
        w = convw_ref[:, cols]
        acc = w[CONV_K - 1:CONV_K, :] * pre
        for kk in range(CONV_K - 1):
            acc = acc + w[kk:kk + 1, :] * cbuf_ref[pl.ds(8 - (CONV_K - 1) + kk, tm), cols]
        cbuf_ref[0:8, cols] = cbuf_ref[tm:tm + 8, cols]
        act = acc * _sigmoid(acc)
        which, first = divmod(grp * gw, tw)
        for half in range(gw // HEAD_DIM):
            a = act[:, half * HEAD_DIM:(half + 1) * HEAD_DIM]
            if scales[which] is not None:
                a = a * (lax.rsqrt(jnp.sum(a * a, axis=-1, keepdims=True) + EPS) * scales[which])
            dst = first + half * HEAD_DIM
            out_refs[which][0, :, dst:dst + HEAD_DIM] = a.astype(BF16)

    z = _dot(h, wz_ref[...])
    zg_ref[0] = (z * _sigmoid(z)).astype(BF16)

    abt = _dot_nt(wabt_ref[...], h)
    g_t = -jnp.exp(alog_ref[...]) * _softplus(abt + dtb_ref[...])
    b_t = _sigmoid(abt)
    for hd in range(n_heads):
        gb_ref[0, hd, 0:1, :] = g_t[hd:hd + 1, :]
        gb_ref[0, hd, 1:2, :] = b_t[n_heads + hd:n_heads + hd + 1, :]

    qm = _dot(h, wqm_ref[...])
    kv = kv_ref[0]
    cross_ref[0] = _cross_attn(qm, kv[:, :MEM_WIDTH], kv[:, MEM_WIDTH:]).astype(BF16)


def _dn_front(x, ng, wqkv, wz, wqm, wabt, convw, alog, dtb, kv, *, tm):
    b, s, d = x.shape
    tw = wz.shape[1]
    n_heads = tw // HEAD_DIM
    mlen = kv.shape[1]
    const = lambda shape: pl.BlockSpec(shape, lambda i, j: (0,) * len(shape))
    tok = lambda w: pl.BlockSpec((1, tm, w), lambda i, j: (i, j, 0))
    return pl.pallas_call(
        functools.partial(_dn_front_kernel, tm=tm, n_heads=n_heads),
        grid=(b, s // tm),
        in_specs=[tok(d), const((1, d)), const((d, 3 * tw)), const((d, tw)), const((d, MEM_WIDTH)),
                  const((16, d)), const((CONV_K, 3 * tw)), const((16, 1)), const((16, 1)),
                  pl.BlockSpec((1, mlen, 2 * MEM_WIDTH), lambda i, j: (i, 0, 0))],
        out_specs=[tok(tw), tok(tw), tok(tw), tok(tw),
                   pl.BlockSpec((1, n_heads, 2, tm), lambda i, j: (i, 0, 0, j)),
                   tok(MEM_WIDTH)],
        out_shape=[jax.ShapeDtypeStruct((b, s, tw), BF16)] * 4
                  + [jax.ShapeDtypeStruct((b, n_heads, 2, s), F32),
                     jax.ShapeDtypeStruct((b, s, MEM_WIDTH), BF16)],
        scratch_shapes=[pltpu.VMEM((tm + 8, 3 * tw), F32)],
        compiler_params=_cparams(("parallel", "arbitrary")),
        name="dn_front",
    )(x, ng, wqkv, wz, wqm, wabt, convw, alog, dtb, kv)


def _cat_lanes(a, b):
    return jnp.concatenate([a, b], axis=1)


def _block_diag(a, b):
    z = jnp.zeros(a.shape, a.dtype)
    return jnp.concatenate([_cat_lanes(a, z), _cat_lanes(z, b)], axis=0)


def _pair_dot(a0, a1, b0, b1):
    w = a0.shape[1]
    r = _dot(_cat_lanes(a0, a1), _block_diag(b0, b1))
    return r[:, :w], r[:, w:]


def _tri_inverse_pairs(a_pairs, ii, jj):
    c = CHUNK
    eye = (ii == jj).astype(F32)
    blk = (ii >> 4) == (jj >> 4)
    n0 = [[jnp.where(blk, -a, 0.0) for a in pair] for pair in a_pairs]
    x = [[eye + n for n in pair] for pair in n0]
    nb = [[n.astype(BF16) for n in pair] for pair in n0]
    nb = [[r.astype(BF16) for r in _pair_dot(p[0], p[1], p[0], p[1])] for p in nb]
    for _ in range(2):
        r = [[_dot(n, _cat_lanes(n, xi.astype(BF16))) for n, xi in zip(pn, px)] for pn, px in zip(nb, x)]
        nb = [[ri[:, :c].astype(BF16) for ri in pr] for pr in r]
        x = [[xi + ri[:, c:] for xi, ri in zip(px, pr)] for px, pr in zip(x, r)]
    fin = [_pair_dot(pn[0], pn[1], px[0].astype(BF16), px[1].astype(BF16)) for pn, px in zip(nb, x)]
    x = [[xi + fi for xi, fi in zip(px, pf)] for px, pf in zip(x, fin)]
    shift = 4
    while (1 << shift) < c:
        inner = (ii >> shift) == (jj >> shift)
        outer = (ii >> (shift + 1)) == (jj >> (shift + 1))
        sel = outer & jnp.logical_not(inner)
        xb = [[xi.astype(BF16) for xi in px] for px in x]
        lo = [[jnp.where(sel, a, 0.0).astype(BF16) for a in pa] for pa in a_pairs]
        lx = [[r.astype(BF16) for r in _pair_dot(pl_[0], pl_[1], pb[0], pb[1])] for pl_, pb in zip(lo, xb)]
        cor = [_pair_dot(pb[0], pb[1], pl_[0], pl_[1]) for pb, pl_ in zip(xb, lx)]
        x = [[xi - ci for xi, ci in zip(px, pc)] for px, pc in zip(x, cor)]
        shift += 1
    return x


def _dn_intra_kernel(q_ref, k_ref, v_ref, gb_ref, u_ref, wp_ref, at_ref, qg_ref, kts_ref, egl_ref, *, n_chunks):
    c = CHUNK
    hw = HEAD_DIM
    ii = lax.broadcasted_iota(jnp.int32, (c, c), 0)
    jj = lax.broadcasted_iota(jnp.int32, (c, c), 1)
    tril = ii >= jj
    strict = ii > jj
    lane2 = lax.broadcasted_iota(jnp.int32, (c, 2 * hw), 1)
    upper_incl = (ii <= jj).astype(BF16)
    chunks = range(n_chunks)
    heads = range(2)
    rows = [slice(n * c, (n + 1) * c) for n in chunks]
    hcol = [slice(hd * hw, (hd + 1) * hw) for hd in heads]
    kp = [k_ref[0, r, :] for r in rows]
    zero = jnp.zeros((c, 2 * hw), BF16)
    kq = [_dot_nt(jnp.concatenate([kp[n], q_ref[0, rows[n], :]], axis=0),
                  jnp.concatenate([jnp.where(lane2 < hw, kp[n], zero), jnp.where(lane2 >= hw, kp[n], zero)], axis=0))
          for n in chunks]
    g_rows = [gb_ref[0, hd, 0:1, rows[n]] for n in chunks for hd in heads]
    hi, mid, lo = _split3(jnp.concatenate(g_rows, axis=0))
    n_rows = len(g_rows)
    cum = _dot(jnp.concatenate([hi, mid, lo], axis=0), upper_incl)
    gc_all = cum[:n_rows] + cum[n_rows:2 * n_rows] + cum[2 * n_rows:]
    a_pairs, b_rs, e_rs = [], [], []
    for n in chunks:
        a_pair, b_pair, e_pair = [], [], []
        for hd in heads:
            g_row = g_rows[n * 2 + hd]
            b_row = gb_ref[0, hd, 1:2, rows[n]]
            gc_row = gc_all[n * 2 + hd:n * 2 + hd + 1, :]
            gl = jnp.sum(g_row, axis=-1, keepdims=True)
            g_r = jnp.broadcast_to(gc_row, (c, c))
            g_c = g_r.T
            b_r = jnp.broadcast_to(b_row, (c, c))
            decay = jnp.where(tril, jnp.exp(g_c - g_r), 0.0)
            kqh = kq[n][:, hd * c:(hd + 1) * c]
            a_pair.append(jnp.where(strict, kqh[:c] * decay, 0.0) * b_r.T)
            at_ref[0, rows[n], hcol[hd]] = jnp.where(tril, kqh[c:] * decay, 0.0).astype(BF16)
            qg_ref[0, rows[n], hcol[hd]] = (q_ref[0, rows[n], hcol[hd]].astype(F32) * jnp.exp(g_c)).astype(BF16)
            kh = kp[n][:, hcol[hd]].astype(F32)
            kts_ref[0, hcol[hd], rows[n]] = (kh.T * jnp.exp(gl - gc_row)).astype(BF16)
            egl_ref[0, n, :, hcol[hd]] = jnp.broadcast_to(jnp.exp(gl), (1, hw))
            b_pair.append(b_r)
            e_pair.append(jnp.exp(g_r))
        a_pairs.append(a_pair)
        b_rs.append(b_pair)
        e_rs.append(e_pair)
    t = _tri_inverse_pairs(a_pairs, ii, jj)
    tb = [[t[n][hd] * b_rs[n][hd] for hd in heads] for n in chunks]
    u = [_pair_dot(tb[n][0].astype(BF16), tb[n][1].astype(BF16),
                   v_ref[0, rows[n], hcol[0]], v_ref[0, rows[n], hcol[1]]) for n in chunks]
    for n in chunks:
        for hd in heads:
            u_ref[0, rows[n], hcol[hd]] = u[n][hd].astype(BF16)
            wp_ref[0, rows[n], hcol[hd]] = (tb[n][hd] * e_rs[n][hd]).astype(BF16)


def _dn_intra(q, k, v, gb, *, tb):
    b, s, tw = q.shape
    n_chunks = tb // CHUNK
    pw = 2 * HEAD_DIM
    tok = pl.BlockSpec((1, tb, pw), lambda i, h, j: (i, j, h))
    return pl.pallas_call(
        functools.partial(_dn_intra_kernel, n_chunks=n_chunks),
        grid=(b, tw // pw, s // tb),
        in_specs=[tok, tok, tok, pl.BlockSpec((1, 2, 2, tb), lambda i, h, j: (i, h, 0, j))],
        out_specs=[tok, tok, tok, tok,
                   pl.BlockSpec((1, pw, tb), lambda i, h, j: (i, h, j)),
                   pl.BlockSpec((1, n_chunks, 1, pw), lambda i, h, j: (i, j, 0, h))],
        out_shape=[jax.ShapeDtypeStruct((b, s, tw), BF16)] * 4
                  + [jax.ShapeDtypeStruct((b, tw, s), BF16),
                     jax.ShapeDtypeStruct((b, s // CHUNK, 1, tw), F32)],
        compiler_params=_cparams(("parallel", "parallel", "parallel")),
        name="dn_intra",
    )(q, k, v, gb)


def _dn_scan_kernel(qg_ref, k_ref, u_ref, wp_ref, at_ref, kts_ref, egl_ref, o_ref, s_ref,
                    *, n_chunks, n_pairs, n_batch):
    c = CHUNK
    pw = 2 * HEAD_DIM

    @pl.when(pl.program_id(1) == 0)
    def _():
        s_ref[...] = jnp.zeros(s_ref.shape, F32)

    items = [(bi, p) for bi in range(n_batch) for p in range(n_pairs)]
    cols = [slice(p * pw, (p + 1) * pw) for p in range(n_pairs)]
    lane = lax.broadcasted_iota(jnp.int32, (c, pw), 1)
    ri = lax.broadcasted_iota(jnp.int32, (pw, pw), 0)
    ci = lax.broadcasted_iota(jnp.int32, (pw, pw), 1)
    same_head = (ri >= HEAD_DIM) == (ci >= HEAD_DIM)
    zero = jnp.zeros((c, pw), BF16)

    def stacked_diag(m):
        return jnp.concatenate([jnp.where(lane < HEAD_DIM, m, zero), jnp.where(lane >= HEAD_DIM, m, zero)], axis=0)

    states = [s_ref[bi * n_pairs + p] for bi, p in items]
    for n in range(n_chunks):
        rows = slice(n * c, (n + 1) * c)
        kqs = [_dot(jnp.concatenate([k_ref[bi, rows, cols[p]], qg_ref[bi, rows, cols[p]]], axis=0), st.astype(BF16))
               for (bi, p), st in zip(items, states)]
        wks = [_dot(wp_ref[bi, rows, cols[p]], stacked_diag(kq[:c].astype(BF16))) for (bi, p), kq in zip(items, kqs)]
        xb = [(u_ref[bi, rows, cols[p]].astype(F32) - wk).astype(BF16) for (bi, p), wk in zip(items, wks)]
        upd = [_dot(kts_ref[bi, cols[p], rows], x) for (bi, p), x in zip(items, xb)]
        states = [st * egl_ref[bi, n, :, cols[p]] + jnp.where(same_head, up, 0.0)
                  for (bi, p), st, up in zip(items, states, upd)]
        intra = [_dot(at_ref[bi, rows, cols[p]], stacked_diag(x)) for (bi, p), x in zip(items, xb)]
        for (bi, p), it, kq in zip(items, intra, kqs):
            o_ref[bi, rows, cols[p]] = (it + kq[c:]).astype(BF16)
    for (bi, p), st in zip(items, states):
        s_ref[bi * n_pairs + p] = st


def _dn_scan(qg, k, u, wp, at, kts, egl, *, tb, n_batch):
    b, s, tw = qg.shape
    n_chunks = tb // CHUNK
    pw = 2 * HEAD_DIM
    tok = pl.BlockSpec((n_batch, tb, tw), lambda i, j: (i, j, 0))
    return pl.pallas_call(
        functools.partial(_dn_scan_kernel, n_chunks=n_chunks, n_pairs=tw // pw, n_batch=n_batch),
        grid=(b // n_batch, s // tb),
        in_specs=[tok, tok, tok, tok, tok,
                  pl.BlockSpec((n_batch, tw, tb), lambda i, j: (i, 0, j)),
                  pl.BlockSpec((n_batch, n_chunks, 1, tw), lambda i, j: (i, j, 0, 0))],
        out_specs=tok,
        out_shape=jax.ShapeDtypeStruct((b, s, tw), BF16),
        scratch_shapes=[pltpu.VMEM((n_batch * (tw // pw), pw, pw), F32)],
        compiler_params=_cparams(("parallel", "arbitrary")),
        name="dn_scan",
    )(qg, k, u, wp, at, kts, egl)


def _mixer_tail(x, mix_b, cross_b, wo_mix_ref, wo_mem_ref, fg_ref, wr_ref, br_ref,
                x1_ref, rows_ref, code_ref, cnt_ref, carry_ref):
    subs = range(len(x))
    sm, d = x[0].shape
    assert d == SUBLANES * LANES
    y = [_dot(mix_b[t], wo_mix_ref[...]) + _dot(cross_b[t], wo_mem_ref[...]) for t in subs]
    x1 = [x[t] + y[t] for t in subs]
    for t in subs:
        x1_ref[t * sm:(t + 1) * sm, :] = x1[t]
    h2 = [_rms(x1[t], fg_ref[...]) for t in subs]
    logits = [_dot(h2[t].astype(BF16), wr_ref[...]) + br_ref[...] for t in subs]
    cls = [_router_class(logits[t]) for t in subs]
    for t in subs:
        for c in range(SUBLANES):
            rows_ref[pl.ds(t * sm * SUBLANES + c, sm, stride=SUBLANES), :] = h2[t][:, c * LANES:(c + 1) * LANES]

    @pl.when(pl.program_id(0) == 0)
    def _():
        carry_ref[...] = jnp.zeros(carry_ref.shape, F32)

    lane = lax.broadcasted_iota(jnp.int32, (sm, LANES), 1)
    ii = lax.broadcasted_iota(jnp.int32, (sm, sm), 0)
    jj = lax.broadcasted_iota(jnp.int32, (sm, sm), 1)
    earlier = (ii > jj).astype(BF16)
    onehot = [lane == cls[t] for t in subs]
    within = [_dot(earlier, onehot[t].astype(BF16)) for t in subs]
    carry = carry_ref[...]
    l8 = lax.broadcasted_iota(jnp.int32, (SUBLANES, LANES), 1)
    s8 = lax.broadcasted_iota(jnp.int32, (SUBLANES, LANES), 0)
    for t in subs:
        rank = jnp.sum(jnp.where(onehot[t], within[t] + carry, 0.0), axis=-1, keepdims=True).astype(jnp.int32)
        carry = carry + jnp.sum(onehot[t].astype(F32), axis=0, keepdims=True)
        cols = jnp.where(lane == 0, cls[t], jnp.where(lane == 1, rank >> 8, jnp.where(lane == 2, rank & 255, 0)))
        picked = _dot_nt((l8 == s8).astype(BF16), cols.astype(F32).astype(BF16)).astype(jnp.int32)
        code_ref[0, :, t * sm:(t + 1) * sm] = (picked[0:1] << RANK_BITS) | (picked[1:2] << 8) | picked[2:3]
    carry_ref[...] = carry
    cnt_ref[...] = carry


def _tail_out_specs(tm, d):
    return [pl.BlockSpec((tm, d), lambda i: (i, 0)),
            pl.BlockSpec((tm * SUBLANES, LANES), lambda i: (i, 0)),
            pl.BlockSpec((1, 1, tm), lambda i: (i, 0, 0)),
            pl.BlockSpec((1, LANES), lambda i: (0, 0))]


def _tail_out_shapes(t, d, tm):
    return [jax.ShapeDtypeStruct((t, d), F32), jax.ShapeDtypeStruct((t * SUBLANES, LANES), F32),
            jax.ShapeDtypeStruct((t // tm, 1, tm), jnp.int32), jax.ShapeDtypeStruct((1, LANES), F32)]


def _dn_out_kernel(x_ref, o_ref, zg_ref, cross_ref, og_ref, wo_mix_ref, wo_mem_ref, fg_ref, wr_ref, br_ref,
                   x1_ref, rows_ref, code_ref, cnt_ref, carry_ref, *, n_heads):
    og = og_ref[...]
    sm = x_ref.shape[0] // N_SUB
    xs, mixes, crosses = [], [], []
    for t in range(N_SUB):
        rows = slice(t * sm, (t + 1) * sm)
        parts = []
        for hd in range(n_heads):
            cols = slice(hd * HEAD_DIM, (hd + 1) * HEAD_DIM)
            oh = o_ref[rows, cols].astype(F32)
            on = oh * lax.rsqrt(jnp.mean(oh * oh, axis=-1, keepdims=True) + EPS) * og
            parts.append((on * zg_ref[rows, cols].astype(F32)).astype(BF16))
        mixes.append(jnp.concatenate(parts, axis=1))
        xs.append(x_ref[rows, :])
        crosses.append(cross_ref[rows, :])
    _mixer_tail(xs, mixes, crosses, wo_mix_ref, wo_mem_ref, fg_ref, wr_ref, br_ref,
                x1_ref, rows_ref, code_ref, cnt_ref, carry_ref)


def _dn_out(x2d, o2d, zg2d, cross2d, og, wo_mix, wo_mem, fg, wr, br, *, tm):
    t, d = x2d.shape
    tw = o2d.shape[1]
    n_heads = tw // HEAD_DIM
    const = lambda shape: pl.BlockSpec(shape, lambda i: (0,) * len(shape))
    tok = lambda w: pl.BlockSpec((tm, w), lambda i: (i, 0))
    return pl.pallas_call(
        functools.partial(_dn_out_kernel, n_heads=n_heads),
        grid=(t // tm,),
        in_specs=[tok(d), tok(tw), tok(tw), tok(MEM_WIDTH), const((1, HEAD_DIM)), const((tw, d)),
                  const((MEM_WIDTH, d)), const((1, d)), const((d, LANES)), const((1, LANES))],
        out_specs=_tail_out_specs(tm, d),
        out_shape=_tail_out_shapes(t, d, tm),
        scratch_shapes=[pltpu.VMEM((1, LANES), F32)],
        compiler_params=_cparams(("arbitrary",)),
        name="dn_out",
    )(x2d, o2d, zg2d, cross2d, og, wo_mix, wo_mem, fg, wr, br)


def _row_tile(ref, r):
    return ref.at[pl.ds(pl.multiple_of(r * SUBLANES, SUBLANES), SUBLANES), :]


def _dispatch_kernel(code_ref, starts_ref, fill_ref, rows_ref, xs_hbm, pos_ref, zero_ref, sem, pad_sem, *, td):
    rank_mask = (1 << RANK_BITS) - 1

    @pl.when(pl.program_id(0) == 0)
    def _():
        zero_ref[...] = jnp.zeros(zero_ref.shape, F32)
        n_fill = fill_ref.shape[0] // 2

        def slots(ref, first, n):
            return ref.at[pl.ds(pl.multiple_of(first * SUBLANES, SUBLANES), n * SUBLANES), :]

        def walk(act):
            for c in range(n_fill):
                first, count = fill_ref[c], fill_ref[n_fill + c]
                n_whole = count // MOE_TILE

                def whole(kk, carry, first=first):
                    act(pltpu.make_async_copy(zero_ref, slots(xs_hbm, first + kk * MOE_TILE, MOE_TILE), pad_sem))
                    return carry

                lax.fori_loop(0, n_whole, whole, 0)
                off = first + n_whole * MOE_TILE
                rem = count - n_whole * MOE_TILE
                bit = MOE_TILE // 2
                while bit:
                    @pl.when((rem & bit) != 0)
                    def _(off=off, bit=bit):
                        act(pltpu.make_async_copy(slots(zero_ref, 0, bit), slots(xs_hbm, off, bit), pad_sem))

                    off = off + (rem & bit)
                    bit //= 2

        walk(lambda cp: cp.start())
        walk(lambda cp: cp.wait())

    def issue(pair, carry):
        for prio in range(DMA_PRIORITIES):
            r = pair * DMA_PRIORITIES + prio
            code = code_ref[r]
            slot = starts_ref[code >> RANK_BITS] + (code & rank_mask)
            pos_ref[r] = slot
            pltpu.make_async_copy(_row_tile(rows_ref, r), _row_tile(xs_hbm, slot), sem).start(priority=prio)
        return carry

    lax.fori_loop(0, td // DMA_PRIORITIES, issue, 0, unroll=4)
    pltpu.make_async_copy(rows_ref, xs_hbm.at[pl.ds(0, td * SUBLANES), :], sem).wait()


def _dispatch(code, starts, fill, rows, n_slots, *, td):
    t = code.shape[0]
    smem_whole = lambda n: pl.BlockSpec((n,), lambda i: (0,), memory_space=pltpu.SMEM)
    return pl.pallas_call(
        functools.partial(_dispatch_kernel, td=td),
        grid=(t // td,),
        in_specs=[pl.BlockSpec((td,), lambda i: (i,), memory_space=pltpu.SMEM),
                  smem_whole(starts.shape[0]), smem_whole(fill.shape[0]),
                  pl.BlockSpec((td * SUBLANES, LANES), lambda i: (i, 0))],
        out_specs=[pl.BlockSpec(memory_space=pl.ANY),
                   pl.BlockSpec((td,), lambda i: (i,), memory_space=pltpu.SMEM)],
        out_shape=[jax.ShapeDtypeStruct((n_slots * SUBLANES, LANES), F32),
                   jax.ShapeDtypeStruct((t,), jnp.int32)],
        scratch_shapes=[pltpu.VMEM((MOE_TILE * SUBLANES, LANES), F32), pltpu.SemaphoreType.DMA(()),
                        pltpu.SemaphoreType.DMA(())],
        compiler_params=_cparams(("arbitrary",)),
        name="moe_dispatch",
    )(code, starts, fill, rows)


def _experts_kernel(ea_ref, eb_ref, used_ref, xs_ref, wr_ref, br_ref, *refs, tile, d, d_expert, n_sub):
    ys_ref = refs[-1]
    i = pl.program_id(0)
    subs = range(n_sub)
    tid = [i * n_sub + t for t in subs]
    base = [t * tile * SUBLANES for t in subs]
    n_used = used_ref[tid[0]]
    for t in subs[1:]:
        n_used = n_used + used_ref[tid[t]]

    @pl.when(n_used > 0)
    def _():
        h = [jnp.concatenate([xs_ref[pl.ds(base[t] + c, tile, stride=SUBLANES), :] for c in range(d // LANES)],
                             axis=1).astype(BF16) for t in subs]
        gates = [_pair_gates(_dot(h[t], wr_ref[...]) + br_ref[...], ea_ref[tid[t]], eb_ref[tid[t]]) for t in subs]
        gu = [[_dot(h[t], refs[4 * t + 2 * e][0, 0]) for e in range(2)] for t in subs]
        parts = [[(g[:, :d_expert] * _sigmoid(g[:, :d_expert]) * g[:, d_expert:] * gates[t][e]).astype(BF16)
                  for e, g in enumerate(gu[t])] for t in subs]
        y = [_dot(parts[t][0], refs[4 * t + 1][0, 0]) + _dot(parts[t][1], refs[4 * t + 3][0, 0]) for t in subs]
        for t in subs:
            for c in range(d // LANES):
                ys_ref[pl.ds(base[t] + c, tile, stride=SUBLANES), :] = y[t][:, c * LANES:(c + 1) * LANES]

    @pl.when(n_used == 0)
    def _():
        ys_ref[...] = jnp.zeros(ys_ref.shape, F32)


def _experts(tile_lo, tile_hi, tile_used, xs, wr, br, w_gu, w_down, layer, *, d, n_sub=2):
    n_tiles = tile_lo.shape[0]
    de = w_down.shape[2]
    blk = n_sub * MOE_TILE * SUBLANES
    const = lambda shape: pl.BlockSpec(shape, lambda i, ea, eb, us: (0,) * len(shape))
    weight_specs, weights = [], []
    for t in range(n_sub):
        for which in range(2):
            pick = lambda i, ea, eb, us, t=t, which=which: (layer, (eb if which else ea)[i * n_sub + t], 0, 0)
            weight_specs += [pl.BlockSpec((1, 1, d, 2 * de), pick), pl.BlockSpec((1, 1, de, d), pick)]
            weights += [w_gu, w_down]
    grid_spec = pltpu.PrefetchScalarGridSpec(
        num_scalar_prefetch=3,
        grid=(n_tiles // n_sub,),
        in_specs=[pl.BlockSpec((blk, LANES), lambda i, ea, eb, us: (i, 0)), const((d, LANES)), const((1, LANES))]
                 + weight_specs,
        out_specs=pl.BlockSpec((blk, LANES), lambda i, ea, eb, us: (i, 0)))
    return pl.pallas_call(
        functools.partial(_experts_kernel, tile=MOE_TILE, d=d, d_expert=de, n_sub=n_sub),
        grid_spec=grid_spec,
        out_shape=jax.ShapeDtypeStruct((n_tiles * MOE_TILE * SUBLANES, LANES), F32),
        compiler_params=_cparams(("arbitrary",)),
        name="moe_experts",
    )(tile_lo, tile_hi, tile_used, xs, wr, br, *weights)


def _gather_start(pos_ref, ys_hbm, buf_ref, sem, n):
    def issue(pair, carry):
        for prio in range(DMA_PRIORITIES):
            r = pair * DMA_PRIORITIES + prio
            pltpu.make_async_copy(_row_tile(ys_hbm, pos_ref[r]), _row_tile(buf_ref, r), sem).start(priority=prio)
        return carry

    lax.fori_loop(0, n // DMA_PRIORITIES, issue, 0, unroll=4)


def _gather_wait(ys_hbm, buf_ref, sem, n):
    pltpu.make_async_copy(ys_hbm.at[pl.ds(0, n * SUBLANES), :], buf_ref, sem).wait()


def _token_major(buf_ref, n, d):
    return jnp.concatenate([buf_ref[pl.ds(c, n, stride=SUBLANES), :] for c in range(d // LANES)], axis=1)


def _gather_ahead(pos_ref, pos_next_ref, ys_hbm, buf_ref, sems, n):
    i = pl.program_id(0)
    slot = lax.rem(i, 2)

    @pl.when(i == 0)
    def _():
        _gather_start(pos_ref, ys_hbm, buf_ref.at[0], sems.at[0], n)

    @pl.when(i + 1 < pl.num_programs(0))
    def _():
        _gather_start(pos_next_ref, ys_hbm, buf_ref.at[1 - slot], sems.at[1 - slot], n)

    _gather_wait(ys_hbm, buf_ref.at[slot], sems.at[slot], n)
    return buf_ref.at[slot]


def _gather_specs(n, n_steps):
    return [pl.BlockSpec((n,), lambda i: (i,), memory_space=pltpu.SMEM),
            pl.BlockSpec((n,), lambda i: (jnp.minimum(i + 1, n_steps - 1),), memory_space=pltpu.SMEM),
            pl.BlockSpec(memory_space=pl.ANY)]


def _combine_kernel(pos_ref, pos_next_ref, ys_hbm, x_ref, fin_ref, out_ref, buf_ref, sems, *, tc, d, final_norm):
    gathered = _gather_ahead(pos_ref, pos_next_ref, ys_hbm, buf_ref, sems, tc)
    out = x_ref[...] + _token_major(gathered, tc, d)
    if final_norm:
        out = _rms(out, fin_ref[...])
    out_ref[...] = out


def _combine(pos, ys, x1, fin_g, *, tc, final_norm):
    t, d = x1.shape
    return pl.pallas_call(
        functools.partial(_combine_kernel, tc=tc, d=d, final_norm=final_norm),
        grid=(t // tc,),
        in_specs=_gather_specs(tc, t // tc) + [pl.BlockSpec((tc, d), lambda i: (i, 0)),
                                               pl.BlockSpec((1, d), lambda i: (0, 0))],
        out_specs=pl.BlockSpec((tc, d), lambda i: (i, 0)),
        out_shape=jax.ShapeDtypeStruct((t, d), F32),
        scratch_shapes=[pltpu.VMEM((2, tc * SUBLANES, LANES), F32), pltpu.SemaphoreType.DMA((2,))],
        compiler_params=_cparams(("arbitrary",)),
        name="moe_combine",
    )(pos, pos, ys, x1, fin_g)


def _moe_sparse(x1, rows, code, counts, wr, br, w_gu, w_down, layer):
    t, d = x1.shape
    n_tiles = t // MOE_TILE + N_CLASSES
    cnt = counts[0, :N_CLASSES].astype(jnp.int32)
    padded = (cnt + MOE_TILE - 1) // MOE_TILE * MOE_TILE
    ends = jnp.cumsum(padded)
    starts = ends - padded
    n_slots = n_tiles * MOE_TILE
    fill = jnp.concatenate([starts + cnt, ends[-1:], padded - cnt, n_slots - ends[-1:]])
    tile_start = jnp.arange(n_tiles, dtype=jnp.int32) * MOE_TILE
    tile_cls = jnp.minimum(jnp.sum((tile_start[:, None] >= ends[None, :]).astype(jnp.int32), axis=1), N_CLASSES - 1)
    lo_ids, hi_ids = _class_experts()
    onehot = (tile_cls[:, None] == jnp.arange(N_CLASSES, dtype=jnp.int32)[None, :]).astype(jnp.int32)
    tile_lo = jnp.sum(onehot * jnp.asarray(lo_ids, jnp.int32)[None, :], axis=1)
    tile_hi = jnp.sum(onehot * jnp.asarray(hi_ids, jnp.int32)[None, :], axis=1)
    tile_used = (tile_start < ends[-1]).astype(jnp.int32)
    xs, pos = _dispatch(code.reshape(t), starts, fill, rows, n_slots, td=1024)
    ys = _experts(tile_lo, tile_hi, tile_used, xs, wr, br, w_gu, w_down, layer, d=d)
    return ys, pos


def _gm_layer_kernel(pos_ref, pos_next_ref, ys_hbm, x_ref, ng_ref, win_ref, wqm_ref, lng_ref, lnb_ref, wsp_ref,
                     bsp_ref, kv_ref, wo_mix_ref, wo_mem_ref, fg_ref, wr_ref, br_ref,
                     x1_ref, rows_ref, code_ref, cnt_ref, carry_ref, buf_ref, sems, *, tm, n_groups):
    c = CHUNK
    tw = n_groups * HEAD_DIM
    d = x_ref.shape[1]
    gathered = _gather_ahead(pos_ref, pos_next_ref, ys_hbm, buf_ref, sems, tm)
    subs = range(N_SUB)
    sm = tm // N_SUB
    x = [x_ref[t * sm:(t + 1) * sm, :]
         + jnp.concatenate([gathered[pl.ds(t * sm * SUBLANES + cc, sm, stride=SUBLANES), :]
                            for cc in range(d // LANES)], axis=1) for t in subs]
    h = [_rms(x[t], ng_ref[...]).astype(BF16) for t in subs]
    proj = [_dot(h[t], win_ref[...]) for t in subs]
    uv = [0.5 * p * (1.0 + lax.erf(p * (2.0 ** -0.5))) for p in proj]
    vn = []
    for t in subs:
        v = uv[t][:, tw:]
        vc = v - jnp.mean(v, axis=-1, keepdims=True)
        var = jnp.mean(vc * vc, axis=-1, keepdims=True)
        vn.append((vc * lax.rsqrt(var + EPS) * lng_ref[...] + lnb_ref[...]).astype(BF16))
    ii = lax.broadcasted_iota(jnp.int32, (c, c), 0)
    jj = lax.broadcasted_iota(jnp.int32, (c, c), 1)
    tril = ii >= jj
    n_chunks = sm // c
    wc = [jnp.where(tril, wsp_ref[g], jnp.zeros((c, c), BF16)) for g in range(n_groups)]
    mix = []
    for t in subs:
        col_parts = []
        for g in range(n_groups):
            cols = slice(g * HEAD_DIM, (g + 1) * HEAD_DIM)
            wide = _dot(wc[g], jnp.concatenate([vn[t][n * c:(n + 1) * c, cols] for n in range(n_chunks)], axis=1))
            bias = bsp_ref[g]
            col_parts.append(jnp.concatenate(
                [wide[:, n * HEAD_DIM:(n + 1) * HEAD_DIM] + bias for n in range(n_chunks)], axis=0))
        mix.append((uv[t][:, :tw] * jnp.concatenate(col_parts, axis=1)).astype(BF16))
    kv = kv_ref[0]
    qm = [_dot(h[t], wqm_ref[...]) for t in subs]
    cross = [_cross_attn(qm[t], kv[:, :MEM_WIDTH], kv[:, MEM_WIDTH:]).astype(BF16) for t in subs]
    _mixer_tail(x, mix, cross, wo_mix_ref, wo_mem_ref, fg_ref, wr_ref, br_ref,
                x1_ref, rows_ref, code_ref, cnt_ref, carry_ref)


def _gm_layer(pos, ys, x2d, ng, win, wqm, lng, lnb, wsp, bsp, kv, kv_layer, wo_mix, wo_mem, fg, wr, br, *, tm, seq):
    t, d = x2d.shape
    n_steps = t // tm
    tw = lng.shape[1]
    n_groups = tw // HEAD_DIM
    mlen = kv.shape[1]
    tiles_per_seq = seq // tm
    const = lambda shape: pl.BlockSpec(shape, lambda i: (0,) * len(shape))
    tok = lambda w: pl.BlockSpec((tm, w), lambda i: (i, 0))
    return pl.pallas_call(
        functools.partial(_gm_layer_kernel, tm=tm, n_groups=n_groups),
        grid=(t // tm,),
        in_specs=_gather_specs(tm, n_steps) + [
                  tok(d), const((1, d)), const((d, 2 * tw)), const((d, MEM_WIDTH)), const((1, tw)), const((1, tw)),
                  const((n_groups, CHUNK, CHUNK)), const((n_groups, CHUNK, HEAD_DIM)),
                  pl.BlockSpec((1, mlen, 2 * MEM_WIDTH), lambda i: (i // tiles_per_seq, 0, kv_layer)),
                  const((tw, d)), const((MEM_WIDTH, d)), const((1, d)), const((d, LANES)), const((1, LANES))],
        out_specs=_tail_out_specs(tm, d),
        out_shape=_tail_out_shapes(t, d, tm),
        scratch_shapes=[pltpu.VMEM((1, LANES), F32), pltpu.VMEM((2, tm * SUBLANES, LANES), F32),
                        pltpu.SemaphoreType.DMA((2,))],
        compiler_params=_cparams(("arbitrary",)),
        name="gm_layer",
    )(pos, pos, ys, x2d, ng, win, wqm, lng, lnb, wsp, bsp, kv, wo_mix, wo_mem, fg, wr, br)


def _router_params(w_group, b_group, w_router, b_router):
    d = w_group.shape[0]
    pad = LANES - N_EXPERTS - MOE_GROUPS
    wr = jnp.concatenate([w_router, w_group, jnp.zeros((d, pad), F32)], axis=1).astype(BF16)
    br = jnp.concatenate([b_router, b_group, jnp.zeros((pad,), F32)])[None, :]
    return wr, br


def _col16(vec):
    return jnp.concatenate([vec, jnp.zeros((16 - vec.shape[0],), F32)])[:, None]


def kernel(x, mem, mem_norm_g, mix_norm_g, w_out, w_mem_kv, dn_w_in, dn_conv_w, dn_a_log, dn_dt_bias, dn_o_norm_g,
           gm_w_in, gm_ln_g, gm_ln_b, gm_w_spatial, gm_b_spatial, ffn_norm_g, moe_w_group, moe_b_group,
           moe_w_router, moe_b_router, moe_w_gate, moe_w_up, moe_w_down, final_norm_g):
    b, s, d = x.shape
    tw = d - MEM_WIDTH
    n_heads = tw // HEAD_DIM
    t = b * s

    kv = _memkv(mem, mem_norm_g[None, :], jnp.concatenate([w_mem_kv[0], w_mem_kv[1]], axis=1).astype(BF16))

    w_in = dn_w_in[0]
    o1, o2, o3, o4 = 3 * tw, 4 * tw, 4 * tw + n_heads, 4 * tw + 2 * n_heads
    wabt = jnp.concatenate([w_in[:, o2:o4].T, jnp.zeros((16 - 2 * n_heads, d), F32)], axis=0).astype(BF16)
    q, k, v, zg, gb, cross = _dn_front(
        x, mix_norm_g[0][None, :], w_in[:, :o1].astype(BF16), w_in[:, o1:o2].astype(BF16),
        w_in[:, o4:].astype(BF16), wabt, dn_conv_w[0], _col16(dn_a_log[0]), _col16(dn_dt_bias[0]), kv, tm=512)
    u, wp, at, qg, kts, egl = _dn_intra(q, k, v, gb, tb=min(2048, s))
    o = _dn_scan(qg, k, u, wp, at, kts, egl, tb=512, n_batch=2 if b % 2 == 0 else 1)
    wr0, br0 = _router_params(moe_w_group[0], moe_b_group[0], moe_w_router[0], moe_b_router[0])
    wo0 = w_out[0].astype(BF16)
    x1, rows0, code0, cnt0 = _dn_out(
        x.reshape(t, d), o.reshape(t, tw), zg.reshape(t, tw), cross.reshape(t, MEM_WIDTH),
        dn_o_norm_g[0][None, :], wo0[:tw], wo0[tw:], ffn_norm_g[0][None, :], wr0, br0, tm=1024)
    w_gu = jnp.concatenate([moe_w_gate, moe_w_up], axis=3).astype(BF16)
    w_down = moe_w_down.astype(BF16)
    ys0, pos0 = _moe_sparse(x1, rows0, code0, cnt0, wr0, br0, w_gu, w_down, 0)

    win1 = gm_w_in[0]
    wr1, br1 = _router_params(moe_w_group[1], moe_b_group[1], moe_w_router[1], moe_b_router[1])
    wo1 = w_out[1].astype(BF16)
    bsp = jnp.broadcast_to(gm_b_spatial[0][:, :, None], (n_heads, CHUNK, HEAD_DIM))
    x3, rows1, code1, cnt1 = _gm_layer(
        pos0, ys0, x1, mix_norm_g[1][None, :], win1[:, :2 * tw].astype(BF16), win1[:, 2 * tw:].astype(BF16),
        gm_ln_g[0][None, :], gm_ln_b[0][None, :], gm_w_spatial[0].astype(BF16), bsp, kv, 1,
        wo1[:tw], wo1[tw:], ffn_norm_g[1][None, :], wr1, br1, tm=1024, seq=s)
    ys1, pos1 = _moe_sparse(x3, rows1, code1, cnt1, wr1, br1, w_gu, w_down, 1)
    out = _combine(pos1, ys1, x3, final_norm_g[None, :], tc=512, final_norm=True)
    return out.reshape(b, s, d)
```

```python
import functools

import jax
import jax.numpy as jnp
from jax import lax
from jax.experimental import pallas as pl
from jax.experimental.pallas import tpu as pltpu

F32 = jnp.float32
BF16 = jnp.bfloat16
EPS = 1e-6

MEM_HEADS = 4
MEM_HEAD_DIM = 64
MEM_WIDTH = MEM_HEADS * MEM_HEAD_DIM
HEAD_DIM = 128
CONV_K = 4
CHUNK = 128
MOE_GROUPS = 4
EXPERTS_PER_GROUP = 4
N_EXPERTS = MOE_GROUPS * EXPERTS_PER_GROUP
PAIRS_PER_GROUP = EXPERTS_PER_GROUP * (EXPERTS_PER_GROUP - 1) // 2
N_CLASSES = MOE_GROUPS * PAIRS_PER_GROUP
MOE_TILE = 256
LANES = 128
SUBLANES = 8
RANK_BITS = 16
DMA_PRIORITIES = 2
N_SUB = 2
VMEM_LIMIT = 56 * 1024 * 1024

NT_DIMS = (((1,), (1,)), ((), ()))
NEG_LOG2_E = -1.4426950408889634


def _dot(a, b):
    return jnp.dot(a, b, preferred_element_type=F32)


def _dot_nt(a, b):
    return lax.dot_general(a, b, NT_DIMS, preferred_element_type=F32)


def _rms(x, g):
    return x * lax.rsqrt(jnp.mean(x * x, axis=-1, keepdims=True) + EPS) * g


def _sigmoid(x):
    return 1.0 / (1.0 + jnp.exp2(x * NEG_LOG2_E))


def _softplus(x):
    return jnp.maximum(x, 0.0) + jnp.log1p(jnp.exp(-jnp.abs(x)))


def _split3(x):
    hi = x.astype(BF16)
    r = x - hi.astype(F32)
    mid = r.astype(BF16)
    lo = (r - mid.astype(F32)).astype(BF16)
    return hi, mid, lo


def _cparams(sem):
    return pltpu.CompilerParams(dimension_semantics=sem, vmem_limit_bytes=VMEM_LIMIT)


def _memkv_kernel(mem_ref, g_ref, w_ref, kv_ref):
    mn = _rms(mem_ref[0], g_ref[...]).astype(BF16)
    kv_ref[0] = _dot(mn, w_ref[...]).astype(BF16)


def _memkv(mem, g, w_all):
    b, m, d = mem.shape
    n = w_all.shape[1]
    return pl.pallas_call(
        _memkv_kernel,
        grid=(b,),
        in_specs=[pl.BlockSpec((1, m, d), lambda i: (i, 0, 0)),
                  pl.BlockSpec((1, d), lambda i: (0, 0)),
                  pl.BlockSpec((d, n), lambda i: (0, 0))],
        out_specs=pl.BlockSpec((1, m, n), lambda i: (i, 0, 0)),
        out_shape=jax.ShapeDtypeStruct((b, m, n), BF16),
        compiler_params=_cparams(("parallel",)),
        name="memkv",
    )(mem, g, w_all)


def _cross_attn(qm, kmem, vmem):
    lane = lax.broadcasted_iota(jnp.int32, (1, MEM_WIDTH), 1)
    out = jnp.zeros(qm.shape, F32)
    for h in range(MEM_HEADS):
        msk = (lane >= h * MEM_HEAD_DIM) & (lane < (h + 1) * MEM_HEAD_DIM)
        qh = jnp.where(msk, qm, 0.0).astype(BF16)
        s = _dot_nt(qh, kmem) * (MEM_HEAD_DIM ** -0.5)
        p = jnp.exp(s - jnp.max(s, axis=-1, keepdims=True))
        inv = 1.0 / jnp.sum(p, axis=-1, keepdims=True)
        vh = jnp.where(msk, vmem, jnp.zeros_like(vmem))
        out = out + _dot(p.astype(BF16), vh) * inv
    return out


def _router_class(logits):
    lane = lax.broadcasted_iota(jnp.int32, logits.shape, 1).astype(F32)
    neg = jnp.float32(-jnp.inf)
    big = jnp.float32(1 << 20)
    first = lambda hit: jnp.min(jnp.where(hit, lane, big), axis=-1, keepdims=True)
    is_g = (lane >= N_EXPERTS) & (lane < N_EXPERTS + MOE_GROUPS)
    gl = jnp.where(is_g, logits, neg)
    g_idx = first(gl == jnp.max(gl, axis=-1, keepdims=True)) - N_EXPERTS
    lo = g_idx * EXPERTS_PER_GROUP
    in_grp = (lane >= lo) & (lane < lo + EXPERTS_PER_GROUP)
    el = jnp.where(in_grp, logits, neg)
    i1 = first(el == jnp.max(el, axis=-1, keepdims=True))
    el2 = jnp.where(lane == i1, neg, el)
    i2 = first(el2 == jnp.max(el2, axis=-1, keepdims=True))
    e_lo = jnp.minimum(i1, i2) - lo
    e_hi = jnp.maximum(i1, i2) - lo
    pair_off = jnp.where(e_lo == 0.0, 0.0, jnp.where(e_lo == 1.0, 3.0, 5.0))
    return (g_idx * PAIRS_PER_GROUP + pair_off + e_hi - e_lo - 1.0).astype(jnp.int32)


def _pair_gates(logits, e_lo, e_hi):
    lane = lax.broadcasted_iota(jnp.int32, logits.shape, 1)
    pick = lambda idx: jnp.sum(jnp.where(lane == idx, logits, 0.0), axis=-1, keepdims=True)
    l_lo, l_hi = pick(e_lo), pick(e_hi)
    g_sel = pick(N_EXPERTS + e_lo // EXPERTS_PER_GROUP)
    is_g = (lane >= N_EXPERTS) & (lane < N_EXPERTS + MOE_GROUPS)
    p_grp = 1.0 / jnp.sum(jnp.where(is_g, jnp.exp(logits - g_sel), 0.0), axis=-1, keepdims=True)
    return p_grp / (1.0 + jnp.exp(l_hi - l_lo)), p_grp / (1.0 + jnp.exp(l_lo - l_hi))


def _class_experts():
    lo_ids, hi_ids = [], []
    for g in range(MOE_GROUPS):
        for a in range(EXPERTS_PER_GROUP):
            for b in range(a + 1, EXPERTS_PER_GROUP):
                lo_ids.append(g * EXPERTS_PER_GROUP + a)
                hi_ids.append(g * EXPERTS_PER_GROUP + b)
    return lo_ids, hi_ids


def _dn_front_kernel(x_ref, ng_ref, wqkv_ref, wz_ref, wqm_ref, wabt_ref, convw_ref, alog_ref, dtb_ref, kv_ref,
                     q_ref, k_ref, v_ref, zg_ref, gb_ref, cross_ref, cbuf_ref, *, tm, n_heads):
    j = pl.program_id(1)
    tw = n_heads * HEAD_DIM
    h = _rms(x_ref[0], ng_ref[...]).astype(BF16)

    @pl.when(j == 0)
    def _():
        cbuf_ref[0:8, :] = jnp.zeros((8, 3 * tw), F32)

    gw = 2 * HEAD_DIM
    out_refs = (q_ref, k_ref, v_ref)
    scales = (HEAD_DIM ** -0.5, 1.0, None)
    for grp in range(3 * tw // gw):
        cols = slice(grp * gw, (grp + 1) * gw)
        pre = _dot(h, wqkv_ref[:, cols])
        cbuf_ref[8:8 + tm, cols] = pre
        w = convw_ref[:, cols]
        acc = w[CONV_K - 1:CONV_K, :] * pre
        for kk in range(CONV_K - 1):
            acc = acc + w[kk:kk + 1, :] * cbuf_ref[pl.ds(8 - (CONV_K - 1) + kk, tm), cols]
        cbuf_ref[0:8, cols] = cbuf_ref[tm:tm + 8, cols]
        act = acc * _sigmoid(acc)
        which, first = divmod(grp * gw, tw)
        for half in range(gw // HEAD_DIM):
            a = act[:, half * HEAD_DIM:(half + 1) * HEAD_DIM]
            if scales[which] is not None:
                a = a * (lax.rsqrt(jnp.sum(a * a, axis=-1, keepdims=True) + EPS) * scales[which])
            dst = first + half * HEAD_DIM
            out_refs[which][0, :, dst:dst + HEAD_DIM] = a.astype(BF16)

    z = _dot(h, wz_ref[...])
    zg_ref[0] = (z * _sigmoid(z)).astype(BF16)

    abt = _dot_nt(wabt_ref[...], h)
    g_t = -jnp.exp(alog_ref[...]) * _softplus(abt + dtb_ref[...])
    b_t = _sigmoid(abt)
    for hd in range(n_heads):
        gb_ref[0, hd, 0:1, :] = g_t[hd:hd + 1, :]
        gb_ref[0, hd, 1:2, :] = b_t[n_heads + hd:n_heads + hd + 1, :]

    qm = _dot(h, wqm_ref[...])
    kv = kv_ref[0]
    cross_ref[0] = _cross_attn(qm, kv[:, :MEM_WIDTH], kv[:, MEM_WIDTH:]).astype(BF16)


def _dn_front(x, ng, wqkv, wz, wqm, wabt, convw, alog, dtb, kv, *, tm):
    b, s, d = x.shape
    tw = wz.shape[1]
    n_heads = tw // HEAD_DIM
    mlen = kv.shape[1]
    const = lambda shape: pl.BlockSpec(shape, lambda i, j: (0,) * len(shape))
    tok = lambda w: pl.BlockSpec((1, tm, w), lambda i, j: (i, j, 0))
    return pl.pallas_call(
        functools.partial(_dn_front_kernel, tm=tm, n_heads=n_heads),
        grid=(b, s // tm),
        in_specs=[tok(d), const((1, d)), const((d, 3 * tw)), const((d, tw)), const((d, MEM_WIDTH)),
                  const((16, d)), const((CONV_K, 3 * tw)), const((16, 1)), const((16, 1)),
                  pl.BlockSpec((1, mlen, 2 * MEM_WIDTH), lambda i, j: (i, 0, 0))],
        out_specs=[tok(tw), tok(tw), tok(tw), tok(tw),
                   pl.BlockSpec((1, n_heads, 2, tm), lambda i, j: (i, 0, 0, j)),
                   tok(MEM_WIDTH)],
        out_shape=[jax.ShapeDtypeStruct((b, s, tw), BF16)] * 4
                  + [jax.ShapeDtypeStruct((b, n_heads, 2, s), F32),
                     jax.ShapeDtypeStruct((b, s, MEM_WIDTH), BF16)],
        scratch_shapes=[pltpu.VMEM((tm + 8, 3 * tw), F32)],
        compiler_params=_cparams(("parallel", "arbitrary")),
        name="dn_front",
    )(x, ng, wqkv, wz, wqm, wabt, convw, alog, dtb, kv)


def _cat_lanes(a, b):
    return jnp.concatenate([a, b], axis=1)


def _block_diag(a, b):
    z = jnp.zeros(a.shape, a.dtype)
    return jnp.concatenate([_cat_lanes(a, z), _cat_lanes(z, b)], axis=0)


def _pair_dot(a0, a1, b0, b1):
    w = a0.shape[1]
    r = _dot(_cat_lanes(a0, a1), _block_diag(b0, b1))
    return r[:, :w], r[:, w:]


def _tri_inverse_pairs(a_pairs, ii, jj):
    c = CHUNK
    eye = (ii == jj).astype(F32)
    blk = (ii >> 4) == (jj >> 4)
    n0 = [[jnp.where(blk, -a, 0.0) for a in pair] for pair in a_pairs]
    x = [[eye + n for n in pair] for pair in n0]
    nb = [[n.astype(BF16) for n in pair] for pair in n0]
    nb = [[r.astype(BF16) for r in _pair_dot(p[0], p[1], p[0], p[1])] for p in nb]
    for _ in range(2):
        r = [[_dot(n, _cat_lanes(n, xi.astype(BF16))) for n, xi in zip(pn, px)] for pn, px in zip(nb, x)]
        nb = [[ri[:, :c].astype(BF16) for ri in pr] for pr in r]
        x = [[xi + ri[:, c:] for xi, ri in zip(px, pr)] for px, pr in zip(x, r)]
    fin = [_pair_dot(pn[0], pn[1], px[0].astype(BF16), px[1].astype(BF16)) for pn, px in zip(nb, x)]
    x = [[xi + fi for xi, fi in zip(px, pf)] for px, pf in zip(x, fin)]
    shift = 4
    while (1 << shift) < c:
        inner = (ii >> shift) == (jj >> shift)
        outer = (ii >> (shift + 1)) == (jj >> (shift + 1))
        sel = outer & jnp.logical_not(inner)
        xb = [[xi.astype(BF16) for xi in px] for px in x]
        lo = [[jnp.where(sel, a, 0.0).astype(BF16) for a in pa] for pa in a_pairs]
        lx = [[r.astype(BF16) for r in _pair_dot(pl_[0], pl_[1], pb[0], pb[1])] for pl_, pb in zip(lo, xb)]
        cor = [_pair_dot(pb[0], pb[1], pl_[0], pl_[1]) for pb, pl_ in zip(xb, lx)]
        x = [[xi - ci for xi, ci in zip(px, pc)] for px, pc in zip(x, cor)]
        shift += 1
    return x


def _dn_intra_kernel(q_ref, k_ref, v_ref, gb_ref, u_ref, wp_ref, at_ref, qg_ref, kts_ref, egl_ref, *, n_chunks):
    c = CHUNK
    hw = HEAD_DIM
    ii = lax.broadcasted_iota(jnp.int32, (c, c), 0)
    jj = lax.broadcasted_iota(jnp.int32, (c, c), 1)
    tril = ii >= jj
    strict = ii > jj
    lane2 = lax.broadcasted_iota(jnp.int32, (c, 2 * hw), 1)
    upper_incl = (ii <= jj).astype(BF16)
    chunks = range(n_chunks)
    heads = range(2)
    rows = [slice(n * c, (n + 1) * c) for n in chunks]
    hcol = [slice(hd * hw, (hd + 1) * hw) for hd in heads]
    kp = [k_ref[0, r, :] for r in rows]
    zero = jnp.zeros((c, 2 * hw), BF16)
    kq = [_dot_nt(jnp.concatenate([kp[n], q_ref[0, rows[n], :]], axis=0),
                  jnp.concatenate([jnp.where(lane2 < hw, kp[n], zero), jnp.where(lane2 >= hw, kp[n], zero)], axis=0))
          for n in chunks]
    g_rows = [gb_ref[0, hd, 0:1, rows[n]] for n in chunks for hd in heads]
    hi, mid, lo = _split3(jnp.concatenate(g_rows, axis=0))
    n_rows = len(g_rows)
    cum = _dot(jnp.concatenate([hi, mid, lo], axis=0), upper_incl)
    gc_all = cum[:n_rows] + cum[n_rows:2 * n_rows] + cum[2 * n_rows:]
    a_pairs, b_rs, e_rs = [], [], []
    for n in chunks:
        a_pair, b_pair, e_pair = [], [], []
        for hd in heads:
            g_row = g_rows[n * 2 + hd]
            b_row = gb_ref[0, hd, 1:2, rows[n]]
            gc_row = gc_all[n * 2 + hd:n * 2 + hd + 1, :]
            gl = jnp.sum(g_row, axis=-1, keepdims=True)
            g_r = jnp.broadcast_to(gc_row, (c, c))
            g_c = g_r.T
            b_r = jnp.broadcast_to(b_row, (c, c))
            decay = jnp.where(tril, jnp.exp(g_c - g_r), 0.0)
            kqh = kq[n][:, hd * c:(hd + 1) * c]
            a_pair.append(jnp.where(strict, kqh[:c] * decay, 0.0) * b_r.T)
            at_ref[0, rows[n], hcol[hd]] = jnp.where(tril, kqh[c:] * decay, 0.0).astype(BF16)
            qg_ref[0, rows[n], hcol[hd]] = (q_ref[0, rows[n], hcol[hd]].astype(F32) * jnp.exp(g_c)).astype(BF16)
            kh = kp[n][:, hcol[hd]].astype(F32)
            kts_ref[0, hcol[hd], rows[n]] = (kh.T * jnp.exp(gl - gc_row)).astype(BF16)
            egl_ref[0, n, :, hcol[hd]] = jnp.broadcast_to(jnp.exp(gl), (1, hw))
            b_pair.append(b_r)
            e_pair.append(jnp.exp(g_r))
        a_pairs.append(a_pair)
        b_rs.append(b_pair)
        e_rs.append(e_pair)
    t = _tri_inverse_pairs(a_pairs, ii, jj)
    tb = [[t[n][hd] * b_rs[n][hd] for hd in heads] for n in chunks]
    u = [_pair_dot(tb[n][0].astype(BF16), tb[n][1].astype(BF16),
                   v_ref[0, rows[n], hcol[0]], v_ref[0, rows[n], hcol[1]]) for n in chunks]
    for n in chunks:
        for hd in heads:
            u_ref[0, rows[n], hcol[hd]] = u[n][hd].astype(BF16)
            wp_ref[0, rows[n], hcol[hd]] = (tb[n][hd] * e_rs[n][hd]).astype(BF16)


def _dn_intra(q, k, v, gb, *, tb):
    b, s, tw = q.shape
    n_chunks = tb // CHUNK
    pw = 2 * HEAD_DIM
    tok = pl.BlockSpec((1, tb, pw), lambda i, h, j: (i, j, h))
    return pl.pallas_call(
        functools.partial(_dn_intra_kernel, n_chunks=n_chunks),
        grid=(b, tw // pw, s // tb),
        in_specs=[tok, tok, tok, pl.BlockSpec((1, 2, 2, tb), lambda i, h, j: (i, h, 0, j))],
        out_specs=[tok, tok, tok, tok,
                   pl.BlockSpec((1, pw, tb), lambda i, h, j: (i, h, j)),
                   pl.BlockSpec((1, n_chunks, 1, pw), lambda i, h, j: (i, j, 0, h))],
        out_shape=[jax.ShapeDtypeStruct((b, s, tw), BF16)] * 4
                  + [jax.ShapeDtypeStruct((b, tw, s), BF16),
                     jax.ShapeDtypeStruct((b, s // CHUNK, 1, tw), F32)],
        compiler_params=_cparams(("parallel", "parallel", "parallel")),
        name="dn_intra",
    )(q, k, v, gb)


def _dn_scan_kernel(qg_ref, k_ref, u_ref, wp_ref, at_ref, kts_ref, egl_ref, o_ref, s_ref,
                    *, n_chunks, n_pairs, n_batch):
    c = CHUNK
    pw = 2 * HEAD_DIM

    @pl.when(pl.program_id(1) == 0)
    def _():
        s_ref[...] = jnp.zeros(s_ref.shape, F32)

    items = [(bi, p) for bi in range(n_batch) for p in range(n_pairs)]
    cols = [slice(p * pw, (p + 1) * pw) for p in range(n_pairs)]
    lane = lax.broadcasted_iota(jnp.int32, (c, pw), 1)
    ri = lax.broadcasted_iota(jnp.int32, (pw, pw), 0)
    ci = lax.broadcasted_iota(jnp.int32, (pw, pw), 1)
    same_head = (ri >= HEAD_DIM) == (ci >= HEAD_DIM)
    zero = jnp.zeros((c, pw), BF16)

    def stacked_diag(m):
        return jnp.concatenate([jnp.where(lane < HEAD_DIM, m, zero), jnp.where(lane >= HEAD_DIM, m, zero)], axis=0)

    states = [s_ref[bi * n_pairs + p] for bi, p in items]
    for n in range(n_chunks):
        rows = slice(n * c, (n + 1) * c)
        kqs = [_dot(jnp.concatenate([k_ref[bi, rows, cols[p]], qg_ref[bi, rows, cols[p]]], axis=0), st.astype(BF16))
               for (bi, p), st in zip(items, states)]
        wks = [_dot(wp_ref[bi, rows, cols[p]], stacked_diag(kq[:c].astype(BF16))) for (bi, p), kq in zip(items, kqs)]
        xb = [(u_ref[bi, rows, cols[p]].astype(F32) - wk).astype(BF16) for (bi, p), wk in zip(items, wks)]
        upd = [_dot(kts_ref[bi, cols[p], rows], x) for (bi, p), x in zip(items, xb)]
        states = [st * egl_ref[bi, n, :, cols[p]] + jnp.where(same_head, up, 0.0)
                  for (bi, p), st, up in zip(items, states, upd)]
        intra = [_dot(at_ref[bi, rows, cols[p]], stacked_diag(x)) for (bi, p), x in zip(items, xb)]
        for (bi, p), it, kq in zip(items, intra, kqs):
            o_ref[bi, rows, cols[p]] = (it + kq[c:]).astype(BF16)
    for (bi, p), st in zip(items, states):
        s_ref[bi * n_pairs + p] = st


def _dn_scan(qg, k, u, wp, at, kts, egl, *, tb, n_batch):
    b, s, tw = qg.shape
    n_chunks = tb // CHUNK
    pw = 2 * HEAD_DIM
    tok = pl.BlockSpec((n_batch, tb, tw), lambda i, j: (i, j, 0))
    return pl.pallas_call(
        functools.partial(_dn_scan_kernel, n_chunks=n_chunks, n_pairs=tw // pw, n_batch=n_batch),
        grid=(b // n_batch, s // tb),
        in_specs=[tok, tok, tok, tok, tok,
                  pl.BlockSpec((n_batch, tw, tb), lambda i, j: (i, 0, j)),
                  pl.BlockSpec((n_batch, n_chunks, 1, tw), lambda i, j: (i, j, 0, 0))],
        out_specs=tok,
        out_shape=jax.ShapeDtypeStruct((b, s, tw), BF16),
        scratch_shapes=[pltpu.VMEM((n_batch * (tw // pw), pw, pw), F32)],
        compiler_params=_cparams(("parallel", "arbitrary")),
        name="dn_scan",
    )(qg, k, u, wp, at, kts, egl)


def _mixer_tail(x, mix_b, cross_b, wo_mix_ref, wo_mem_ref, fg_ref, wr_ref, br_ref,
                x1_ref, rows_ref, code_ref, cnt_ref, carry_ref):
    subs = range(len(x))
    sm, d = x[0].shape
    assert d == SUBLANES * LANES
    y = [_dot(mix_b[t], wo_mix_ref[...]) + _dot(cross_b[t], wo_mem_ref[...]) for t in subs]
    x1 = [x[t] + y[t] for t in subs]
    for t in subs:
        x1_ref[t * sm:(t + 1) * sm, :] = x1[t]
    h2 = [_rms(x1[t], fg_ref[...]) for t in subs]
    logits = [_dot(h2[t].astype(BF16), wr_ref[...]) + br_ref[...] for t in subs]
    cls = [_router_class(logits[t]) for t in subs]
    for t in subs:
        for c in range(SUBLANES):
            rows_ref[pl.ds(t * sm * SUBLANES + c, sm, stride=SUBLANES), :] = h2[t][:, c * LANES:(c + 1) * LANES]

    @pl.when(pl.program_id(0) == 0)
    def _():
        carry_ref[...] = jnp.zeros(carry_ref.shape, F32)

    lane = lax.broadcasted_iota(jnp.int32, (sm, LANES), 1)
    ii = lax.broadcasted_iota(jnp.int32, (sm, sm), 0)
    jj = lax.broadcasted_iota(jnp.int32, (sm, sm), 1)
    earlier = (ii > jj).astype(BF16)
    onehot = [lane == cls[t] for t in subs]
    within = [_dot(earlier, onehot[t].astype(BF16)) for t in subs]
    carry = carry_ref[...]
    l8 = lax.broadcasted_iota(jnp.int32, (SUBLANES, LANES), 1)
    s8 = lax.broadcasted_iota(jnp.int32, (SUBLANES, LANES), 0)
    for t in subs:
        rank = jnp.sum(jnp.where(onehot[t], within[t] + carry, 0.0), axis=-1, keepdims=True).astype(jnp.int32)
        carry = carry + jnp.sum(onehot[t].astype(F32), axis=0, keepdims=True)
        cols = jnp.where(lane == 0, cls[t], jnp.where(lane == 1, rank >> 8, jnp.where(lane == 2, rank & 255, 0)))
        picked = _dot_nt((l8 == s8).astype(BF16), cols.astype(F32).astype(BF16)).astype(jnp.int32)
        code_ref[0, :, t * sm:(t + 1) * sm] = (picked[0:1] << RANK_BITS) | (picked[1:2] << 8) | picked[2:3]
    carry_ref[...] = carry
    cnt_ref[...] = carry


def _tail_out_specs(tm, d):
    return [pl.BlockSpec((tm, d), lambda i: (i, 0)),
            pl.BlockSpec((tm * SUBLANES, LANES), lambda i: (i, 0)),
            pl.BlockSpec((1, 1, tm), lambda i: (i, 0, 0)),
            pl.BlockSpec((1, LANES), lambda i: (0, 0))]


def _tail_out_shapes(t, d, tm):
    return [jax.ShapeDtypeStruct((t, d), F32), jax.ShapeDtypeStruct((t * SUBLANES, LANES), F32),
            jax.ShapeDtypeStruct((t // tm, 1, tm), jnp.int32), jax.ShapeDtypeStruct((1, LANES), F32)]


def _dn_out_kernel(x_ref, o_ref, zg_ref, cross_ref, og_ref, wo_mix_ref, wo_mem_ref, fg_ref, wr_ref, br_ref,
                   x1_ref, rows_ref, code_ref, cnt_ref, carry_ref, *, n_heads):
    og = og_ref[...]
    sm = x_ref.shape[0] // N_SUB
    xs, mixes, crosses = [], [], []
    for t in range(N_SUB):
        rows = slice(t * sm, (t + 1) * sm)
        parts = []
        for hd in range(n_heads):
            cols = slice(hd * HEAD_DIM, (hd + 1) * HEAD_DIM)
            oh = o_ref[rows, cols].astype(F32)
            on = oh * lax.rsqrt(jnp.mean(oh * oh, axis=-1, keepdims=True) + EPS) * og
            parts.append((on * zg_ref[rows, cols].astype(F32)).astype(BF16))
        mixes.append(jnp.concatenate(parts, axis=1))
        xs.append(x_ref[rows, :])
        crosses.append(cross_ref[rows, :])
    _mixer_tail(xs, mixes, crosses, wo_mix_ref, wo_mem_ref, fg_ref, wr_ref, br_ref,
                x1_ref, rows_ref, code_ref, cnt_ref, carry_ref)


def _dn_out(x2d, o2d, zg2d, cross2d, og, wo_mix, wo_mem, fg, wr, br, *, tm):
    t, d = x2d.shape
    tw = o2d.shape[1]
    n_heads = tw // HEAD_DIM
    const = lambda shape: pl.BlockSpec(shape, lambda i: (0,) * len(shape))
    tok = lambda w: pl.BlockSpec((tm, w), lambda i: (i, 0))
    return pl.pallas_call(
        functools.partial(_dn_out_kernel, n_heads=n_heads),
        grid=(t // tm,),
        in_specs=[tok(d), tok(tw), tok(tw), tok(MEM_WIDTH), const((1, HEAD_DIM)), const((tw, d)),
                  const((MEM_WIDTH, d)), const((1, d)), const((d, LANES)), const((1, LANES))],
        out_specs=_tail_out_specs(tm, d),
        out_shape=_tail_out_shapes(t, d, tm),
        scratch_shapes=[pltpu.VMEM((1, LANES), F32)],
        compiler_params=_cparams(("arbitrary",)),
        name="dn_out",
    )(x2d, o2d, zg2d, cross2d, og, wo_mix, wo_mem, fg, wr, br)


def _row_tile(ref, r):
    return ref.at[pl.ds(pl.multiple_of(r * SUBLANES, SUBLANES), SUBLANES), :]


def _dispatch_kernel(code_ref, starts_ref, fill_ref, rows_ref, xs_hbm, pos_ref, zero_ref, sem, pad_sem, *, td):
    rank_mask = (1 << RANK_BITS) - 1

    @pl.when(pl.program_id(0) == 0)
    def _():
        zero_ref[...] = jnp.zeros(zero_ref.shape, F32)
        n_fill = fill_ref.shape[0] // 2

        def slots(ref, first, n):
            return ref.at[pl.ds(pl.multiple_of(first * SUBLANES, SUBLANES), n * SUBLANES), :]

        def walk(act):
            for c in range(n_fill):
                first, count = fill_ref[c], fill_ref[n_fill + c]
                n_whole = count // MOE_TILE

                def whole(kk, carry, first=first):
                    act(pltpu.make_async_copy(zero_ref, slots(xs_hbm, first + kk * MOE_TILE, MOE_TILE), pad_sem))
                    return carry

                lax.fori_loop(0, n_whole, whole, 0)
                off = first + n_whole * MOE_TILE
                rem = count - n_whole * MOE_TILE
                bit = MOE_TILE // 2
                while bit:
                    @pl.when((rem & bit) != 0)
                    def _(off=off, bit=bit):
                        act(pltpu.make_async_copy(slots(zero_ref, 0, bit), slots(xs_hbm, off, bit), pad_sem))

                    off = off + (rem & bit)
                    bit //= 2

        walk(lambda cp: cp.start())
        walk(lambda cp: cp.wait())

    def issue(pair, carry):
        for prio in range(DMA_PRIORITIES):
            r = pair * DMA_PRIORITIES + prio
            code = code_ref[r]
            slot = starts_ref[code >> RANK_BITS] + (code & rank_mask)
            pos_ref[r] = slot
            pltpu.make_async_copy(_row_tile(rows_ref, r), _row_tile(xs_hbm, slot), sem).start(priority=prio)
        return carry

    lax.fori_loop(0, td // DMA_PRIORITIES, issue, 0, unroll=4)
    pltpu.make_async_copy(rows_ref, xs_hbm.at[pl.ds(0, td * SUBLANES), :], sem).wait()


def _dispatch(code, starts, fill, rows, n_slots, *, td):
    t = code.shape[0]
    smem_whole = lambda n: pl.BlockSpec((n,), lambda i: (0,), memory_space=pltpu.SMEM)
    return pl.pallas_call(
        functools.partial(_dispatch_kernel, td=td),
        grid=(t // td,),
        in_specs=[pl.BlockSpec((td,), lambda i: (i,), memory_space=pltpu.SMEM),
                  smem_whole(starts.shape[0]), smem_whole(fill.shape[0]),
                  pl.BlockSpec((td * SUBLANES, LANES), lambda i: (i, 0))],
        out_specs=[pl.BlockSpec(memory_space=pl.ANY),
                   pl.BlockSpec((td,), lambda i: (i,), memory_space=pltpu.SMEM)],
        out_shape=[jax.ShapeDtypeStruct((n_slots * SUBLANES, LANES), F32),
                   jax.ShapeDtypeStruct((t,), jnp.int32)],
        scratch_shapes=[pltpu.VMEM((MOE_TILE * SUBLANES, LANES), F32), pltpu.SemaphoreType.DMA(()),
                        pltpu.SemaphoreType.DMA(())],
        compiler_params=_cparams(("arbitrary",)),
        name="moe_dispatch",
    )(code, starts, fill, rows)


def _experts_kernel(ea_ref, eb_ref, used_ref, xs_ref, wr_ref, br_ref, *refs, tile, d, d_expert, n_sub):
    ys_ref = refs[-1]
    i = pl.program_id(0)
    subs = range(n_sub)
    tid = [i * n_sub + t for t in subs]
    base = [t * tile * SUBLANES for t in subs]
    n_used = used_ref[tid[0]]
    for t in subs[1:]:
        n_used = n_used + used_ref[tid[t]]

    @pl.when(n_used > 0)
    def _():
        h = [jnp.concatenate([xs_ref[pl.ds(base[t] + c, tile, stride=SUBLANES), :] for c in range(d // LANES)],
                             axis=1).astype(BF16) for t in subs]
        gates = [_pair_gates(_dot(h[t], wr_ref[...]) + br_ref[...], ea_ref[tid[t]], eb_ref[tid[t]]) for t in subs]
        gu = [[_dot(h[t], refs[4 * t + 2 * e][0, 0]) for e in range(2)] for t in subs]
        parts = [[(g[:, :d_expert] * _sigmoid(g[:, :d_expert]) * g[:, d_expert:] * gates[t][e]).astype(BF16)
                  for e, g in enumerate(gu[t])] for t in subs]
        y = [_dot(parts[t][0], refs[4 * t + 1][0, 0]) + _dot(parts[t][1], refs[4 * t + 3][0, 0]) for t in subs]
        for t in subs:
            for c in range(d // LANES):
                ys_ref[pl.ds(base[t] + c, tile, stride=SUBLANES), :] = y[t][:, c * LANES:(c + 1) * LANES]

    @pl.when(n_used == 0)
    def _():
        ys_ref[...] = jnp.zeros(ys_ref.shape, F32)


def _experts(tile_lo, tile_hi, tile_used, xs, wr, br, w_gu, w_down, layer, *, d, n_sub=2):
    n_tiles = tile_lo.shape[0]
    de = w_down.shape[2]
    blk = n_sub * MOE_TILE * SUBLANES
    const = lambda shape: pl.BlockSpec(shape, lambda i, ea, eb, us: (0,) * len(shape))
    weight_specs, weights = [], []
    for t in range(n_sub):
        for which in range(2):
            pick = lambda i, ea, eb, us, t=t, which=which: (layer, (eb if which else ea)[i * n_sub + t], 0, 0)
            weight_specs += [pl.BlockSpec((1, 1, d, 2 * de), pick), pl.BlockSpec((1, 1, de, d), pick)]
            weights += [w_gu, w_down]
    grid_spec = pltpu.PrefetchScalarGridSpec(
        num_scalar_prefetch=3,
        grid=(n_tiles // n_sub,),
        in_specs=[pl.BlockSpec((blk, LANES), lambda i, ea, eb, us: (i, 0)), const((d, LANES)), const((1, LANES))]
                 + weight_specs,
        out_specs=pl.BlockSpec((blk, LANES), lambda i, ea, eb, us: (i, 0)))
    return pl.pallas_call(
        functools.partial(_experts_kernel, tile=MOE_TILE, d=d, d_expert=de, n_sub=n_sub),
        grid_spec=grid_spec,
        out_shape=jax.ShapeDtypeStruct((n_tiles * MOE_TILE * SUBLANES, LANES), F32),
        compiler_params=_cparams(("arbitrary",)),
        name="moe_experts",
    )(tile_lo, tile_hi, tile_used, xs, wr, br, *weights)


def _gather_start(pos_ref, ys_hbm, buf_ref, sem, n):
    def issue(pair, carry):
        for prio in range(DMA_PRIORITIES):
            r = pair * DMA_PRIORITIES + prio
            pltpu.make_async_copy(_row_tile(ys_hbm, pos_ref[r]), _row_tile(buf_ref, r), sem).start(priority=prio)
        return carry

    lax.fori_loop(0, n // DMA_PRIORITIES, issue, 0, unroll=4)


def _gather_wait(ys_hbm, buf_ref, sem, n):
    pltpu.make_async_copy(ys_hbm.at[pl.ds(0, n * SUBLANES), :], buf_ref, sem).wait()


def _token_major(buf_ref, n, d):
    return jnp.concatenate([buf_ref[pl.ds(c, n, stride=SUBLANES), :] for c in range(d // LANES)], axis=1)


def _gather_ahead(pos_ref, pos_next_ref, ys_hbm, buf_ref, sems, n):
    i = pl.program_id(0)
    slot = lax.rem(i, 2)

    @pl.when(i == 0)
    def _():
        _gather_start(pos_ref, ys_hbm, buf_ref.at[0], sems.at[0], n)

    @pl.when(i + 1 < pl.num_programs(0))
    def _():
        _gather_start(pos_next_ref, ys_hbm, buf_ref.at[1 - slot], sems.at[1 - slot], n)

    _gather_wait(ys_hbm, buf_ref.at[slot], sems.at[slot], n)
    return buf_ref.at[slot]


def _gather_specs(n, n_steps):
    return [pl.BlockSpec((n,), lambda i: (i,), memory_space=pltpu.SMEM),
            pl.BlockSpec((n,), lambda i: (jnp.minimum(i + 1, n_steps - 1),), memory_space=pltpu.SMEM),
            pl.BlockSpec(memory_space=pl.ANY)]


def _combine_kernel(pos_ref, pos_next_ref, ys_hbm, x_ref, fin_ref, out_ref, buf_ref, sems, *, tc, d, final_norm):
    gathered = _gather_ahead(pos_ref, pos_next_ref, ys_hbm, buf_ref, sems, tc)
    out = x_ref[...] + _token_major(gathered, tc, d)
    if final_norm:
        out = _rms(out, fin_ref[...])
    out_ref[...] = out


def _combine(pos, ys, x1, fin_g, *, tc, final_norm):
    t, d = x1.shape
    return pl.pallas_call(
        functools.partial(_combine_kernel, tc=tc, d=d, final_norm=final_norm),
        grid=(t // tc,),
        in_specs=_gather_specs(tc, t // tc) + [pl.BlockSpec((tc, d), lambda i: (i, 0)),
                                               pl.BlockSpec((1, d), lambda i: (0, 0))],
        out_specs=pl.BlockSpec((tc, d), lambda i: (i, 0)),
        out_shape=jax.ShapeDtypeStruct((t, d), F32),
        scratch_shapes=[pltpu.VMEM((2, tc * SUBLANES, LANES), F32), pltpu.SemaphoreType.DMA((2,))],
        compiler_params=_cparams(("arbitrary",)),
        name="moe_combine",
    )(pos, pos, ys, x1, fin_g)


def _moe_sparse(x1, rows, code, counts, wr, br, w_gu, w_down, layer):
    t, d = x1.shape
    n_tiles = t // MOE_TILE + N_CLASSES
    cnt = counts[0, :N_CLASSES].astype(jnp.int32)
    padded = (cnt + MOE_TILE - 1) // MOE_TILE * MOE_TILE
    ends = jnp.cumsum(padded)
    starts = ends - padded
    n_slots = n_tiles * MOE_TILE
    fill = jnp.concatenate([starts + cnt, ends[-1:], padded - cnt, n_slots - ends[-1:]])
    tile_start = jnp.arange(n_tiles, dtype=jnp.int32) * MOE_TILE
    tile_cls = jnp.minimum(jnp.sum((tile_start[:, None] >= ends[None, :]).astype(jnp.int32), axis=1), N_CLASSES - 1)
    lo_ids, hi_ids = _class_experts()
    onehot = (tile_cls[:, None] == jnp.arange(N_CLASSES, dtype=jnp.int32)[None, :]).astype(jnp.int32)
    tile_lo = jnp.sum(onehot * jnp.asarray(lo_ids, jnp.int32)[None, :], axis=1)
    tile_hi = jnp.sum(onehot * jnp.asarray(hi_ids, jnp.int32)[None, :], axis=1)
    tile_used = (tile_start < ends[-1]).astype(jnp.int32)
    xs, pos = _dispatch(code.reshape(t), starts, fill, rows, n_slots, td=2048 if t % 2048 == 0 else 1024)
    ys = _experts(tile_lo, tile_hi, tile_used, xs, wr, br, w_gu, w_down, layer, d=d)
    return ys, pos


def _gm_layer_kernel(pos_ref, pos_next_ref, ys_hbm, x_ref, ng_ref, win_ref, wqm_ref, lng_ref, lnb_ref, wsp_ref,
                     bsp_ref, kv_ref, wo_mix_ref, wo_mem_ref, fg_ref, wr_ref, br_ref,
                     x1_ref, rows_ref, code_ref, cnt_ref, carry_ref, buf_ref, sems, *, tm, n_groups):
    c = CHUNK
    tw = n_groups * HEAD_DIM
    d = x_ref.shape[1]
    gathered = _gather_ahead(pos_ref, pos_next_ref, ys_hbm, buf_ref, sems, tm)
    subs = range(N_SUB)
    sm = tm // N_SUB
    x = [x_ref[t * sm:(t + 1) * sm, :]
         + jnp.concatenate([gathered[pl.ds(t * sm * SUBLANES + cc, sm, stride=SUBLANES), :]
                            for cc in range(d // LANES)], axis=1) for t in subs]
    h = [_rms(x[t], ng_ref[...]).astype(BF16) for t in subs]
    proj = [_dot(h[t], win_ref[...]) for t in subs]
    uv = [0.5 * p * (1.0 + lax.erf(p * (2.0 ** -0.5))) for p in proj]
    vn = []
    for t in subs:
        v = uv[t][:, tw:]
        vc = v - jnp.mean(v, axis=-1, keepdims=True)
        var = jnp.mean(vc * vc, axis=-1, keepdims=True)
        vn.append((vc * lax.rsqrt(var + EPS) * lng_ref[...] + lnb_ref[...]).astype(BF16))
    ii = lax.broadcasted_iota(jnp.int32, (c, c), 0)
    jj = lax.broadcasted_iota(jnp.int32, (c, c), 1)
    tril = ii >= jj
    n_chunks = sm // c
    wc = [jnp.where(tril, wsp_ref[g], jnp.zeros((c, c), BF16)) for g in range(n_groups)]
    mix = []
    for t in subs:
        col_parts = []
        for g in range(n_groups):
            cols = slice(g * HEAD_DIM, (g + 1) * HEAD_DIM)
            wide = _dot(wc[g], jnp.concatenate([vn[t][n * c:(n + 1) * c, cols] for n in range(n_chunks)], axis=1))
            bias = bsp_ref[g]
            col_parts.append(jnp.concatenate(
                [wide[:, n * HEAD_DIM:(n + 1) * HEAD_DIM] + bias for n in range(n_chunks)], axis=0))
        mix.append((uv[t][:, :tw] * jnp.concatenate(col_parts, axis=1)).astype(BF16))
    kv = kv_ref[0]
    qm = [_dot(h[t], wqm_ref[...]) for t in subs]
    cross = [_cross_attn(qm[t], kv[:, :MEM_WIDTH], kv[:, MEM_WIDTH:]).astype(BF16) for t in subs]
    _mixer_tail(x, mix, cross, wo_mix_ref, wo_mem_ref, fg_ref, wr_ref, br_ref,
                x1_ref, rows_ref, code_ref, cnt_ref, carry_ref)


def _gm_layer(pos, ys, x2d, ng, win, wqm, lng, lnb, wsp, bsp, kv, kv_layer, wo_mix, wo_mem, fg, wr, br, *, tm, seq):
    t, d = x2d.shape
    n_steps = t // tm
    tw = lng.shape[1]
    n_groups = tw // HEAD_DIM
    mlen = kv.shape[1]
    tiles_per_seq = seq // tm
    const = lambda shape: pl.BlockSpec(shape, lambda i: (0,) * len(shape))
    tok = lambda w: pl.BlockSpec((tm, w), lambda i: (i, 0))
    return pl.pallas_call(
        functools.partial(_gm_layer_kernel, tm=tm, n_groups=n_groups),
        grid=(t // tm,),
        in_specs=_gather_specs(tm, n_steps) + [
                  tok(d), const((1, d)), const((d, 2 * tw)), const((d, MEM_WIDTH)), const((1, tw)), const((1, tw)),
                  const((n_groups, CHUNK, CHUNK)), const((n_groups, CHUNK, HEAD_DIM)),
                  pl.BlockSpec((1, mlen, 2 * MEM_WIDTH), lambda i: (i // tiles_per_seq, 0, kv_layer)),
                  const((tw, d)), const((MEM_WIDTH, d)), const((1, d)), const((d, LANES)), const((1, LANES))],
        out_specs=_tail_out_specs(tm, d),
        out_shape=_tail_out_shapes(t, d, tm),
        scratch_shapes=[pltpu.VMEM((1, LANES), F32), pltpu.VMEM((2, tm * SUBLANES, LANES), F32),
                        pltpu.SemaphoreType.DMA((2,))],
        compiler_params=_cparams(("arbitrary",)),
        name="gm_layer",
    )(pos, pos, ys, x2d, ng, win, wqm, lng, lnb, wsp, bsp, kv, wo_mix, wo_mem, fg, wr, br)


def _router_params(w_group, b_group, w_router, b_router):
    d = w_group.shape[0]
    pad = LANES - N_EXPERTS - MOE_GROUPS
    wr = jnp.concatenate([w_router, w_group, jnp.zeros((d, pad), F32)], axis=1).astype(BF16)
    br = jnp.concatenate([b_router, b_group, jnp.zeros((pad,), F32)])[None, :]
    return wr, br


def _col16(vec):
    return jnp.concatenate([vec, jnp.zeros((16 - vec.shape[0],), F32)])[:, None]


def kernel(x, mem, mem_norm_g, mix_norm_g, w_out, w_mem_kv, dn_w_in, dn_conv_w, dn_a_log, dn_dt_bias, dn_o_norm_g,
           gm_w_in, gm_ln_g, gm_ln_b, gm_w_spatial, gm_b_spatial, ffn_norm_g, moe_w_group, moe_b_group,
           moe_w_router, moe_b_router, moe_w_gate, moe_w_up, moe_w_down, final_norm_g):
    b, s, d = x.shape
    tw = d - MEM_WIDTH
    n_heads = tw // HEAD_DIM
    t = b * s

    kv = _memkv(mem, mem_norm_g[None, :], jnp.concatenate([w_mem_kv[0], w_mem_kv[1]], axis=1).astype(BF16))

    w_in = dn_w_in[0]
    o1, o2, o3, o4 = 3 * tw, 4 * tw, 4 * tw + n_heads, 4 * tw + 2 * n_heads
    wabt = jnp.concatenate([w_in[:, o2:o4].T, jnp.zeros((16 - 2 * n_heads, d), F32)], axis=0).astype(BF16)
    q, k, v, zg, gb, cross = _dn_front(
        x, mix_norm_g[0][None, :], w_in[:, :o1].astype(BF16), w_in[:, o1:o2].astype(BF16),
        w_in[:, o4:].astype(BF16), wabt, dn_conv_w[0], _col16(dn_a_log[0]), _col16(dn_dt_bias[0]), kv, tm=512)
    u, wp, at, qg, kts, egl = _dn_intra(q, k, v, gb, tb=min(2048, s))
    o = _dn_scan(qg, k, u, wp, at, kts, egl, tb=512, n_batch=2 if b % 2 == 0 else 1)
    wr0, br0 = _router_params(moe_w_group[0], moe_b_group[0], moe_w_router[0], moe_b_router[0])
    wo0 = w_out[0].astype(BF16)
    x1, rows0, code0, cnt0 = _dn_out(
        x.reshape(t, d), o.reshape(t, tw), zg.reshape(t, tw), cross.reshape(t, MEM_WIDTH),
        dn_o_norm_g[0][None, :], wo0[:tw], wo0[tw:], ffn_norm_g[0][None, :], wr0, br0, tm=1024)
    w_gu = jnp.concatenate([moe_w_gate, moe_w_up], axis=3).astype(BF16)
    w_down = moe_w_down.astype(BF16)
    ys0, pos0 = _moe_sparse(x1, rows0, code0, cnt0, wr0, br0, w_gu, w_down, 0)

    win1 = gm_w_in[0]
    wr1, br1 = _router_params(moe_w_group[1], moe_b_group[1], moe_w_router[1], moe_b_router[1])
    wo1 = w_out[1].astype(BF16)
    bsp = jnp.broadcast_to(gm_b_spatial[0][:, :, None], (n_heads, CHUNK, HEAD_DIM))
    x3, rows1, code1, cnt1 = _gm_layer(
        pos0, ys0, x1, mix_norm_g[1][None, :], win1[:, :2 * tw].astype(BF16), win1[:, 2 * tw:].astype(BF16),
        gm_ln_g[0][None, :], gm_ln_b[0][None, :], gm_w_spatial[0].astype(BF16), bsp, kv, 1,
        wo1[:tw], wo1[tw:], ffn_norm_g[1][None, :], wr1, br1, tm=1024, seq=s)
    ys1, pos1 = _moe_sparse(x3, rows1, code1, cnt1, wr1, br1, w_gu, w_down, 1)
    out = _combine(pos1, ys1, x3, final_norm_g[None, :], tc=512, final_norm=True)
    return out.reshape(b, s, d)
```
